```python
import jax, jax.numpy as jnp
from jax import lax
import numpy as np

D_MODEL = 1024
BATCH = 2
SEQ = 8192
DEPTH = 1
DEC_BATCH = 16
DEC_SEQ = 2048
PAST_LEN = 128

MIX_WIDTH = D_MODEL
RET_WIDTH = MIX_WIDTH // 2
RET_HEADS = 4
RET_DK = RET_WIDTH // RET_HEADS
RET_DV = RET_WIDTH // RET_HEADS
RET_CHUNK = 128
MLA_HEADS = 8
MLA_D_NOPE = 64
MLA_D_ROPE = 32
MLA_D_V = (MIX_WIDTH - RET_WIDTH) // MLA_HEADS
MLA_Q_LORA = D_MODEL // 4
MLA_KV_LORA = D_MODEL // 8
MLA_BLOCK_Q = 128
N_EXPERTS = 16
EXPERT_FF = D_MODEL
EC_CAPACITY_FACTOR = 2
ROPE_BASE = 10000.0
LN_EPS = 1e-5
RMS_EPS = 1e-6
DN_ALPHA = (2.0 * DEPTH) ** 0.25
DN_BETA = (8.0 * DEPTH) ** -0.25
IN_COLS = 4 * RET_WIDTH + MLA_Q_LORA + MLA_KV_LORA + MLA_D_ROPE

kernel_name = "hymba_retnet_mla_ec_moe_encoder"


def layer_norm(x, w, b):
    xf = x.astype(jnp.float32)
    mu = jnp.mean(xf, axis=-1, keepdims=True)
    var = jnp.mean(jnp.square(xf - mu), axis=-1, keepdims=True)
    return ((xf - mu) * lax.rsqrt(var + LN_EPS) * w.astype(jnp.float32) + b.astype(jnp.float32)).astype(x.dtype)


def rms_norm(x, w):
    xf = x.astype(jnp.float32)
    ms = jnp.mean(jnp.square(xf), axis=-1, keepdims=True)
    return (xf * lax.rsqrt(ms + RMS_EPS) * w.astype(jnp.float32)).astype(x.dtype)


def rope(x):
    S, d = x.shape[1], x.shape[-1]
    inv = 1.0 / (ROPE_BASE ** (jnp.arange(0, d, 2, dtype=jnp.float32) / d))
    ang = jnp.arange(S, dtype=jnp.float32)[:, None] * inv[None, :]
    shape = (S,) + (1,) * (x.ndim - 3) + (d // 2,)
    cos = jnp.cos(ang).reshape(shape)
    sin = jnp.sin(ang).reshape(shape)
    xf = x.astype(jnp.float32)
    x1, x2 = xf[..., : d // 2], xf[..., d // 2:]
    return jnp.concatenate([x1 * cos - x2 * sin, x2 * cos + x1 * sin], axis=-1).astype(x.dtype)


def retention_one_direction(q, k, v, log_gamma, strict):
    B, H, S, dk = q.shape
    dv = v.shape[-1]
    C = RET_CHUNK
    n = S // C
    qc = q.reshape(B, H, n, C, dk)
    kc = k.reshape(B, H, n, C, dk)
    vc = v.reshape(B, H, n, C, dv)
    pos = jnp.arange(C, dtype=jnp.float32)
    diff = pos[:, None] - pos[None, :]
    mask = (diff > 0) if strict else (diff >= 0)
    decay = jnp.where(mask[None], jnp.exp(log_gamma[:, None, None] * jnp.maximum(diff, 0.0)[None]), 0.0)
    scores = jnp.einsum('bhncd,bhnmd->bhncm', qc, kc) * decay[None, :, None]
    o_intra = jnp.einsum('bhncm,bhnme->bhnce', scores, vc)
    k_w = jnp.exp(log_gamma[:, None] * (C - 1 - pos)[None, :])
    kv = jnp.einsum('bhncd,bhnce->nbhde', kc * k_w[None, :, None, :, None], vc)
    chunk_decay = jnp.exp(log_gamma * C)[None, :, None, None]

    def step(state, kv_c):
        return chunk_decay * state + kv_c, state

    _, prev = lax.scan(step, jnp.zeros((B, H, dk, dv), q.dtype), kv)
    q_w = jnp.exp(log_gamma[:, None] * (pos + 1.0)[None, :])
    o_cross = jnp.einsum('bhncd,nbhde->bhnce', qc * q_w[None, :, None, :, None], prev)
    return (o_intra + o_cross).reshape(B, H, S, dv)


def bidirectional_retention(q, k, v, log_gamma_fwd, log_gamma_bwd):
    fwd = retention_one_direction(q, k, v, log_gamma_fwd, False)
    flip = lambda t: jnp.flip(t, axis=2)
    bwd = flip(retention_one_direction(flip(q), flip(k), flip(v), log_gamma_bwd, True))
    return fwd + bwd


def mla_attention(q_nope, q_rope, k_nope, k_rope, v):
    B, S, H, _ = q_nope.shape
    nb = S // MLA_BLOCK_Q
    scale = (MLA_D_NOPE + MLA_D_ROPE) ** -0.5

    def to_blocks(t):
        return jnp.moveaxis(t.reshape((B, nb, MLA_BLOCK_Q) + t.shape[2:]), 1, 0)

    def block(args):
        qn, qr = args
        s = (jnp.einsum('bqhd,bkhd->bhqk', qn, k_nope)
             + jnp.einsum('bqhd,bkd->bhqk', qr, k_rope)).astype(jnp.float32) * scale
        p = jax.nn.softmax(s, axis=-1).astype(v.dtype)
        return jnp.einsum('bhqk,bkhd->bqhd', p, v)

    out = lax.map(block, (to_blocks(q_nope), to_blocks(q_rope)))
    return jnp.moveaxis(out, 0, 1).reshape(B, S, H * MLA_D_V)


def token_mixer(x, w_in, ret_decay_fwd, ret_decay_bwd, ret_gn_w, mla_q_norm_w, mla_w_uq,
                mla_kv_norm_w, mla_w_ukv, mla_out_norm_w, w_out):
    B, S, _ = x.shape
    proj = x @ w_in
    sizes = [RET_WIDTH, RET_WIDTH, RET_WIDTH, RET_WIDTH, MLA_Q_LORA, MLA_KV_LORA, MLA_D_ROPE]
    offsets = np.cumsum(sizes)[:-1].tolist()
    rq, rk, rv, rg, cq, ckv, kr = jnp.split(proj, offsets, axis=-1)

    rq = rope(rq.reshape(B, S, RET_HEADS, RET_DK))
    rk = rope(rk.reshape(B, S, RET_HEADS, RET_DK)) * (RET_DK ** -0.5)
    rv = rv.reshape(B, S, RET_HEADS, RET_DV)
    to_bhsd = lambda t: jnp.transpose(t, (0, 2, 1, 3)).astype(jnp.float32)
    o = bidirectional_retention(to_bhsd(rq), to_bhsd(rk), to_bhsd(rv),
                                jax.nn.log_sigmoid(ret_decay_fwd.astype(jnp.float32)),
                                jax.nn.log_sigmoid(ret_decay_bwd.astype(jnp.float32)))
    o = jnp.transpose(o, (0, 2, 1, 3))
    mu = jnp.mean(o, axis=-1, keepdims=True)
    var = jnp.mean(jnp.square(o - mu), axis=-1, keepdims=True)
    o = ((o - mu) * lax.rsqrt(var + LN_EPS)).reshape(B, S, RET_WIDTH) * ret_gn_w.astype(jnp.float32)
    ret_out = jax.nn.silu(rg) * o.astype(x.dtype)

    q = (rms_norm(cq, mla_q_norm_w) @ mla_w_uq).reshape(B, S, MLA_HEADS, MLA_D_NOPE + MLA_D_ROPE)
    q_nope, q_rope = q[..., :MLA_D_NOPE], rope(q[..., MLA_D_NOPE:])
    kv = (rms_norm(ckv, mla_kv_norm_w) @ mla_w_ukv).reshape(B, S, MLA_HEADS, MLA_D_NOPE + MLA_D_V)
    k_nope, v = kv[..., :MLA_D_NOPE], kv[..., MLA_D_NOPE:]
    k_rope = rope(kr)
    mla_out = rms_norm(mla_attention(q_nope, q_rope, k_nope, k_rope, v), mla_out_norm_w)

    return jnp.concatenate([ret_out, mla_out], axis=-1) @ w_out


def expert_choice_moe(x2d, w_router, w_gate, w_up, w_down):
    n, D = x2d.shape
    cap = EC_CAPACITY_FACTOR * n // N_EXPERTS
    affinity = jax.nn.softmax((x2d @ w_router).astype(jnp.float32), axis=-1)
    gates, idx = lax.top_k(affinity.T, cap)
    xe = x2d[idx]
    h = jax.nn.silu(jnp.einsum('ecd,edf->ecf', xe, w_gate)) * jnp.einsum('ecd,edf->ecf', xe, w_up)
    ye = jnp.einsum('ecf,efd->ecd', h, w_down) * gates[..., None].astype(x2d.dtype)
    return jnp.zeros_like(x2d).at[idx.reshape(-1)].add(ye.reshape(-1, D))


def encoder_layer(x, w_in, ret_decay_fwd, ret_decay_bwd, ret_gn_w, mla_q_norm_w, mla_w_uq,
                  mla_kv_norm_w, mla_w_ukv, mla_out_norm_w, w_out, ln1_w, ln1_b,
                  w_router, w_gate, w_up, w_down, ln2_w, ln2_b):
    mix = token_mixer(x, w_in, ret_decay_fwd, ret_decay_bwd, ret_gn_w, mla_q_norm_w, mla_w_uq,
                      mla_kv_norm_w, mla_w_ukv, mla_out_norm_w, w_out)
    x = layer_norm(DN_ALPHA * x + mix, ln1_w, ln1_b)
    B, S, D = x.shape
    moe = expert_choice_moe(x.reshape(B * S, D), w_router, w_gate, w_up, w_down).reshape(B, S, D)
    return layer_norm(DN_ALPHA * x + moe, ln2_w, ln2_b)


def setup_inputs(seed: int = 0) -> dict:
    key = jax.random.key(seed)
    ks = jax.random.split(key, 24)
    f32 = jnp.float32
    nrm = lambda k, shape, scale: jax.random.normal(k, shape, f32) * scale
    decay_base = jnp.asarray(np.log(2.0 ** (5 + np.arange(RET_HEADS)) - 1.0), f32)
    return {
        "x_prompt": nrm(ks[0], (BATCH, SEQ, D_MODEL), 1.0),
        "x_sample": nrm(ks[1], (DEC_BATCH, DEC_SEQ, D_MODEL), 1.0),
        "w_in": nrm(ks[2], (DEPTH, D_MODEL, IN_COLS), D_MODEL ** -0.5),
        "ret_decay_fwd": decay_base[None] + nrm(ks[3], (DEPTH, RET_HEADS), 0.1),
        "ret_decay_bwd": decay_base[None] + nrm(ks[4], (DEPTH, RET_HEADS), 0.1),
        "ret_gn_w": 1.0 + nrm(ks[5], (DEPTH, RET_WIDTH), 0.01),
        "mla_q_norm_w": 1.0 + nrm(ks[6], (DEPTH, MLA_Q_LORA), 0.01),
        "mla_w_uq": nrm(ks[7], (DEPTH, MLA_Q_LORA, MLA_HEADS * (MLA_D_NOPE + MLA_D_ROPE)), MLA_Q_LORA ** -0.5),
        "mla_kv_norm_w": 1.0 + nrm(ks[8], (DEPTH, MLA_KV_LORA), 0.01),
        "mla_w_ukv": nrm(ks[9], (DEPTH, MLA_KV_LORA, MLA_HEADS * (MLA_D_NOPE + MLA_D_V)), MLA_KV_LORA ** -0.5),
        "mla_out_norm_w": 1.0 + nrm(ks[10], (DEPTH, MLA_HEADS * MLA_D_V), 0.01),
        "w_out": nrm(ks[11], (DEPTH, MIX_WIDTH, D_MODEL), MIX_WIDTH ** -0.5 * DN_BETA),
        "ln1_w": 1.0 + nrm(ks[12], (DEPTH, D_MODEL), 0.01),
        "ln1_b": nrm(ks[13], (DEPTH, D_MODEL), 0.01),
        "w_router": nrm(ks[14], (DEPTH, D_MODEL, N_EXPERTS), D_MODEL ** -0.5),
        "w_gate": nrm(ks[15], (DEPTH, N_EXPERTS, D_MODEL, EXPERT_FF), D_MODEL ** -0.5),
        "w_up": nrm(ks[16], (DEPTH, N_EXPERTS, D_MODEL, EXPERT_FF), D_MODEL ** -0.5),
        "w_down": nrm(ks[17], (DEPTH, N_EXPERTS, EXPERT_FF, D_MODEL), EXPERT_FF ** -0.5 * DN_BETA),
        "ln2_w": 1.0 + nrm(ks[18], (DEPTH, D_MODEL), 0.01),
        "ln2_b": nrm(ks[19], (DEPTH, D_MODEL), 0.01),
    }


def reference(x_prompt, x_sample, w_in, ret_decay_fwd, ret_decay_bwd, ret_gn_w, mla_q_norm_w, mla_w_uq,
              mla_kv_norm_w, mla_w_ukv, mla_out_norm_w, w_out, ln1_w, ln1_b, w_router, w_gate, w_up,
              w_down, ln2_w, ln2_b):
    y_prompt = x_prompt
    y_sample = x_sample
    for l in range(DEPTH):
        layer_params = (w_in[l], ret_decay_fwd[l], ret_decay_bwd[l], ret_gn_w[l], mla_q_norm_w[l], mla_w_uq[l],
                        mla_kv_norm_w[l], mla_w_ukv[l], mla_out_norm_w[l], w_out[l], ln1_w[l], ln1_b[l],
                        w_router[l], w_gate[l], w_up[l], w_down[l], ln2_w[l], ln2_b[l])
        y_prompt = encoder_layer(y_prompt, *layer_params)
        y_sample = encoder_layer(y_sample, *layer_params)
    return (y_prompt, y_sample)
```

```python
import functools
import math

import numpy as np
import jax
import jax.numpy as jnp
from jax import lax
from jax.experimental import pallas as pl
from jax.experimental.pallas import tpu as pltpu

F32 = jnp.float32
BF16 = jnp.bfloat16

D_MODEL = 1024
RET_WIDTH = 512
RET_HEADS = 4
RET_DK = 128
RET_CHUNK = 128
MLA_HEADS = 8
MLA_D_NOPE = 64
MLA_D_ROPE = 32
MLA_D_V = 64
MLA_Q_LORA = 256
MLA_KV_LORA = 128
N_EXPERTS = 16
EC_CAPACITY_FACTOR = 2
ROPE_BASE = 10000.0
LN_EPS = 1e-5
RMS_EPS = 1e-6
DN_ALPHA = 2.0 ** 0.25

LANES = 128
HEAD_PAD = 128
PROJ_COLS = 4 * RET_WIDTH + MLA_Q_LORA + MLA_KV_LORA + 2 * LANES
VMEM_LIMIT = 56 * 1024 * 1024

TM_PROJ = 512
RET_STEP_CHUNKS = 4
TQ_ATT = 256
TK_ATT = 512
TM_MOE = 256
SLOT_ROWS = 64
ROW_ALIGN = 16
TM_FFN = 512
BLOCKS_PER_TILE = TM_MOE // LANES

_QK_SCALE = (MLA_D_NOPE + MLA_D_ROPE) ** -0.5 * math.log2(math.e)


def _cparams(sem):
    return pltpu.CompilerParams(dimension_semantics=sem, vmem_limit_bytes=VMEM_LIMIT)


def _inproj_kernel(x_ref, w_ref, wq_ref, wkv_ref, qnw_ref, kvnw_ref, cm_ref, sm_ref,
                   rq_ref, rk_ref, rv_ref, rg_ref, q_ref, k_ref, v_ref):
    xb = x_ref[...].astype(BF16)
    proj = jnp.dot(xb, w_ref[...], preferred_element_type=F32)
    rq_ref[...] = proj[:, 0:512]
    rk_ref[...] = proj[:, 512:1024]
    rv_ref[...] = proj[:, 1024:1536]
    rg_ref[...] = proj[:, 1536:2048]
    cm = cm_ref[...]
    sm = sm_ref[...]

    cq = proj[:, 2048:2304]
    cqn = cq * lax.rsqrt(jnp.mean(cq * cq, axis=-1, keepdims=True) + RMS_EPS) * qnw_ref[...]
    qq = jnp.dot(cqn.astype(BF16), wq_ref[...], preferred_element_type=F32)

    ckv = proj[:, 2304:2432]
    ckvn = ckv * lax.rsqrt(jnp.mean(ckv * ckv, axis=-1, keepdims=True) + RMS_EPS) * kvnw_ref[...]
    kv = jnp.dot(ckvn.astype(BF16), wkv_ref[...], preferred_element_type=F32)

    krope = proj[:, 2432:2560] * cm + proj[:, 2560:2688] * sm
    for h in range(MLA_HEADS):
        lo = h * HEAD_PAD
        qh = qq[:, lo:lo + HEAD_PAD] * cm + qq[:, 1024 + lo:1024 + lo + HEAD_PAD] * sm
        q_ref[:, lo:lo + HEAD_PAD] = (qh * _QK_SCALE).astype(BF16)
        k_ref[:, lo:lo + HEAD_PAD] = (kv[:, lo:lo + HEAD_PAD] + krope).astype(BF16)
    v_ref[...] = kv[:, 1024:1536].astype(BF16)


def _inproj(x2d, seq, w_all, wq2, wkv, qnw, kvnw, cm, sm):
    t = x2d.shape[0]
    tm = TM_PROJ
    nseq = seq // tm
    row = lambda i: (i, 0)
    const = lambda i: (0, 0)
    tab = lambda i: (i % nseq, 0)
    outs = [jax.ShapeDtypeStruct((t, RET_WIDTH), F32)] * 4 + [
        jax.ShapeDtypeStruct((t, MLA_HEADS * HEAD_PAD), BF16),
        jax.ShapeDtypeStruct((t, MLA_HEADS * HEAD_PAD), BF16),
        jax.ShapeDtypeStruct((t, MLA_HEADS * MLA_D_V), BF16)]
    return pl.pallas_call(
        _inproj_kernel,
        grid=(t // tm,),
        in_specs=[pl.BlockSpec((tm, D_MODEL), row),
                  pl.BlockSpec((D_MODEL, PROJ_COLS), const),
                  pl.BlockSpec((MLA_Q_LORA, 2 * MLA_HEADS * HEAD_PAD), const),
                  pl.BlockSpec((MLA_KV_LORA, MLA_HEADS * (HEAD_PAD + MLA_D_V)), const),
                  pl.BlockSpec((1, MLA_Q_LORA), const),
                  pl.BlockSpec((1, MLA_KV_LORA), const),
                  pl.BlockSpec((tm, LANES), tab),
                  pl.BlockSpec((tm, LANES), tab)],
        out_specs=[pl.BlockSpec((tm, RET_WIDTH), row)] * 4 + [
            pl.BlockSpec((tm, MLA_HEADS * HEAD_PAD), row),
            pl.BlockSpec((tm, MLA_HEADS * HEAD_PAD), row),
            pl.BlockSpec((tm, MLA_HEADS * MLA_D_V), row)],
        out_shape=outs,
        compiler_params=_cparams(("parallel",)),
        name="inproj",
    )(x2d, w_all, wq2, wkv, qnw, kvnw, cm, sm)


def _log_sigmoid(x):
    return jnp.minimum(x, 0.0) - jnp.log(1.0 + jnp.exp(-jnp.abs(x)))


def _ret_kernel(q_ref, k_ref, v_ref, g_ref, cos_ref, sin_ref, df_ref, db_ref, gnw_ref,
                o_ref, sf_ref, sb_ref, sbst_ref, *, nsteps):
    phase = pl.program_id(2)
    step = pl.program_id(3)
    c = RET_CHUNK
    lgf = _log_sigmoid(df_ref[0])
    lgb = _log_sigmoid(db_ref[0])
    rpos = lax.broadcasted_iota(jnp.int32, (c, c), 0).astype(F32)
    cpos = lax.broadcasted_iota(jnp.int32, (c, c), 1).astype(F32)

    def rope(x, cs, sn):
        return x * cs + pltpu.roll(x, RET_DK // 2, axis=1) * sn

    @pl.when(jnp.logical_and(phase == 0, step == 0))
    def _():
        sb_ref[...] = jnp.zeros_like(sb_ref)

    @pl.when(jnp.logical_and(phase == 1, step == 0))
    def _():
        sf_ref[...] = jnp.zeros_like(sf_ref)

    @pl.when(phase == 0)
    def _():
        kwb = jnp.exp(lgb * rpos)
        cdb = jnp.exp(lgb * float(c))
        kvstep = nsteps - 1 - step
        for g in range(RET_STEP_CHUNKS - 1, -1, -1):
            rows = slice(g * c, (g + 1) * c)
            kr = rope(k_ref[rows, :], cos_ref[rows, :], sin_ref[rows, :]) * (RET_DK ** -0.5)
            vb = v_ref[rows, :].astype(BF16)
            sbst_ref[kvstep * RET_STEP_CHUNKS + g] = sb_ref[...].astype(BF16)
            kvc = lax.dot_general((kr * kwb).astype(BF16), vb, (((0,), (0,)), ((), ())),
                                  preferred_element_type=F32)
            sb_ref[...] = cdb * sb_ref[...] + kvc

    @pl.when(phase == 1)
    def _():
        diff = rpos - cpos
        dmat = jnp.where(diff >= 0, jnp.exp(lgf * jnp.maximum(diff, 0.0)),
                         jnp.exp(lgb * jnp.maximum(-diff, 0.0)))
        kwf = jnp.exp(lgf * (float(c - 1) - rpos))
        qwf = jnp.exp(lgf * (rpos + 1.0))
        qwb = jnp.exp(lgb * (float(c) - rpos))
        cdf = jnp.exp(lgf * float(c))
        gnw = gnw_ref[...]
        for g in range(RET_STEP_CHUNKS):
            rows = slice(g * c, (g + 1) * c)
            cs = cos_ref[rows, :]
            sn = sin_ref[rows, :]
            qr = rope(q_ref[rows, :], cs, sn)
            kr = rope(k_ref[rows, :], cs, sn) * (RET_DK ** -0.5)
            vb = v_ref[rows, :].astype(BF16)
            sc = lax.dot_general(qr.astype(BF16), kr.astype(BF16), (((1,), (1,)), ((), ())),
                                 preferred_element_type=F32) * dmat
            o = jnp.dot(sc.astype(BF16), vb, preferred_element_type=F32)
            o += jnp.dot((qr * qwf).astype(BF16), sf_ref[...].astype(BF16), preferred_element_type=F32)
            o += jnp.dot((qr * qwb).astype(BF16), sbst_ref[step * RET_STEP_CHUNKS + g],
                         preferred_element_type=F32)
            kvc = lax.dot_general((kr * kwf).astype(BF16), vb, (((0,), (0,)), ((), ())),
                                  preferred_element_type=F32)
            sf_ref[...] = cdf * sf_ref[...] + kvc
            mu = jnp.mean(o, axis=-1, keepdims=True)
            oc = o - mu
            var = jnp.mean(oc * oc, axis=-1, keepdims=True)
            on = oc * lax.rsqrt(var + LN_EPS) * gnw
            gt = g_ref[rows, :]
            o_ref[rows, :] = (gt / (1.0 + jnp.exp(-gt)) * on).astype(o_ref.dtype)


def _retention(rq, rk, rv, rg, cosr, sinr, dfw, dbw, gnw, batch, seq):
    t = rq.shape[0]
    ts = RET_STEP_CHUNKS * RET_CHUNK
    nsteps = seq // ts
    kvi = lambda p, i: jnp.where(p == 0, nsteps - 1 - i, i)
    qi = lambda p, i: jnp.where(p == 0, 0, i)
    kv_map = lambda b, h, p, i: (b * nsteps + kvi(p, i), h)
    q_map = lambda b, h, p, i: (b * nsteps + qi(p, i), h)
    tab_map = lambda b, h, p, i: (kvi(p, i), 0)
    head3 = lambda b, h, p, i: (h, 0, 0)
    return pl.pallas_call(
        functools.partial(_ret_kernel, nsteps=nsteps),
        grid=(batch, RET_HEADS, 2, nsteps),
        in_specs=[pl.BlockSpec((ts, RET_DK), q_map),
                  pl.BlockSpec((ts, RET_DK), kv_map),
                  pl.BlockSpec((ts, RET_DK), kv_map),
                  pl.BlockSpec((ts, RET_DK), q_map),
                  pl.BlockSpec((ts, RET_DK), tab_map),
                  pl.BlockSpec((ts, RET_DK), tab_map),
                  pl.BlockSpec((1, 1, LANES), head3),
                  pl.BlockSpec((1, 1, LANES), head3),
                  pl.BlockSpec((1, RET_DK), lambda b, h, p, i: (0, h))],
        out_specs=pl.BlockSpec((ts, RET_DK), q_map),
        out_shape=jax.ShapeDtypeStruct((t, RET_WIDTH), BF16),
        scratch_shapes=[pltpu.VMEM((RET_DK, RET_DK), F32),
                        pltpu.VMEM((RET_DK, RET_DK), F32),
                        pltpu.VMEM((seq // RET_CHUNK, RET_DK, RET_DK), BF16)],
        compiler_params=_cparams(("parallel", "parallel", "arbitrary", "arbitrary")),
        name="retention",
    )(rq, rk, rv, rg, cosr, sinr, dfw, dbw, gnw)


def _attn_kernel(q_ref, k_ref, v_ref, o_ref, *, seq):
    nkv = seq // TK_ATT
    lane = lax.broadcasted_iota(jnp.int32, (TQ_ATT, LANES), 1)
    outs = []
    for h in range(2):
        hs = slice(h * HEAD_PAD, (h + 1) * HEAD_PAD)
        qh = q_ref[:, hs]

        def body(j, carry):
            m, l, acc = carry
            ks = pl.multiple_of(j * TK_ATT, TK_ATT)
            kc = k_ref[pl.ds(ks, TK_ATT), hs]
            s = lax.dot_general(qh, kc, (((1,), (1,)), ((), ())), preferred_element_type=F32)
            mn = jnp.maximum(m, jnp.max(s, axis=-1, keepdims=True))
            a = jnp.exp2(m - mn)
            p = jnp.exp2(s - mn)
            l = a * l + jnp.sum(p, axis=-1, keepdims=True)
            pv = jnp.dot(p.astype(BF16), v_ref[pl.ds(ks, TK_ATT), :], preferred_element_type=F32)
            return mn, l, a * acc + pv

        init = (jnp.full((TQ_ATT, 1), -jnp.inf, F32), jnp.zeros((TQ_ATT, 1), F32),
                jnp.zeros((TQ_ATT, LANES), F32))
        _, l, acc = lax.fori_loop(0, nkv, body, init)
        outs.append(acc / l)
    o_ref[...] = jnp.where(lane < MLA_D_V, outs[0], outs[1])


def _attention(q, k, v, batch, seq):
    t = q.shape[0]
    nq = seq // TQ_ATT
    return pl.pallas_call(
        functools.partial(_attn_kernel, seq=seq),
        grid=(batch, MLA_HEADS // 2, nq),
        in_specs=[pl.BlockSpec((TQ_ATT, 2 * HEAD_PAD), lambda b, hp, i: (b * nq + i, hp)),
                  pl.BlockSpec((seq, 2 * HEAD_PAD), lambda b, hp, i: (b, hp)),
                  pl.BlockSpec((seq, 2 * MLA_D_V), lambda b, hp, i: (b, hp))],
        out_specs=pl.BlockSpec((TQ_ATT, 2 * MLA_D_V), lambda b, hp, i: (b * nq + i, hp)),
        out_shape=jax.ShapeDtypeStruct((t, MLA_HEADS * MLA_D_V), F32),
        compiler_params=_cparams(("parallel", "parallel", "arbitrary")),
        name="attention",
    )(q, k, v)


def _layer_norm(z, w, b):
    mu = jnp.mean(z, axis=-1, keepdims=True)
    zc = z - mu
    var = jnp.mean(zc * zc, axis=-1, keepdims=True)
    return zc * lax.rsqrt(var + LN_EPS) * w + b


def _split_bf16(a):
    hi = a.astype(BF16)
    lo = (a - hi.astype(F32)).astype(BF16)
    return hi, lo


def _outproj_kernel(x_ref, ret_ref, att_ref, wo1_ref, wo2_ref, onw_ref, lw_ref, lb_ref, wr_ref,
                    y_ref, lg_ref):
    att = att_ref[...]
    an = att * lax.rsqrt(jnp.mean(att * att, axis=-1, keepdims=True) + RMS_EPS) * onw_ref[...]
    mix = jnp.dot(ret_ref[...], wo1_ref[...], preferred_element_type=F32)
    mix += jnp.dot(an.astype(BF16), wo2_ref[...], preferred_element_type=F32)
    y = _layer_norm(DN_ALPHA * x_ref[...] + mix, lw_ref[...], lb_ref[...])
    y_ref[...] = y
    yh, yl = _split_bf16(y)
    wh, wl = _split_bf16(wr_ref[...])
    nt = (((1,), (1,)), ((), ()))
    lt = lax.dot_general(wh, yh, nt, preferred_element_type=F32)
    lt += lax.dot_general(wh, yl, nt, preferred_element_type=F32)
    lt += lax.dot_general(wl, yh, nt, preferred_element_type=F32)
    for cblk in range(lt.shape[1] // LANES):
        lg_ref[cblk] = lt[:, cblk * LANES:(cblk + 1) * LANES]


def _outproj(x2d, ret, att, wo1, wo2, onw, lw, lb, wrt):
    t = x2d.shape[0]
    tm = TM_PROJ
    row = lambda i: (i, 0)
    const = lambda i: (0, 0)
    return pl.pallas_call(
        _outproj_kernel,
        grid=(t // tm,),
        in_specs=[pl.BlockSpec((tm, D_MODEL), row),
                  pl.BlockSpec((tm, RET_WIDTH), row),
                  pl.BlockSpec((tm, RET_WIDTH), row),
                  pl.BlockSpec((RET_WIDTH, D_MODEL), const),
                  pl.BlockSpec((RET_WIDTH, D_MODEL), const),
                  pl.BlockSpec((1, RET_WIDTH), const),
                  pl.BlockSpec((1, D_MODEL), const),
                  pl.BlockSpec((1, D_MODEL), const),
                  pl.BlockSpec((N_EXPERTS, D_MODEL), const)],
        out_specs=[pl.BlockSpec((tm, D_MODEL), row),
                   pl.BlockSpec((tm // LANES, N_EXPERTS, LANES), lambda i: (i, 0, 0))],
        out_shape=[jax.ShapeDtypeStruct((t, D_MODEL), F32),
                   jax.ShapeDtypeStruct((t // LANES, N_EXPERTS, LANES), F32)],
        compiler_params=_cparams(("parallel",)),
        name="outproj",
    )(x2d, ret, att, wo1, wo2, onw, lw, lb, wrt)


def _route_kernel(lg_ref, rank_ref, gate_ref, cnt_ref, off_ref, tot_ref, blk_ref, boff_ref,
                  *, cap, nb):
    e = N_EXPERTS
    ntiles = nb // BLOCKS_PER_TILE
    lg = lg_ref[...]
    ex = jnp.exp(lg - jnp.max(lg, axis=1, keepdims=True))
    aff = ex / jnp.sum(ex, axis=1, keepdims=True)
    gate_ref[...] = aff
    key = pltpu.bitcast(aff, jnp.int32)

    def count(mask):
        part = jnp.sum(mask.astype(F32), axis=0)
        return jnp.broadcast_to(jnp.sum(part, axis=1, keepdims=True), (e, LANES))

    def bisect(_, lohi):
        lo, hi = lohi
        mid = lo + lax.shift_right_logical(hi - lo, 1)
        ok = count(key >= mid[None]) >= float(cap)
        return jnp.where(ok, mid, lo), jnp.where(ok, hi, mid)

    lo0 = jnp.zeros((e, LANES), jnp.int32)
    hi0 = jnp.full((e, LANES), 0x7F800000, jnp.int32)
    thr, _ = lax.fori_loop(0, 31, bisect, (lo0, hi0))
    gt = key > thr[None]
    eq = key == thr[None]
    need = float(cap) - count(gt)

    ii = lax.broadcasted_iota(jnp.int32, (LANES, LANES), 0)
    jj = lax.broadcasted_iota(jnp.int32, (LANES, LANES), 1)
    upper = (ii <= jj).astype(BF16)
    ones = jnp.ones((LANES, LANES), BF16)

    def block_scan(mask):
        m2 = mask.astype(BF16).reshape(nb * e, LANES)
        incl = jnp.dot(m2, upper, preferred_element_type=F32).reshape(nb, e, LANES)
        tot = jnp.dot(m2, ones, preferred_element_type=F32).reshape(nb, e, LANES)
        return incl, tot

    def leading_excl_scan(src_ref, dst_ref, n):
        def step(j, carry):
            dst_ref[j] = carry
            return carry + src_ref[j]
        return lax.fori_loop(0, n, step, jnp.zeros((e, LANES), F32))

    eqf = eq.astype(F32)
    incl, tot = block_scan(eq)
    blk_ref[...] = tot
    leading_excl_scan(blk_ref, boff_ref, nb)
    eq_before = incl - eqf + boff_ref[...]
    sel = jnp.logical_or(gt, jnp.logical_and(eq, eq_before < need[None]))

    self32 = sel.astype(F32)
    incl, tot = block_scan(sel)
    local = (incl - self32).reshape(ntiles, BLOCKS_PER_TILE, e, LANES)
    tot4 = tot.reshape(ntiles, BLOCKS_PER_TILE, e, LANES)
    ranks = [local[:, 0]]
    run = tot4[:, 0]
    for b in range(1, BLOCKS_PER_TILE):
        ranks.append(local[:, b] + run)
        run = run + tot4[:, b]
    rank = jnp.stack(ranks, axis=1).reshape(nb, e, LANES)
    rank_ref[...] = jnp.where(sel, rank, -1.0)
    cnt_ref[...] = run
    blk_ref[0:ntiles] = jnp.floor((run + float(ROW_ALIGN - 1)) * (1.0 / ROW_ALIGN)) * float(ROW_ALIGN)
    tot_ref[...] = leading_excl_scan(blk_ref, boff_ref, ntiles)
    off_ref[...] = boff_ref[0:ntiles]


def _route(logits, cap):
    nb = logits.shape[0]
    ntiles = nb // BLOCKS_PER_TILE
    big = jax.ShapeDtypeStruct((nb, N_EXPERTS, LANES), F32)
    small = jax.ShapeDtypeStruct((ntiles, N_EXPERTS, LANES), F32)
    return pl.pallas_call(
        functools.partial(_route_kernel, cap=cap, nb=nb),
        out_shape=[big, big, small, small, jax.ShapeDtypeStruct((N_EXPERTS, LANES), F32)],
        scratch_shapes=[pltpu.VMEM((nb, N_EXPERTS, LANES), F32),
                        pltpu.VMEM((nb, N_EXPERTS, LANES), F32)],
        compiler_params=pltpu.CompilerParams(vmem_limit_bytes=VMEM_LIMIT),
        name="route",
    )(logits)


def _groups(cnt):
    return lax.shift_right_logical(cnt + (ROW_ALIGN - 1), 4)


def _compact_kernel(off_ref, cnt_ref, y_ref, rank_ref, xg0_ref, xg_ref, p_ref, st_ref, ost_ref,
                    sem, osem, *, cap_rows):
    del xg0_ref
    i = pl.program_id(0)
    nt = pl.num_programs(0)
    slot = i % 2
    fast_groups = SLOT_ROWS // ROW_ALIGN

    def piece(tile, slt, ex, k):
        base = ex * cap_rows + off_ref[tile * N_EXPERTS + ex] + k * ROW_ALIGN
        return pltpu.make_async_copy(
            st_ref.at[slt, pl.ds(ex * SLOT_ROWS + k * ROW_ALIGN, ROW_ALIGN)],
            xg_ref.at[pl.ds(pl.multiple_of(base, ROW_ALIGN), ROW_ALIGN)],
            sem.at[slt])

    def for_pieces(tile, slt, fn):
        for ex in range(N_EXPERTS):
            ng = _groups(cnt_ref[tile * N_EXPERTS + ex])
            for k in range(fast_groups):
                @pl.when(k < ng)
                def _(ex=ex, k=k):
                    fn(piece(tile, slt, ex, k))

    @pl.when(i >= 2)
    def _():
        for_pieces(i - 2, slot, lambda cp: cp.wait())

    xb = y_ref[...].astype(BF16)
    rank = jnp.concatenate([rank_ref[b] for b in range(BLOCKS_PER_TILE)], axis=1)
    jrow = lax.broadcasted_iota(jnp.int32, (SLOT_ROWS, TM_MOE), 0).astype(F32)
    for ex in range(N_EXPERTS):
        p_ref[ex * SLOT_ROWS:(ex + 1) * SLOT_ROWS, :] = (
            rank[ex:ex + 1, :] == jrow).astype(BF16)
    st_ref[slot] = jnp.dot(p_ref[...], xb, preferred_element_type=F32).astype(BF16)
    for_pieces(i, slot, lambda cp: cp.start())

    for ex in range(N_EXPERTS):
        ng = _groups(cnt_ref[i * N_EXPERTS + ex])
        nch = lax.shift_right_logical(ng + (fast_groups - 1), 2)

        def chunk(c, carry, ex=ex, ng=ng):
            pc = (rank[ex:ex + 1, :] == jrow + (c * SLOT_ROWS).astype(F32)).astype(BF16)
            ost_ref[...] = jnp.dot(pc, xb, preferred_element_type=F32).astype(BF16)
            for k in range(fast_groups):
                @pl.when(c * fast_groups + k < ng)
                def _(k=k):
                    base = (ex * cap_rows + off_ref[i * N_EXPERTS + ex]
                            + c * SLOT_ROWS + k * ROW_ALIGN)
                    cp = pltpu.make_async_copy(
                        ost_ref.at[pl.ds(k * ROW_ALIGN, ROW_ALIGN)],
                        xg_ref.at[pl.ds(pl.multiple_of(base, ROW_ALIGN), ROW_ALIGN)],
                        osem.at[0])
                    cp.start()
                    cp.wait()
            return carry

        lax.fori_loop(1, nch, chunk, 0)

    @pl.when(i == nt - 1)
    def _():
        @pl.when(i >= 1)
        def _():
            for_pieces(i - 1, 1 - slot, lambda cp: cp.wait())
        for_pieces(i, slot, lambda cp: cp.wait())


def _compact(off, cnt, y1, rank_em, cap_rows):
    n = y1.shape[0]
    ntiles = n // TM_MOE
    xg0 = jnp.zeros((N_EXPERTS * cap_rows, D_MODEL), BF16)
    gs = pltpu.PrefetchScalarGridSpec(
        num_scalar_prefetch=2,
        grid=(ntiles,),
        in_specs=[pl.BlockSpec((TM_MOE, D_MODEL), lambda i, o, c: (i, 0)),
                  pl.BlockSpec((BLOCKS_PER_TILE, N_EXPERTS, LANES), lambda i, o, c: (i, 0, 0)),
                  pl.BlockSpec(memory_space=pl.ANY)],
        out_specs=pl.BlockSpec(memory_space=pl.ANY),
        scratch_shapes=[pltpu.VMEM((N_EXPERTS * SLOT_ROWS, TM_MOE), BF16),
                        pltpu.VMEM((2, N_EXPERTS * SLOT_ROWS, D_MODEL), BF16),
                        pltpu.VMEM((SLOT_ROWS, D_MODEL), BF16),
                        pltpu.SemaphoreType.DMA((2,)),
                        pltpu.SemaphoreType.DMA((1,))])
    return pl.pallas_call(
        functools.partial(_compact_kernel, cap_rows=cap_rows),
        grid_spec=gs,
        out_shape=jax.ShapeDtypeStruct((N_EXPERTS * cap_rows, D_MODEL), BF16),
        input_output_aliases={4: 0},
        compiler_params=_cparams(("arbitrary",)),
        name="compact",
    )(off, cnt, y1, rank_em, xg0)


def _ffn_kernel(nt_ref, x_ref, wg_ref, wu_ref, wd_ref, y_ref):
    ex = pl.program_id(0)
    j = pl.program_id(1)

    @pl.when(j < nt_ref[ex])
    def _():
        x = x_ref[...]
        hg = jnp.dot(x, wg_ref[0], preferred_element_type=F32)
        hu = jnp.dot(x, wu_ref[0], preferred_element_type=F32)
        h = (hg / (1.0 + jnp.exp(-hg)) * hu).astype(BF16)
        y_ref[...] = jnp.dot(h, wd_ref[0], preferred_element_type=F32).astype(BF16)

    @pl.when(j >= nt_ref[ex])
    def _():
        y_ref[...] = jnp.zeros_like(y_ref)


def _ffn(ntile, xg, wg, wu, wd, cap_rows):
    nt = cap_rows // TM_FFN
    xmap = lambda ex, j, ntr: (ex * nt + jnp.minimum(j, jnp.maximum(ntr[ex] - 1, 0)), 0)
    wmap = lambda ex, j, ntr: (ex, 0, 0)
    gs = pltpu.PrefetchScalarGridSpec(
        num_scalar_prefetch=1,
        grid=(N_EXPERTS, nt),
        in_specs=[pl.BlockSpec((TM_FFN, D_MODEL), xmap),
                  pl.BlockSpec((1, D_MODEL, D_MODEL), wmap),
                  pl.BlockSpec((1, D_MODEL, D_MODEL), wmap),
                  pl.BlockSpec((1, D_MODEL, D_MODEL), wmap)],
        out_specs=pl.BlockSpec((TM_FFN, D_MODEL), lambda ex, j, ntr: (ex * nt + j, 0)))
    return pl.pallas_call(
        _ffn_kernel,
        grid_spec=gs,
        out_shape=jax.ShapeDtypeStruct((N_EXPERTS * cap_rows, D_MODEL), BF16),
        compiler_params=_cparams(("arbitrary", "arbitrary")),
        name="ffn",
    )(ntile, xg, wg, wu, wd)


def _combine_kernel(off_ref, cnt_ref, y1_ref, rank_ref, gate_ref, yh_ref, lw_ref, lb_ref,
                    o_ref, ybuf, obuf, acc_ref, sem, osem, *, cap_rows):
    i = pl.program_id(0)
    nt = pl.num_programs(0)
    slot = i % 2

    def fetch(tile, slt, fn):
        for ex in range(N_EXPERTS):
            base = ex * cap_rows + off_ref[tile * N_EXPERTS + ex]
            fn(pltpu.make_async_copy(
                yh_ref.at[pl.ds(pl.multiple_of(base, ROW_ALIGN), SLOT_ROWS)],
                ybuf.at[slt, pl.ds(ex * SLOT_ROWS, SLOT_ROWS)],
                sem.at[slt]))

    @pl.when(i == 0)
    def _():
        fetch(0, 0, lambda cp: cp.start())

    @pl.when(i + 1 < nt)
    def _():
        fetch(i + 1, 1 - slot, lambda cp: cp.start())

    rank = rank_ref[...]
    gate = gate_ref[...]
    er = lax.broadcasted_iota(jnp.int32, (N_EXPERTS, N_EXPERTS * SLOT_ROWS), 0)
    ec = lax.broadcasted_iota(jnp.int32, (N_EXPERTS, N_EXPERTS * SLOT_ROWS), 1)
    expand = (ec // SLOT_ROWS == er).astype(BF16)
    rexp = jnp.dot(rank.astype(BF16), expand, preferred_element_type=F32)
    gexp = jnp.dot(gate.astype(BF16), expand, preferred_element_type=F32)
    jl = (lax.broadcasted_iota(jnp.int32, (TM_MOE, N_EXPERTS * SLOT_ROWS), 1) % SLOT_ROWS).astype(F32)
    gmat = jnp.where(rexp == jl, gexp, 0.0).astype(BF16)

    fetch(i, slot, lambda cp: cp.wait())
    acc_ref[...] = jnp.dot(gmat, ybuf[slot], preferred_element_type=F32)

    jc = lax.broadcasted_iota(jnp.int32, (TM_MOE, SLOT_ROWS), 1).astype(F32)
    for ex in range(N_EXPERTS):
        ng = _groups(cnt_ref[i * N_EXPERTS + ex])
        nch = lax.shift_right_logical(ng + (SLOT_ROWS // ROW_ALIGN - 1), 2)

        def chunk(c, carry, ex=ex):
            base = ex * cap_rows + off_ref[i * N_EXPERTS + ex] + c * SLOT_ROWS
            cp = pltpu.make_async_copy(
                yh_ref.at[pl.ds(pl.multiple_of(base, ROW_ALIGN), SLOT_ROWS)], obuf, osem.at[0])
            cp.start()
            cp.wait()
            gc = jnp.where(rank[:, ex:ex + 1] == jc + (c * SLOT_ROWS).astype(F32),
                           gate[:, ex:ex + 1].astype(BF16).astype(F32), 0.0).astype(BF16)
            acc_ref[...] += jnp.dot(gc, obuf[...], preferred_element_type=F32)
            return carry

        lax.fori_loop(1, nch, chunk, 0)

    o_ref[...] = _layer_norm(DN_ALPHA * y1_ref[...] + acc_ref[...], lw_ref[...], lb_ref[...])


def _combine(off, cnt, y1, rank_tm, gate_tm, yh, lw, lb, cap_rows):
    n = y1.shape[0]
    ntiles = n // TM_MOE
    row = lambda i, o, c: (i, 0)
    const = lambda i, o, c: (0, 0)
    gs = pltpu.PrefetchScalarGridSpec(
        num_scalar_prefetch=2,
        grid=(ntiles,),
        in_specs=[pl.BlockSpec((TM_MOE, D_MODEL), row),
                  pl.BlockSpec((TM_MOE, N_EXPERTS), row),
                  pl.BlockSpec((TM_MOE, N_EXPERTS), row),
                  pl.BlockSpec(memory_space=pl.ANY),
                  pl.BlockSpec((1, D_MODEL), const),
                  pl.BlockSpec((1, D_MODEL), const)],
        out_specs=pl.BlockSpec((TM_MOE, D_MODEL), row),
        scratch_shapes=[pltpu.VMEM((2, N_EXPERTS * SLOT_ROWS, D_MODEL), BF16),
                        pltpu.VMEM((SLOT_ROWS, D_MODEL), BF16),
                        pltpu.VMEM((TM_MOE, D_MODEL), F32),
                        pltpu.SemaphoreType.DMA((2,)),
                        pltpu.SemaphoreType.DMA((1,))])
    return pl.pallas_call(
        functools.partial(_combine_kernel, cap_rows=cap_rows),
        grid_spec=gs,
        out_shape=jax.ShapeDtypeStruct((n, D_MODEL), F32),
        compiler_params=_cparams(("arbitrary",)),
        name="combine",
    )(off, cnt, y1, rank_tm, gate_tm, yh, lw, lb)


def _rope_tables(seq):
    pos = jnp.arange(seq, dtype=F32)[:, None]
    inv_m = 1.0 / (ROPE_BASE ** (jnp.arange(0, MLA_D_ROPE, 2, dtype=F32) / MLA_D_ROPE))
    am = pos * inv_m[None, :]
    one = jnp.ones((seq, MLA_D_NOPE), F32)
    zero_n = jnp.zeros((seq, MLA_D_NOPE), F32)
    zero_p = jnp.zeros((seq, HEAD_PAD - MLA_D_NOPE - MLA_D_ROPE), F32)
    cm = jnp.concatenate([one, jnp.cos(am), jnp.cos(am), zero_p], axis=1)
    sm = jnp.concatenate([zero_n, jnp.sin(am), jnp.sin(am), zero_p], axis=1)
    inv_r = 1.0 / (ROPE_BASE ** (jnp.arange(0, RET_DK, 2, dtype=F32) / RET_DK))
    ar = pos * inv_r[None, :]
    cosr = jnp.concatenate([jnp.cos(ar), jnp.cos(ar)], axis=1)
    sinr = jnp.concatenate([-jnp.sin(ar), jnp.sin(ar)], axis=1)
    return cm, sm, cosr, sinr


def _prep_weights(w_in, mla_w_uq, mla_w_ukv):
    half = MLA_D_ROPE // 2
    base = 4 * RET_WIDTH + MLA_Q_LORA + MLA_KV_LORA
    w_kr = w_in[:, base:base + MLA_D_ROPE]
    zn = jnp.zeros((D_MODEL, MLA_D_NOPE), F32)
    zp = jnp.zeros((D_MODEL, HEAD_PAD - MLA_D_NOPE - MLA_D_ROPE), F32)
    kr_p = jnp.concatenate([zn, w_kr, zp], axis=1)
    kr_rot = jnp.concatenate([zn, -w_kr[:, half:], w_kr[:, :half], zp], axis=1)
    w_all = jnp.concatenate([w_in[:, :base], kr_p, kr_rot], axis=1).astype(BF16)

    wq = mla_w_uq.reshape(MLA_Q_LORA, MLA_HEADS, MLA_D_NOPE + MLA_D_ROPE)
    nope, ropew = wq[..., :MLA_D_NOPE], wq[..., MLA_D_NOPE:]
    zq = jnp.zeros((MLA_Q_LORA, MLA_HEADS, HEAD_PAD - MLA_D_NOPE - MLA_D_ROPE), F32)
    q_p = jnp.concatenate([nope, ropew, zq], axis=-1).reshape(MLA_Q_LORA, -1)
    q_rot = jnp.concatenate([jnp.zeros_like(nope), -ropew[..., half:], ropew[..., :half], zq],
                            axis=-1).reshape(MLA_Q_LORA, -1)
    wq2 = jnp.concatenate([q_p, q_rot], axis=1).astype(BF16)

    wkv3 = mla_w_ukv.reshape(MLA_KV_LORA, MLA_HEADS, MLA_D_NOPE + MLA_D_V)
    k_p = jnp.concatenate([wkv3[..., :MLA_D_NOPE],
                           jnp.zeros((MLA_KV_LORA, MLA_HEADS, HEAD_PAD - MLA_D_NOPE), F32)],
                          axis=-1).reshape(MLA_KV_LORA, -1)
    v_w = wkv3[..., MLA_D_NOPE:].reshape(MLA_KV_LORA, -1)
    wkv = jnp.concatenate([k_p, v_w], axis=1).astype(BF16)
    return w_all, wq2, wkv


def _mixer_ln(x, wts):
    batch, seq, _ = x.shape
    x2d = x.reshape(batch * seq, D_MODEL)
    cm, sm, cosr, sinr = _rope_tables(seq)
    rq, rk, rv, rg, q, k, v = _inproj(x2d, seq, wts["w_all"], wts["wq2"], wts["wkv"],
                                      wts["qnw"], wts["kvnw"], cm, sm)
    ret = _retention(rq, rk, rv, rg, cosr, sinr, wts["dfw"], wts["dbw"], wts["gnw"], batch, seq)
    att = _attention(q, k, v, batch, seq)
    return _outproj(x2d, ret, att, wts["wo1"], wts["wo2"], wts["onw"], wts["ln1w"], wts["ln1b"],
                    wts["wrt"])


def _moe_ln(y1, logits, wts):
    n = y1.shape[0]
    cap = EC_CAPACITY_FACTOR * n // N_EXPERTS
    ntiles = n // TM_MOE
    cap_rows = -(-(cap + ROW_ALIGN * ntiles + SLOT_ROWS) // TM_FFN) * TM_FFN
    rank_em, gate_em, cnt_rep, off_rep, tot_rep = _route(logits, cap)
    cnt = cnt_rep[:, :, 0].astype(jnp.int32).reshape(-1)
    off = off_rep[:, :, 0].astype(jnp.int32).reshape(-1)
    ntile = (tot_rep[:, 0].astype(jnp.int32) + (TM_FFN - 1)) // TM_FFN
    to_tm = lambda a: jnp.transpose(a, (0, 2, 1)).reshape(n, N_EXPERTS)
    xg = _compact(off, cnt, y1, rank_em, cap_rows)
    yh = _ffn(ntile, xg, wts["wg"], wts["wu"], wts["wd"], cap_rows)
    return _combine(off, cnt, y1, to_tm(rank_em), to_tm(gate_em), yh, wts["ln2w"], wts["ln2b"],
                    cap_rows)


def _layer(x, wts):
    batch, seq, _ = x.shape
    y1, logits = _mixer_ln(x, wts)
    return _moe_ln(y1, logits, wts).reshape(batch, seq, D_MODEL)


def kernel(x_prompt, x_sample, w_in, ret_decay_fwd, ret_decay_bwd, ret_gn_w, mla_q_norm_w, mla_w_uq,
           mla_kv_norm_w, mla_w_ukv, mla_out_norm_w, w_out, ln1_w, ln1_b, w_router, w_gate, w_up,
           w_down, ln2_w, ln2_b):
    depth = w_in.shape[0]
    y_prompt, y_sample = x_prompt, x_sample
    for l in range(depth):
        w_all, wq2, wkv = _prep_weights(w_in[l], mla_w_uq[l], mla_w_ukv[l])
        rep = lambda a: jnp.broadcast_to(a.astype(F32)[:, None, None], (RET_HEADS, 1, LANES))
        wts = dict(
            w_all=w_all, wq2=wq2, wkv=wkv,
            qnw=mla_q_norm_w[l][None, :], kvnw=mla_kv_norm_w[l][None, :],
            dfw=rep(ret_decay_fwd[l]), dbw=rep(ret_decay_bwd[l]), gnw=ret_gn_w[l][None, :],
            wo1=w_out[l][:RET_WIDTH].astype(BF16), wo2=w_out[l][RET_WIDTH:].astype(BF16),
            onw=mla_out_norm_w[l][None, :], ln1w=ln1_w[l][None, :], ln1b=ln1_b[l][None, :],
            wrt=w_router[l].T,
            wg=w_gate[l].astype(BF16), wu=w_up[l].astype(BF16), wd=w_down[l].astype(BF16),
            ln2w=ln2_w[l][None, :], ln2b=ln2_b[l][None, :])
        y_prompt = _layer(y_prompt, wts)
        y_sample = _layer(y_sample, wts)
    return (y_prompt, y_sample)
```

```python
import functools
import math

import numpy as np
import jax
import jax.numpy as jnp
from jax import lax
from jax.experimental import pallas as pl
from jax.experimental.pallas import tpu as pltpu

F32 = jnp.float32
BF16 = jnp.bfloat16

D_MODEL = 1024
RET_WIDTH = 512
RET_HEADS = 4
RET_DK = 128
RET_CHUNK = 128
MLA_HEADS = 8
MLA_D_NOPE = 64
MLA_D_ROPE = 32
MLA_D_V = 64
MLA_Q_LORA = 256
MLA_KV_LORA = 128
N_EXPERTS = 16
EC_CAPACITY_FACTOR = 2
ROPE_BASE = 10000.0
LN_EPS = 1e-5
RMS_EPS = 1e-6
DN_ALPHA = 2.0 ** 0.25

LANES = 128
HEAD_PAD = 128
PROJ_COLS = 4 * RET_WIDTH + MLA_Q_LORA + MLA_KV_LORA + 2 * LANES
VMEM_LIMIT = 56 * 1024 * 1024

TM_PROJ = 512
RET_STEP_CHUNKS = 8
TQ_ATT = 512
VT_CHUNKS = 2
VT_ROWS = 80
TK_ATT = VT_CHUNKS * TM_PROJ
KSUB_ATT = 256
TM_MOE = 256
SLOT_ROWS = 64
ROW_ALIGN = 16
TM_FFN = 512
BLOCKS_PER_TILE = TM_MOE // LANES

_QK_SCALE = (MLA_D_NOPE + MLA_D_ROPE) ** -0.5 * math.log2(math.e)


def _cparams(sem):
    return pltpu.CompilerParams(dimension_semantics=sem, vmem_limit_bytes=VMEM_LIMIT)


def _inproj_kernel(x_ref, w_ref, wq_ref, wkv_ref, qnw_ref, kvnw_ref, cm_ref, sm_ref,
                   rq_ref, rk_ref, rv_ref, rg_ref, q_ref, k_ref, vt_ref):
    xb = x_ref[...].astype(BF16)
    proj = jnp.dot(xb, w_ref[...], preferred_element_type=F32)
    rq_ref[...] = proj[:, 0:512]
    rk_ref[...] = proj[:, 512:1024]
    rv_ref[...] = proj[:, 1024:1536]
    rg_ref[...] = proj[:, 1536:2048]
    cm = cm_ref[...]
    sm = sm_ref[...]

    cq = proj[:, 2048:2304]
    cqn = cq * lax.rsqrt(jnp.mean(cq * cq, axis=-1, keepdims=True) + RMS_EPS) * qnw_ref[...]
    qq = jnp.dot(cqn.astype(BF16), wq_ref[...], preferred_element_type=F32)

    ckv = proj[:, 2304:2432]
    ckvn = ckv * lax.rsqrt(jnp.mean(ckv * ckv, axis=-1, keepdims=True) + RMS_EPS) * kvnw_ref[...]
    kv = jnp.dot(ckvn.astype(BF16), wkv_ref[...], preferred_element_type=F32)

    krope = proj[:, 2432:2560] * cm + proj[:, 2560:2688] * sm
    for h in range(MLA_HEADS):
        lo = h * HEAD_PAD
        qh = qq[:, lo:lo + HEAD_PAD] * cm + qq[:, 1024 + lo:1024 + lo + HEAD_PAD] * sm
        q_ref[:, lo:lo + HEAD_PAD] = (qh * _QK_SCALE).astype(BF16)
        k_ref[:, lo:lo + HEAD_PAD] = (kv[:, lo:lo + HEAD_PAD] + krope).astype(BF16)
    vt = kv[:, 1024:1536].T.astype(BF16)
    pad_rows = VT_ROWS - MLA_D_V
    ones_row = (lax.broadcasted_iota(jnp.int32, (pad_rows, vt.shape[1]), 0) == 0).astype(BF16)
    for h in range(MLA_HEADS):
        vt_ref[0, 0, h * VT_ROWS:h * VT_ROWS + MLA_D_V, :] = vt[h * MLA_D_V:(h + 1) * MLA_D_V, :]
        vt_ref[0, 0, h * VT_ROWS + MLA_D_V:(h + 1) * VT_ROWS, :] = ones_row


def _inproj(x2d, seq, w_all, wq2, wkv, qnw, kvnw, cm, sm):
    t = x2d.shape[0]
    tm = TM_PROJ
    nseq = seq // tm
    row = lambda i: (i, 0)
    const = lambda i: (0, 0)
    tab = lambda i: (i % nseq, 0)
    outs = [jax.ShapeDtypeStruct((t, RET_WIDTH), F32)] * 4 + [
        jax.ShapeDtypeStruct((t, MLA_HEADS * HEAD_PAD), BF16),
        jax.ShapeDtypeStruct((t, MLA_HEADS * HEAD_PAD), BF16),
        jax.ShapeDtypeStruct((t // seq, nseq, MLA_HEADS * VT_ROWS, tm), BF16)]
    return pl.pallas_call(
        _inproj_kernel,
        grid=(t // tm,),
        in_specs=[pl.BlockSpec((tm, D_MODEL), row),
                  pl.BlockSpec((D_MODEL, PROJ_COLS), const),
                  pl.BlockSpec((MLA_Q_LORA, 2 * MLA_HEADS * HEAD_PAD), const),
                  pl.BlockSpec((MLA_KV_LORA, MLA_HEADS * (HEAD_PAD + MLA_D_V)), const),
                  pl.BlockSpec((1, MLA_Q_LORA), const),
                  pl.BlockSpec((1, MLA_KV_LORA), const),
                  pl.BlockSpec((tm, LANES), tab),
                  pl.BlockSpec((tm, LANES), tab)],
        out_specs=[pl.BlockSpec((tm, RET_WIDTH), row)] * 4 + [
            pl.BlockSpec((tm, MLA_HEADS * HEAD_PAD), row),
            pl.BlockSpec((tm, MLA_HEADS * HEAD_PAD), row),
            pl.BlockSpec((1, 1, MLA_HEADS * VT_ROWS, tm), lambda i: (i // nseq, i % nseq, 0, 0))],
        out_shape=outs,
        compiler_params=_cparams(("parallel",)),
        name="inproj",
    )(x2d, w_all, wq2, wkv, qnw, kvnw, cm, sm)


def _log_sigmoid(x):
    return jnp.minimum(x, 0.0) - jnp.log(1.0 + jnp.exp(-jnp.abs(x)))


def _ret_kernel(q_ref, k_ref, v_ref, g_ref, cos_ref, sin_ref, df_ref, db_ref, gnw_ref,
                o_ref, sf_ref, sb_ref, sbst_ref, *, nsteps):
    phase = pl.program_id(2)
    step = pl.program_id(3)
    c = RET_CHUNK
    lgf = _log_sigmoid(df_ref[0])
    lgb = _log_sigmoid(db_ref[0])
    rpos = lax.broadcasted_iota(jnp.int32, (c, c), 0).astype(F32)
    cpos = lax.broadcasted_iota(jnp.int32, (c, c), 1).astype(F32)

    def rope(x, cs, sn):
        return x * cs + pltpu.roll(x, RET_DK // 2, axis=1) * sn

    @pl.when(jnp.logical_and(phase == 0, step == 0))
    def _():
        sb_ref[...] = jnp.zeros_like(sb_ref)

    @pl.when(jnp.logical_and(phase == 1, step == 0))
    def _():
        sf_ref[...] = jnp.zeros_like(sf_ref)

    @pl.when(phase == 0)
    def _():
        kwb = jnp.exp(lgb * rpos)
        cdb = jnp.exp(lgb * float(c))
        kvstep = nsteps - 1 - step
        for g in range(RET_STEP_CHUNKS - 1, -1, -1):
            rows = slice(g * c, (g + 1) * c)
            kr = rope(k_ref[rows, :], cos_ref[rows, :], sin_ref[rows, :]) * (RET_DK ** -0.5)
            vb = v_ref[rows, :].astype(BF16)
            sbst_ref[kvstep * RET_STEP_CHUNKS + g] = sb_ref[...].astype(BF16)
            kvc = lax.dot_general((kr * kwb).astype(BF16), vb, (((0,), (0,)), ((), ())),
                                  preferred_element_type=F32)
            sb_ref[...] = cdb * sb_ref[...] + kvc

    @pl.when(phase == 1)
    def _():
        diff = rpos - cpos
        dmat = jnp.where(diff >= 0, jnp.exp(lgf * jnp.maximum(diff, 0.0)),
                         jnp.exp(lgb * jnp.maximum(-diff, 0.0)))
        kwf = jnp.exp(lgf * (float(c - 1) - rpos))
        qwf = jnp.exp(lgf * (rpos + 1.0))
        qwb = jnp.exp(lgb * (float(c) - rpos))
        cdf = jnp.exp(lgf * float(c))
        gnw = gnw_ref[...]
        for g in range(RET_STEP_CHUNKS):
            rows = slice(g * c, (g + 1) * c)
            cs = cos_ref[rows, :]
            sn = sin_ref[rows, :]
            qr = rope(q_ref[rows, :], cs, sn)
            kr = rope(k_ref[rows, :], cs, sn) * (RET_DK ** -0.5)
            vb = v_ref[rows, :].astype(BF16)
            sc = lax.dot_general(qr.astype(BF16), kr.astype(BF16), (((1,), (1,)), ((), ())),
                                 preferred_element_type=F32) * dmat
            o = jnp.dot(sc.astype(BF16), vb, preferred_element_type=F32)
            o += jnp.dot((qr * qwf).astype(BF16), sf_ref[...].astype(BF16), preferred_element_type=F32)
            o += jnp.dot((qr * qwb).astype(BF16), sbst_ref[step * RET_STEP_CHUNKS + g],
                         preferred_element_type=F32)
            kvc = lax.dot_general((kr * kwf).astype(BF16), vb, (((0,), (0,)), ((), ())),
                                  preferred_element_type=F32)
            sf_ref[...] = cdf * sf_ref[...] + kvc
            mu = jnp.mean(o, axis=-1, keepdims=True)
            oc = o - mu
            var = jnp.mean(oc * oc, axis=-1, keepdims=True)
            on = oc * lax.rsqrt(var + LN_EPS) * gnw
            gt = g_ref[rows, :]
            o_ref[rows, :] = (gt / (1.0 + jnp.exp(-gt)) * on).astype(o_ref.dtype)


def _retention(rq, rk, rv, rg, cosr, sinr, dfw, dbw, gnw, batch, seq):
    t = rq.shape[0]
    ts = RET_STEP_CHUNKS * RET_CHUNK
    nsteps = seq // ts
    kvi = lambda p, i: jnp.where(p == 0, nsteps - 1 - i, i)
    qi = lambda p, i: jnp.where(p == 0, 0, i)
    kv_map = lambda b, h, p, i: (b * nsteps + kvi(p, i), h)
    q_map = lambda b, h, p, i: (b * nsteps + qi(p, i), h)
    tab_map = lambda b, h, p, i: (kvi(p, i), 0)
    head3 = lambda b, h, p, i: (h, 0, 0)
    return pl.pallas_call(
        functools.partial(_ret_kernel, nsteps=nsteps),
        grid=(batch, RET_HEADS, 2, nsteps),
        in_specs=[pl.BlockSpec((ts, RET_DK), q_map),
                  pl.BlockSpec((ts, RET_DK), kv_map),
                  pl.BlockSpec((ts, RET_DK), kv_map),
                  pl.BlockSpec((ts, RET_DK), q_map),
                  pl.BlockSpec((ts, RET_DK), tab_map),
                  pl.BlockSpec((ts, RET_DK), tab_map),
                  pl.BlockSpec((1, 1, LANES), head3),
                  pl.BlockSpec((1, 1, LANES), head3),
                  pl.BlockSpec((1, RET_DK), lambda b, h, p, i: (0, h))],
        out_specs=pl.BlockSpec((ts, RET_DK), q_map),
        out_shape=jax.ShapeDtypeStruct((t, RET_WIDTH), BF16),
        scratch_shapes=[pltpu.VMEM((RET_DK, RET_DK), F32),
                        pltpu.VMEM((RET_DK, RET_DK), F32),
                        pltpu.VMEM((seq // RET_CHUNK, RET_DK, RET_DK), BF16)],
        compiler_params=_cparams(("parallel", "parallel", "arbitrary", "arbitrary")),
        name="retention",
    )(rq, rk, rv, rg, cosr, sinr, dfw, dbw, gnw)


def _attn_kernel(q_ref, k_ref, vt_ref, o_ref, ste_ref, sto_ref, *, seq):
    nkv = seq // TK_ATT
    nt = (((1,), (1,)), ((), ()))
    qs = [q_ref[:, h * HEAD_PAD:(h + 1) * HEAD_PAD] for h in range(2)]

    nsub = TK_ATT // KSUB_ATT
    sub_per_chunk = TM_PROJ // KSUB_ATT

    def score_steps(j, st_ref):
        ks = pl.multiple_of(j * TK_ATT, TK_ATT)

        def step(h, s):
            def run():
                rows = pl.ds(ks + s * KSUB_ATT, KSUB_ATT)
                st_ref[h, s * KSUB_ATT:(s + 1) * KSUB_ATT, :] = lax.dot_general(
                    k_ref[rows, h * HEAD_PAD:(h + 1) * HEAD_PAD], qs[h], nt, preferred_element_type=F32)
            return run
        return [step(h, s) for h in range(2) for s in range(nsub)]

    def softmax_steps(j, st_ref, stats, out):
        steps = []
        for h in range(2):
            state = {}

            def start(h=h, state=state):
                m, acc = stats[h]
                state["mn"] = jnp.maximum(m, jnp.max(st_ref[h], axis=0, keepdims=True))
                state["pv"] = jnp.exp2(m - state["mn"]) * acc

            def part(s, h=h, state=state):
                def run():
                    pb = jnp.exp2(st_ref[h, s * KSUB_ATT:(s + 1) * KSUB_ATT, :] - state["mn"]).astype(BF16)
                    c, r = divmod(s, sub_per_chunk)
                    state["pv"] = state["pv"] + jnp.dot(
                        vt_ref[0, j * VT_CHUNKS + c, h * VT_ROWS:(h + 1) * VT_ROWS,
                               r * KSUB_ATT:(r + 1) * KSUB_ATT],
                        pb, preferred_element_type=F32)
                    if s == nsub - 1:
                        out[h] = (state["mn"], state["pv"])
                return run
            steps += [start] + [part(s) for s in range(nsub)]
        return steps

    def interleave(mxu_steps, vpu_steps):
        n = max(len(mxu_steps), len(vpu_steps))
        for i in range(n):
            if i < len(mxu_steps):
                mxu_steps[i]()
            if i < len(vpu_steps):
                vpu_steps[i]()

    def body(jj, stats):
        mid, out = {}, {}
        interleave(score_steps(2 * jj + 1, sto_ref), softmax_steps(2 * jj, ste_ref, stats, mid))
        nxt = jnp.minimum(2 * jj + 2, nkv - 1)
        interleave(score_steps(nxt, ste_ref), softmax_steps(2 * jj + 1, sto_ref, (mid[0], mid[1]), out))
        return out[0], out[1]

    init = (jnp.full((1, TQ_ATT), -jnp.inf, F32), jnp.zeros((VT_ROWS, TQ_ATT), F32))
    for run in score_steps(0, ste_ref):
        run()
    fin = lax.fori_loop(0, nkv // 2, body, (init, init))
    for h in range(2):
        acc = fin[h][1]
        o_ref[0, h * MLA_D_V:(h + 1) * MLA_D_V, :] = acc[0:MLA_D_V] / acc[MLA_D_V:MLA_D_V + 1]


def _attention(q, k, vt, batch, seq):
    nq = seq // TQ_ATT
    nvt = seq // TM_PROJ
    return pl.pallas_call(
        functools.partial(_attn_kernel, seq=seq),
        grid=(batch, MLA_HEADS // 2, nq),
        in_specs=[pl.BlockSpec((TQ_ATT, 2 * HEAD_PAD), lambda b, hp, i: (b * nq + i, hp)),
                  pl.BlockSpec((seq, 2 * HEAD_PAD), lambda b, hp, i: (b, hp)),
                  pl.BlockSpec((1, nvt, 2 * VT_ROWS, TM_PROJ), lambda b, hp, i: (b, 0, hp, 0))],
        out_specs=pl.BlockSpec((1, 2 * MLA_D_V, TQ_ATT), lambda b, hp, i: (b, hp, i)),
        out_shape=jax.ShapeDtypeStruct((batch, MLA_HEADS * MLA_D_V, seq), F32),
        scratch_shapes=[pltpu.VMEM((2, TK_ATT, TQ_ATT), F32),
                        pltpu.VMEM((2, TK_ATT, TQ_ATT), F32)],
        compiler_params=_cparams(("parallel", "parallel", "arbitrary")),
        name="attention",
    )(q, k, vt)


def _layer_norm(z, w, b):
    mu = jnp.mean(z, axis=-1, keepdims=True)
    zc = z - mu
    var = jnp.mean(zc * zc, axis=-1, keepdims=True)
    return zc * lax.rsqrt(var + LN_EPS) * w + b


def _split_bf16(a):
    hi = a.astype(BF16)
    lo = (a - hi.astype(F32)).astype(BF16)
    return hi, lo


def _outproj_kernel(x_ref, ret_ref, att_ref, wo1_ref, wo2_ref, onw_ref, lw_ref, lb_ref, wr_ref,
                    y_ref, lg_ref):
    att = att_ref[0].T
    an = att * lax.rsqrt(jnp.mean(att * att, axis=-1, keepdims=True) + RMS_EPS) * onw_ref[...]
    mix = jnp.dot(ret_ref[...], wo1_ref[...], preferred_element_type=F32)
    mix += jnp.dot(an.astype(BF16), wo2_ref[...], preferred_element_type=F32)
    y = _layer_norm(DN_ALPHA * x_ref[...] + mix, lw_ref[...], lb_ref[...])
    y_ref[...] = y
    yh, yl = _split_bf16(y)
    wh, wl = _split_bf16(wr_ref[...])
    nt = (((1,), (1,)), ((), ()))
    lt = lax.dot_general(wh, yh, nt, preferred_element_type=F32)
    lt += lax.dot_general(wh, yl, nt, preferred_element_type=F32)
    lt += lax.dot_general(wl, yh, nt, preferred_element_type=F32)
    for cblk in range(lt.shape[1] // LANES):
        lg_ref[cblk] = lt[:, cblk * LANES:(cblk + 1) * LANES]


def _outproj(x2d, ret, att_t, wo1, wo2, onw, lw, lb, wrt):
    t = x2d.shape[0]
    tm = TM_PROJ
    nseq = att_t.shape[2] // tm
    row = lambda i: (i, 0)
    const = lambda i: (0, 0)
    return pl.pallas_call(
        _outproj_kernel,
        grid=(t // tm,),
        in_specs=[pl.BlockSpec((tm, D_MODEL), row),
                  pl.BlockSpec((tm, RET_WIDTH), row),
                  pl.BlockSpec((1, RET_WIDTH, tm), lambda i: (i // nseq, 0, i % nseq)),
                  pl.BlockSpec((RET_WIDTH, D_MODEL), const),
                  pl.BlockSpec((RET_WIDTH, D_MODEL), const),
                  pl.BlockSpec((1, RET_WIDTH), const),
                  pl.BlockSpec((1, D_MODEL), const),
                  pl.BlockSpec((1, D_MODEL), const),
                  pl.BlockSpec((N_EXPERTS, D_MODEL), const)],
        out_specs=[pl.BlockSpec((tm, D_MODEL), row),
                   pl.BlockSpec((tm // LANES, N_EXPERTS, LANES), lambda i: (i, 0, 0))],
        out_shape=[jax.ShapeDtypeStruct((t, D_MODEL), F32),
                   jax.ShapeDtypeStruct((t // LANES, N_EXPERTS, LANES), F32)],
        compiler_params=_cparams(("parallel",)),
        name="outproj",
    )(x2d, ret, att_t, wo1, wo2, onw, lw, lb, wrt)


def _route_kernel(lg_ref, rank_ref, gate_ref, cnt_ref, off_ref, tot_ref, blk_ref, boff_ref,
                  *, cap, nb):
    e = N_EXPERTS
    ntiles = nb // BLOCKS_PER_TILE
    lg = lg_ref[...]
    ex = jnp.exp(lg - jnp.max(lg, axis=1, keepdims=True))
    aff = ex / jnp.sum(ex, axis=1, keepdims=True)
    gate_ref[...] = aff
    key = pltpu.bitcast(aff, jnp.int32)

    def count(mask):
        part = jnp.sum(mask.astype(F32), axis=0)
        return jnp.broadcast_to(jnp.sum(part, axis=1, keepdims=True), (e, LANES))

    def bisect(_, lohi):
        lo, hi = lohi
        mid = lo + lax.shift_right_logical(hi - lo, 1)
        ok = count(key >= mid[None]) >= float(cap)
        return jnp.where(ok, mid, lo), jnp.where(ok, hi, mid)

    lo0 = jnp.zeros((e, LANES), jnp.int32)
    hi0 = jnp.full((e, LANES), 0x7F800000, jnp.int32)
    thr, _ = lax.fori_loop(0, 31, bisect, (lo0, hi0))
    gt = key > thr[None]
    eq = key == thr[None]
    need = float(cap) - count(gt)

    ii = lax.broadcasted_iota(jnp.int32, (LANES, LANES), 0)
    jj = lax.broadcasted_iota(jnp.int32, (LANES, LANES), 1)
    upper = (ii <= jj).astype(BF16)
    ones = jnp.ones((LANES, LANES), BF16)

    def block_scan(mask):
        m2 = mask.astype(BF16).reshape(nb * e, LANES)
        incl = jnp.dot(m2, upper, preferred_element_type=F32).reshape(nb, e, LANES)
        tot = jnp.dot(m2, ones, preferred_element_type=F32).reshape(nb, e, LANES)
        return incl, tot

    def leading_excl_scan(src_ref, dst_ref, n):
        def step(j, carry):
            dst_ref[j] = carry
            return carry + src_ref[j]
        return lax.fori_loop(0, n, step, jnp.zeros((e, LANES), F32))

    eqf = eq.astype(F32)
    incl, tot = block_scan(eq)
    blk_ref[...] = tot
    leading_excl_scan(blk_ref, boff_ref, nb)
    eq_before = incl - eqf + boff_ref[...]
    sel = jnp.logical_or(gt, jnp.logical_and(eq, eq_before < need[None]))

    self32 = sel.astype(F32)
    incl, tot = block_scan(sel)
    local = (incl - self32).reshape(ntiles, BLOCKS_PER_TILE, e, LANES)
    tot4 = tot.reshape(ntiles, BLOCKS_PER_TILE, e, LANES)
    ranks = [local[:, 0]]
    run = tot4[:, 0]
    for b in range(1, BLOCKS_PER_TILE):
        ranks.append(local[:, b] + run)
        run = run + tot4[:, b]
    rank = jnp.stack(ranks, axis=1).reshape(nb, e, LANES)
    rank_ref[...] = jnp.where(sel, rank, -1.0)
    cnt_ref[...] = run
    blk_ref[0:ntiles] = jnp.floor((run + float(ROW_ALIGN - 1)) * (1.0 / ROW_ALIGN)) * float(ROW_ALIGN)
    tot_ref[...] = leading_excl_scan(blk_ref, boff_ref, ntiles)
    off_ref[...] = boff_ref[0:ntiles]


def _route(logits, cap):
    nb = logits.shape[0]
    ntiles = nb // BLOCKS_PER_TILE
    big = jax.ShapeDtypeStruct((nb, N_EXPERTS, LANES), F32)
    small = jax.ShapeDtypeStruct((ntiles, N_EXPERTS, LANES), F32)
    return pl.pallas_call(
        functools.partial(_route_kernel, cap=cap, nb=nb),
        out_shape=[big, big, small, small, jax.ShapeDtypeStruct((N_EXPERTS, LANES), F32)],
        scratch_shapes=[pltpu.VMEM((nb, N_EXPERTS, LANES), F32),
                        pltpu.VMEM((nb, N_EXPERTS, LANES), F32)],
        compiler_params=pltpu.CompilerParams(vmem_limit_bytes=VMEM_LIMIT),
        name="route",
    )(logits)


def _groups(cnt):
    return lax.shift_right_logical(cnt + (ROW_ALIGN - 1), 4)


def _compact_kernel(off_ref, cnt_ref, y_ref, rank_ref, xg0_ref, xg_ref, p_ref, st_ref, ost_ref,
                    sem, osem, *, cap_rows):
    del xg0_ref
    i = pl.program_id(0)
    nt = pl.num_programs(0)
    slot = i % 2
    fast_groups = SLOT_ROWS // ROW_ALIGN

    def piece(tile, slt, ex, k):
        base = ex * cap_rows + off_ref[tile * N_EXPERTS + ex] + k * ROW_ALIGN
        return pltpu.make_async_copy(
            st_ref.at[slt, pl.ds(ex * SLOT_ROWS + k * ROW_ALIGN, ROW_ALIGN)],
            xg_ref.at[pl.ds(pl.multiple_of(base, ROW_ALIGN), ROW_ALIGN)],
            sem.at[slt])

    def for_pieces(tile, slt, fn):
        for ex in range(N_EXPERTS):
            ng = _groups(cnt_ref[tile * N_EXPERTS + ex])
            for k in range(fast_groups):
                @pl.when(k < ng)
                def _(ex=ex, k=k):
                    fn(piece(tile, slt, ex, k))

    @pl.when(i >= 2)
    def _():
        for_pieces(i - 2, slot, lambda cp: cp.wait())

    xb = y_ref[...].astype(BF16)
    rank = jnp.concatenate([rank_ref[b] for b in range(BLOCKS_PER_TILE)], axis=1)
    jrow = lax.broadcasted_iota(jnp.int32, (SLOT_ROWS, TM_MOE), 0).astype(F32)
    for ex in range(N_EXPERTS):
        p_ref[ex * SLOT_ROWS:(ex + 1) * SLOT_ROWS, :] = (
            rank[ex:ex + 1, :] == jrow).astype(BF16)
    st_ref[slot] = jnp.dot(p_ref[...], xb, preferred_element_type=F32).astype(BF16)
    for_pieces(i, slot, lambda cp: cp.start())

    for ex in range(N_EXPERTS):
        ng = _groups(cnt_ref[i * N_EXPERTS + ex])
        nch = lax.shift_right_logical(ng + (fast_groups - 1), 2)

        def chunk(c, carry, ex=ex, ng=ng):
            pc = (rank[ex:ex + 1, :] == jrow + (c * SLOT_ROWS).astype(F32)).astype(BF16)
            ost_ref[...] = jnp.dot(pc, xb, preferred_element_type=F32).astype(BF16)
            for k in range(fast_groups):
                @pl.when(c * fast_groups + k < ng)
                def _(k=k):
                    base = (ex * cap_rows + off_ref[i * N_EXPERTS + ex]
                            + c * SLOT_ROWS + k * ROW_ALIGN)
                    cp = pltpu.make_async_copy(
                        ost_ref.at[pl.ds(k * ROW_ALIGN, ROW_ALIGN)],
                        xg_ref.at[pl.ds(pl.multiple_of(base, ROW_ALIGN), ROW_ALIGN)],
                        osem.at[0])
                    cp.start()
                    cp.wait()
            return carry

        lax.fori_loop(1, nch, chunk, 0)

    @pl.when(i == nt - 1)
    def _():
        @pl.when(i >= 1)
        def _():
            for_pieces(i - 1, 1 - slot, lambda cp: cp.wait())
        for_pieces(i, slot, lambda cp: cp.wait())


def _compact(off, cnt, y1, rank_em, cap_rows):
    n = y1.shape[0]
    ntiles = n // TM_MOE
    xg0 = jnp.zeros((N_EXPERTS * cap_rows, D_MODEL), BF16)
    gs = pltpu.PrefetchScalarGridSpec(
        num_scalar_prefetch=2,
        grid=(ntiles,),
        in_specs=[pl.BlockSpec((TM_MOE, D_MODEL), lambda i, o, c: (i, 0)),
                  pl.BlockSpec((BLOCKS_PER_TILE, N_EXPERTS, LANES), lambda i, o, c: (i, 0, 0)),
                  pl.BlockSpec(memory_space=pl.ANY)],
        out_specs=pl.BlockSpec(memory_space=pl.ANY),
        scratch_shapes=[pltpu.VMEM((N_EXPERTS * SLOT_ROWS, TM_MOE), BF16),
                        pltpu.VMEM((2, N_EXPERTS * SLOT_ROWS, D_MODEL), BF16),
                        pltpu.VMEM((SLOT_ROWS, D_MODEL), BF16),
                        pltpu.SemaphoreType.DMA((2,)),
                        pltpu.SemaphoreType.DMA((1,))])
    return pl.pallas_call(
        functools.partial(_compact_kernel, cap_rows=cap_rows),
        grid_spec=gs,
        out_shape=jax.ShapeDtypeStruct((N_EXPERTS * cap_rows, D_MODEL), BF16),
        input_output_aliases={4: 0},
        compiler_params=_cparams(("arbitrary",)),
        name="compact",
    )(off, cnt, y1, rank_em, xg0)


def _ffn_kernel(nt_ref, x_ref, wg_ref, wu_ref, wd_ref, y_ref):
    ex = pl.program_id(0)
    j = pl.program_id(1)

    @pl.when(j < nt_ref[ex])
    def _():
        x = x_ref[...]
        hg = jnp.dot(x, wg_ref[0], preferred_element_type=F32)
        hu = jnp.dot(x, wu_ref[0], preferred_element_type=F32)
        h = (hg / (1.0 + jnp.exp(-hg)) * hu).astype(BF16)
        y_ref[...] = jnp.dot(h, wd_ref[0], preferred_element_type=F32).astype(BF16)

    @pl.when(j >= nt_ref[ex])
    def _():
        y_ref[...] = jnp.zeros_like(y_ref)


def _ffn(ntile, xg, wg, wu, wd, cap_rows):
    nt = cap_rows // TM_FFN
    xmap = lambda ex, j, ntr: (ex * nt + jnp.minimum(j, jnp.maximum(ntr[ex] - 1, 0)), 0)
    wmap = lambda ex, j, ntr: (ex, 0, 0)
    gs = pltpu.PrefetchScalarGridSpec(
        num_scalar_prefetch=1,
        grid=(N_EXPERTS, nt),
        in_specs=[pl.BlockSpec((TM_FFN, D_MODEL), xmap),
                  pl.BlockSpec((1, D_MODEL, D_MODEL), wmap),
                  pl.BlockSpec((1, D_MODEL, D_MODEL), wmap),
                  pl.BlockSpec((1, D_MODEL, D_MODEL), wmap)],
        out_specs=pl.BlockSpec((TM_FFN, D_MODEL), lambda ex, j, ntr: (ex * nt + j, 0)))
    return pl.pallas_call(
        _ffn_kernel,
        grid_spec=gs,
        out_shape=jax.ShapeDtypeStruct((N_EXPERTS * cap_rows, D_MODEL), BF16),
        compiler_params=_cparams(("arbitrary", "arbitrary")),
        name="ffn",
    )(ntile, xg, wg, wu, wd)


def _combine_kernel(off_ref, cnt_ref, y1_ref, rank_ref, gate_ref, yh_ref, lw_ref, lb_ref,
                    o_ref, ybuf, obuf, acc_ref, sem, osem, *, cap_rows):
    i = pl.program_id(0)
    nt = pl.num_programs(0)
    slot = i % 2

    def fetch(tile, slt, fn):
        for ex in range(N_EXPERTS):
            base = ex * cap_rows + off_ref[tile * N_EXPERTS + ex]
            fn(pltpu.make_async_copy(
                yh_ref.at[pl.ds(pl.multiple_of(base, ROW_ALIGN), SLOT_ROWS)],
                ybuf.at[slt, pl.ds(ex * SLOT_ROWS, SLOT_ROWS)],
                sem.at[slt]))

    @pl.when(i == 0)
    def _():
        fetch(0, 0, lambda cp: cp.start())

    @pl.when(i + 1 < nt)
    def _():
        fetch(i + 1, 1 - slot, lambda cp: cp.start())

    rank = rank_ref[...]
    gate = gate_ref[...]
    er = lax.broadcasted_iota(jnp.int32, (N_EXPERTS, N_EXPERTS * SLOT_ROWS), 0)
    ec = lax.broadcasted_iota(jnp.int32, (N_EXPERTS, N_EXPERTS * SLOT_ROWS), 1)
    expand = (ec // SLOT_ROWS == er).astype(BF16)
    rexp = jnp.dot(rank.astype(BF16), expand, preferred_element_type=F32)
    gexp = jnp.dot(gate.astype(BF16), expand, preferred_element_type=F32)
    jl = (lax.broadcasted_iota(jnp.int32, (TM_MOE, N_EXPERTS * SLOT_ROWS), 1) % SLOT_ROWS).astype(F32)
    gmat = jnp.where(rexp == jl, gexp, 0.0).astype(BF16)

    fetch(i, slot, lambda cp: cp.wait())
    acc_ref[...] = jnp.dot(gmat, ybuf[slot], preferred_element_type=F32)

    jc = lax.broadcasted_iota(jnp.int32, (TM_MOE, SLOT_ROWS), 1).astype(F32)
    for ex in range(N_EXPERTS):
        ng = _groups(cnt_ref[i * N_EXPERTS + ex])
        nch = lax.shift_right_logical(ng + (SLOT_ROWS // ROW_ALIGN - 1), 2)

        def chunk(c, carry, ex=ex):
            base = ex * cap_rows + off_ref[i * N_EXPERTS + ex] + c * SLOT_ROWS
            cp = pltpu.make_async_copy(
                yh_ref.at[pl.ds(pl.multiple_of(base, ROW_ALIGN), SLOT_ROWS)], obuf, osem.at[0])
            cp.start()
            cp.wait()
            gc = jnp.where(rank[:, ex:ex + 1] == jc + (c * SLOT_ROWS).astype(F32),
                           gate[:, ex:ex + 1].astype(BF16).astype(F32), 0.0).astype(BF16)
            acc_ref[...] += jnp.dot(gc, obuf[...], preferred_element_type=F32)
            return carry

        lax.fori_loop(1, nch, chunk, 0)

    o_ref[...] = _layer_norm(DN_ALPHA * y1_ref[...] + acc_ref[...], lw_ref[...], lb_ref[...])


def _combine(off, cnt, y1, rank_tm, gate_tm, yh, lw, lb, cap_rows):
    n = y1.shape[0]
    ntiles = n // TM_MOE
    row = lambda i, o, c: (i, 0)
    const = lambda i, o, c: (0, 0)
    gs = pltpu.PrefetchScalarGridSpec(
        num_scalar_prefetch=2,
        grid=(ntiles,),
        in_specs=[pl.BlockSpec((TM_MOE, D_MODEL), row),
                  pl.BlockSpec((TM_MOE, N_EXPERTS), row),
                  pl.BlockSpec((TM_MOE, N_EXPERTS), row),
                  pl.BlockSpec(memory_space=pl.ANY),
                  pl.BlockSpec((1, D_MODEL), const),
                  pl.BlockSpec((1, D_MODEL), const)],
        out_specs=pl.BlockSpec((TM_MOE, D_MODEL), row),
        scratch_shapes=[pltpu.VMEM((2, N_EXPERTS * SLOT_ROWS, D_MODEL), BF16),
                        pltpu.VMEM((SLOT_ROWS, D_MODEL), BF16),
                        pltpu.VMEM((TM_MOE, D_MODEL), F32),
                        pltpu.SemaphoreType.DMA((2,)),
                        pltpu.SemaphoreType.DMA((1,))])
    return pl.pallas_call(
        functools.partial(_combine_kernel, cap_rows=cap_rows),
        grid_spec=gs,
        out_shape=jax.ShapeDtypeStruct((n, D_MODEL), F32),
        compiler_params=_cparams(("arbitrary",)),
        name="combine",
    )(off, cnt, y1, rank_tm, gate_tm, yh, lw, lb)


def _rope_tables(seq):
    pos = jnp.arange(seq, dtype=F32)[:, None]
    inv_m = 1.0 / (ROPE_BASE ** (jnp.arange(0, MLA_D_ROPE, 2, dtype=F32) / MLA_D_ROPE))
    am = pos * inv_m[None, :]
    one = jnp.ones((seq, MLA_D_NOPE), F32)
    zero_n = jnp.zeros((seq, MLA_D_NOPE), F32)
    zero_p = jnp.zeros((seq, HEAD_PAD - MLA_D_NOPE - MLA_D_ROPE), F32)
    cm = jnp.concatenate([one, jnp.cos(am), jnp.cos(am), zero_p], axis=1)
    sm = jnp.concatenate([zero_n, jnp.sin(am), jnp.sin(am), zero_p], axis=1)
    inv_r = 1.0 / (ROPE_BASE ** (jnp.arange(0, RET_DK, 2, dtype=F32) / RET_DK))
    ar = pos * inv_r[None, :]
    cosr = jnp.concatenate([jnp.cos(ar), jnp.cos(ar)], axis=1)
    sinr = jnp.concatenate([-jnp.sin(ar), jnp.sin(ar)], axis=1)
    return cm, sm, cosr, sinr


def _prep_weights(w_in, mla_w_uq, mla_w_ukv):
    half = MLA_D_ROPE // 2
    base = 4 * RET_WIDTH + MLA_Q_LORA + MLA_KV_LORA
    w_kr = w_in[:, base:base + MLA_D_ROPE]
    zn = jnp.zeros((D_MODEL, MLA_D_NOPE), F32)
    zp = jnp.zeros((D_MODEL, HEAD_PAD - MLA_D_NOPE - MLA_D_ROPE), F32)
    kr_p = jnp.concatenate([zn, w_kr, zp], axis=1)
    kr_rot = jnp.concatenate([zn, -w_kr[:, half:], w_kr[:, :half], zp], axis=1)
    w_all = jnp.concatenate([w_in[:, :base], kr_p, kr_rot], axis=1).astype(BF16)

    wq = mla_w_uq.reshape(MLA_Q_LORA, MLA_HEADS, MLA_D_NOPE + MLA_D_ROPE)
    nope, ropew = wq[..., :MLA_D_NOPE], wq[..., MLA_D_NOPE:]
    zq = jnp.zeros((MLA_Q_LORA, MLA_HEADS, HEAD_PAD - MLA_D_NOPE - MLA_D_ROPE), F32)
    q_p = jnp.concatenate([nope, ropew, zq], axis=-1).reshape(MLA_Q_LORA, -1)
    q_rot = jnp.concatenate([jnp.zeros_like(nope), -ropew[..., half:], ropew[..., :half], zq],
                            axis=-1).reshape(MLA_Q_LORA, -1)
    wq2 = jnp.concatenate([q_p, q_rot], axis=1).astype(BF16)

    wkv3 = mla_w_ukv.reshape(MLA_KV_LORA, MLA_HEADS, MLA_D_NOPE + MLA_D_V)
    k_p = jnp.concatenate([wkv3[..., :MLA_D_NOPE],
                           jnp.zeros((MLA_KV_LORA, MLA_HEADS, HEAD_PAD - MLA_D_NOPE), F32)],
                          axis=-1).reshape(MLA_KV_LORA, -1)
    v_w = wkv3[..., MLA_D_NOPE:].reshape(MLA_KV_LORA, -1)
    wkv = jnp.concatenate([k_p, v_w], axis=1).astype(BF16)
    return w_all, wq2, wkv


def _mixer_ln(x, wts):
    batch, seq, _ = x.shape
    x2d = x.reshape(batch * seq, D_MODEL)
    cm, sm, cosr, sinr = _rope_tables(seq)
    rq, rk, rv, rg, q, k, v = _inproj(x2d, seq, wts["w_all"], wts["wq2"], wts["wkv"],
                                      wts["qnw"], wts["kvnw"], cm, sm)
    ret = _retention(rq, rk, rv, rg, cosr, sinr, wts["dfw"], wts["dbw"], wts["gnw"], batch, seq)
    att = _attention(q, k, v, batch, seq)
    return _outproj(x2d, ret, att, wts["wo1"], wts["wo2"], wts["onw"], wts["ln1w"], wts["ln1b"],
                    wts["wrt"])


def _moe_ln(y1, logits, wts):
    n = y1.shape[0]
    cap = EC_CAPACITY_FACTOR * n // N_EXPERTS
    ntiles = n // TM_MOE
    cap_rows = -(-(cap + ROW_ALIGN * ntiles + SLOT_ROWS) // TM_FFN) * TM_FFN
    rank_em, gate_em, cnt_rep, off_rep, tot_rep = _route(logits, cap)
    cnt = cnt_rep[:, :, 0].astype(jnp.int32).reshape(-1)
    off = off_rep[:, :, 0].astype(jnp.int32).reshape(-1)
    ntile = (tot_rep[:, 0].astype(jnp.int32) + (TM_FFN - 1)) // TM_FFN
    to_tm = lambda a: jnp.transpose(a, (0, 2, 1)).reshape(n, N_EXPERTS)
    xg = _compact(off, cnt, y1, rank_em, cap_rows)
    yh = _ffn(ntile, xg, wts["wg"], wts["wu"], wts["wd"], cap_rows)
    return _combine(off, cnt, y1, to_tm(rank_em), to_tm(gate_em), yh, wts["ln2w"], wts["ln2b"],
                    cap_rows)


def _layer(x, wts):
    batch, seq, _ = x.shape
    y1, logits = _mixer_ln(x, wts)
    return _moe_ln(y1, logits, wts).reshape(batch, seq, D_MODEL)


def kernel(x_prompt, x_sample, w_in, ret_decay_fwd, ret_decay_bwd, ret_gn_w, mla_q_norm_w, mla_w_uq,
           mla_kv_norm_w, mla_w_ukv, mla_out_norm_w, w_out, ln1_w, ln1_b, w_router, w_gate, w_up,
           w_down, ln2_w, ln2_b):
    depth = w_in.shape[0]
    y_prompt, y_sample = x_prompt, x_sample
    for l in range(depth):
        w_all, wq2, wkv = _prep_weights(w_in[l], mla_w_uq[l], mla_w_ukv[l])
        rep = lambda a: jnp.broadcast_to(a.astype(F32)[:, None, None], (RET_HEADS, 1, LANES))
        wts = dict(
            w_all=w_all, wq2=wq2, wkv=wkv,
            qnw=mla_q_norm_w[l][None, :], kvnw=mla_kv_norm_w[l][None, :],
            dfw=rep(ret_decay_fwd[l]), dbw=rep(ret_decay_bwd[l]), gnw=ret_gn_w[l][None, :],
            wo1=w_out[l][:RET_WIDTH].astype(BF16), wo2=w_out[l][RET_WIDTH:].astype(BF16),
            onw=mla_out_norm_w[l][None, :], ln1w=ln1_w[l][None, :], ln1b=ln1_b[l][None, :],
            wrt=w_router[l].T,
            wg=w_gate[l].astype(BF16), wu=w_up[l].astype(BF16), wd=w_down[l].astype(BF16),
            ln2w=ln2_w[l][None, :], ln2b=ln2_b[l][None, :])
        y_prompt = _layer(y_prompt, wts)
        y_sample = _layer(y_sample, wts)
    return (y_prompt, y_sample)
```

```python
import functools
import math

import numpy as np
import jax
import jax.numpy as jnp
from jax import lax
from jax.experimental import pallas as pl
from jax.experimental.pallas import tpu as pltpu

F32 = jnp.float32
BF16 = jnp.bfloat16

D_MODEL = 1024
RET_WIDTH = 512
RET_HEADS = 4
RET_DK = 128
RET_CHUNK = 128
MLA_HEADS = 8
MLA_D_NOPE = 64
MLA_D_ROPE = 32
MLA_D_V = 64
MLA_Q_LORA = 256
MLA_KV_LORA = 128
N_EXPERTS = 16
EC_CAPACITY_FACTOR = 2
ROPE_BASE = 10000.0
LN_EPS = 1e-5
RMS_EPS = 1e-6
DN_ALPHA = 2.0 ** 0.25

LANES = 128
HEAD_PAD = 128
PROJ_COLS = 4 * RET_WIDTH + MLA_Q_LORA + MLA_KV_LORA + 2 * LANES
VMEM_LIMIT = 56 * 1024 * 1024

TM_PROJ = 512
RET_STEP_CHUNKS = 8
TQ_ATT = 512
VT_CHUNKS = 2
VT_ROWS = 80
TK_ATT = VT_CHUNKS * TM_PROJ
KSUB_ATT = 256
TM_MOE = 256
ROW_ALIGN = 16
CHUNK_ROWS = 64
SLAB_ROWS = ROW_ALIGN + CHUNK_ROWS
NOT_ROUTED = -512.0
TM_FFN = 512
BLOCKS_PER_TILE = TM_MOE // LANES

_QK_SCALE = (MLA_D_NOPE + MLA_D_ROPE) ** -0.5 * math.log2(math.e)


def _cparams(sem):
    return pltpu.CompilerParams(dimension_semantics=sem, vmem_limit_bytes=VMEM_LIMIT)


def _inproj_kernel(x_ref, w_ref, wq_ref, wkv_ref, qnw_ref, kvnw_ref, cm_ref, sm_ref,
                   rq_ref, rk_ref, rv_ref, rg_ref, q_ref, k_ref, vt_ref):
    xb = x_ref[...].astype(BF16)
    proj = jnp.dot(xb, w_ref[...], preferred_element_type=F32)
    rq_ref[...] = proj[:, 0:512]
    rk_ref[...] = proj[:, 512:1024]
    rv_ref[...] = proj[:, 1024:1536]
    rg_ref[...] = proj[:, 1536:2048]
    cm = cm_ref[...]
    sm = sm_ref[...]

    cq = proj[:, 2048:2304]
    cqn = cq * lax.rsqrt(jnp.mean(cq * cq, axis=-1, keepdims=True) + RMS_EPS) * qnw_ref[...]
    qq = jnp.dot(cqn.astype(BF16), wq_ref[...], preferred_element_type=F32)

    ckv = proj[:, 2304:2432]
    ckvn = ckv * lax.rsqrt(jnp.mean(ckv * ckv, axis=-1, keepdims=True) + RMS_EPS) * kvnw_ref[...]
    kv = jnp.dot(ckvn.astype(BF16), wkv_ref[...], preferred_element_type=F32)

    krope = proj[:, 2432:2560] * cm + proj[:, 2560:2688] * sm
    for h in range(MLA_HEADS):
        lo = h * HEAD_PAD
        qh = qq[:, lo:lo + HEAD_PAD] * cm + qq[:, 1024 + lo:1024 + lo + HEAD_PAD] * sm
        q_ref[:, lo:lo + HEAD_PAD] = (qh * _QK_SCALE).astype(BF16)
        k_ref[:, lo:lo + HEAD_PAD] = (kv[:, lo:lo + HEAD_PAD] + krope).astype(BF16)
    vt = kv[:, 1024:1536].T.astype(BF16)
    pad_rows = VT_ROWS - MLA_D_V
    ones_row = (lax.broadcasted_iota(jnp.int32, (pad_rows, vt.shape[1]), 0) == 0).astype(BF16)
    for h in range(MLA_HEADS):
        vt_ref[0, 0, h * VT_ROWS:h * VT_ROWS + MLA_D_V, :] = vt[h * MLA_D_V:(h + 1) * MLA_D_V, :]
        vt_ref[0, 0, h * VT_ROWS + MLA_D_V:(h + 1) * VT_ROWS, :] = ones_row


def _inproj(x2d, seq, w_all, wq2, wkv, qnw, kvnw, cm, sm):
    t = x2d.shape[0]
    tm = TM_PROJ
    nseq = seq // tm
    row = lambda i: (i, 0)
    const = lambda i: (0, 0)
    tab = lambda i: (i % nseq, 0)
    outs = [jax.ShapeDtypeStruct((t, RET_WIDTH), F32)] * 4 + [
        jax.ShapeDtypeStruct((t, MLA_HEADS * HEAD_PAD), BF16),
        jax.ShapeDtypeStruct((t, MLA_HEADS * HEAD_PAD), BF16),
        jax.ShapeDtypeStruct((t // seq, nseq, MLA_HEADS * VT_ROWS, tm), BF16)]
    return pl.pallas_call(
        _inproj_kernel,
        grid=(t // tm,),
        in_specs=[pl.BlockSpec((tm, D_MODEL), row),
                  pl.BlockSpec((D_MODEL, PROJ_COLS), const),
                  pl.BlockSpec((MLA_Q_LORA, 2 * MLA_HEADS * HEAD_PAD), const),
                  pl.BlockSpec((MLA_KV_LORA, MLA_HEADS * (HEAD_PAD + MLA_D_V)), const),
                  pl.BlockSpec((1, MLA_Q_LORA), const),
                  pl.BlockSpec((1, MLA_KV_LORA), const),
                  pl.BlockSpec((tm, LANES), tab),
                  pl.BlockSpec((tm, LANES), tab)],
        out_specs=[pl.BlockSpec((tm, RET_WIDTH), row)] * 4 + [
            pl.BlockSpec((tm, MLA_HEADS * HEAD_PAD), row),
            pl.BlockSpec((tm, MLA_HEADS * HEAD_PAD), row),
            pl.BlockSpec((1, 1, MLA_HEADS * VT_ROWS, tm), lambda i: (i // nseq, i % nseq, 0, 0))],
        out_shape=outs,
        compiler_params=_cparams(("parallel",)),
        name="inproj",
    )(x2d, w_all, wq2, wkv, qnw, kvnw, cm, sm)


def _log_sigmoid(x):
    return jnp.minimum(x, 0.0) - jnp.log(1.0 + jnp.exp(-jnp.abs(x)))


def _ret_kernel(q_ref, k_ref, v_ref, g_ref, cos_ref, sin_ref, df_ref, db_ref, gnw_ref,
                o_ref, sf_ref, sb_ref, sbst_ref, kvf_ref, kb_ref, vb_ref, *, nsteps):
    phase = pl.program_id(2)
    step = pl.program_id(3)
    c = RET_CHUNK
    lgf = _log_sigmoid(df_ref[0])
    lgb = _log_sigmoid(db_ref[0])
    rpos = lax.broadcasted_iota(jnp.int32, (c, c), 0).astype(F32)
    cpos = lax.broadcasted_iota(jnp.int32, (c, c), 1).astype(F32)
    tn = (((0,), (0,)), ((), ()))

    def rope(x, cs, sn):
        return x * cs + pltpu.roll(x, RET_DK // 2, axis=1) * sn

    @pl.when(jnp.logical_and(phase == 0, step == 0))
    def _():
        sb_ref[...] = jnp.zeros_like(sb_ref)

    @pl.when(jnp.logical_and(phase == 1, step == 0))
    def _():
        sf_ref[...] = jnp.zeros_like(sf_ref)

    @pl.when(phase == 0)
    def _():
        kwb = jnp.exp(lgb * rpos)
        kwf = jnp.exp(lgf * (float(c - 1) - rpos))
        cdb = jnp.exp(lgb * float(c))
        kvstep = nsteps - 1 - step
        kvs = {}
        for g in range(RET_STEP_CHUNKS):
            rows = slice(g * c, (g + 1) * c)
            idx = kvstep * RET_STEP_CHUNKS + g
            kr = rope(k_ref[rows, :], cos_ref[rows, :], sin_ref[rows, :]) * (RET_DK ** -0.5)
            vb = v_ref[rows, :].astype(BF16)
            kb_ref[idx] = kr.astype(BF16)
            vb_ref[idx] = vb
            kw = jnp.concatenate([(kr * kwb).astype(BF16), (kr * kwf).astype(BF16)], axis=1)
            kvs[g] = lax.dot_general(kw, vb, tn, preferred_element_type=F32)
        sb = sb_ref[...]
        for g in range(RET_STEP_CHUNKS - 1, -1, -1):
            idx = kvstep * RET_STEP_CHUNKS + g
            sbst_ref[idx] = sb.astype(BF16)
            kvf_ref[idx] = kvs[g][c:2 * c]
            sb = cdb * sb + kvs[g][0:c]
        sb_ref[...] = sb

    @pl.when(phase == 1)
    def _():
        diff = rpos - cpos
        dmat = jnp.where(diff >= 0, jnp.exp(lgf * jnp.maximum(diff, 0.0)),
                         jnp.exp(lgb * jnp.maximum(-diff, 0.0)))
        qwf = jnp.exp(lgf * (rpos + 1.0))
        qwb = jnp.exp(lgb * (float(c) - rpos))
        cdf = jnp.exp(lgf * float(c))
        gnw = gnw_ref[...]
        qrs, scs, outs = {}, {}, {}
        sfs = {0: sf_ref[...]}

        def scores(g):
            rows = slice(g * c, (g + 1) * c)
            idx = step * RET_STEP_CHUNKS + g
            qrs[g] = rope(q_ref[rows, :], cos_ref[rows, :], sin_ref[rows, :])
            scs[g] = lax.dot_general(qrs[g].astype(BF16), kb_ref[idx], (((1,), (1,)), ((), ())),
                                     preferred_element_type=F32)
            sfs[g + 1] = cdf * sfs[g] + kvf_ref[idx]

        def mix(g):
            idx = step * RET_STEP_CHUNKS + g
            qr = qrs.pop(g)
            lhs = jnp.concatenate([(scs.pop(g) * dmat).astype(BF16), (qr * qwf).astype(BF16),
                                   (qr * qwb).astype(BF16)], axis=1)
            rhs = jnp.concatenate([vb_ref[idx], sfs[g].astype(BF16), sbst_ref[idx]], axis=0)
            outs[g] = jnp.dot(lhs, rhs, preferred_element_type=F32)

        def finish(g):
            rows = slice(g * c, (g + 1) * c)
            o = outs.pop(g)
            mu = jnp.mean(o, axis=-1, keepdims=True)
            oc = o - mu
            var = jnp.mean(oc * oc, axis=-1, keepdims=True)
            on = oc * lax.rsqrt(var + LN_EPS) * gnw
            gt = g_ref[rows, :]
            o_ref[rows, :] = (gt / (1.0 + jnp.exp(-gt)) * on).astype(o_ref.dtype)

        for s in range(RET_STEP_CHUNKS + 4):
            if s - 4 >= 0:
                finish(s - 4)
            if 0 <= s - 2 < RET_STEP_CHUNKS:
                mix(s - 2)
            if s < RET_STEP_CHUNKS:
                scores(s)
        sf_ref[...] = sfs[RET_STEP_CHUNKS]


def _retention(rq, rk, rv, rg, cosr, sinr, dfw, dbw, gnw, batch, seq):
    t = rq.shape[0]
    ts = RET_STEP_CHUNKS * RET_CHUNK
    nsteps = seq // ts
    nchunks = seq // RET_CHUNK
    kvi = lambda p, i: jnp.where(p == 0, nsteps - 1 - i, 0)
    qi = lambda p, i: jnp.where(p == 0, 0, i)
    kv_map = lambda b, h, p, i: (b * nsteps + kvi(p, i), h)
    q_map = lambda b, h, p, i: (b * nsteps + qi(p, i), h)
    tab_map = lambda b, h, p, i: (jnp.where(p == 0, nsteps - 1 - i, i), 0)
    head3 = lambda b, h, p, i: (h, 0, 0)
    return pl.pallas_call(
        functools.partial(_ret_kernel, nsteps=nsteps),
        grid=(batch, RET_HEADS, 2, nsteps),
        in_specs=[pl.BlockSpec((ts, RET_DK), q_map),
                  pl.BlockSpec((ts, RET_DK), kv_map),
                  pl.BlockSpec((ts, RET_DK), kv_map),
                  pl.BlockSpec((ts, RET_DK), q_map),
                  pl.BlockSpec((ts, RET_DK), tab_map),
                  pl.BlockSpec((ts, RET_DK), tab_map),
                  pl.BlockSpec((1, 1, LANES), head3),
                  pl.BlockSpec((1, 1, LANES), head3),
                  pl.BlockSpec((1, RET_DK), lambda b, h, p, i: (0, h))],
        out_specs=pl.BlockSpec((ts, RET_DK), q_map),
        out_shape=jax.ShapeDtypeStruct((t, RET_WIDTH), BF16),
        scratch_shapes=[pltpu.VMEM((RET_DK, RET_DK), F32),
                        pltpu.VMEM((RET_DK, RET_DK), F32),
                        pltpu.VMEM((nchunks, RET_DK, RET_DK), BF16),
                        pltpu.VMEM((nchunks, RET_DK, RET_DK), F32),
                        pltpu.VMEM((nchunks, RET_CHUNK, RET_DK), BF16),
                        pltpu.VMEM((nchunks, RET_CHUNK, RET_DK), BF16)],
        compiler_params=_cparams(("parallel", "parallel", "arbitrary", "arbitrary")),
        name="retention",
    )(rq, rk, rv, rg, cosr, sinr, dfw, dbw, gnw)


def _attn_kernel(q_ref, k_ref, vt_ref, o_ref, ste_ref, sto_ref, *, seq):
    nkv = seq // TK_ATT
    npair = (seq // TQ_ATT) * nkv
    nt = (((1,), (1,)), ((), ()))
    nsub = TK_ATT // KSUB_ATT
    sub_per_chunk = TM_PROJ // KSUB_ATT
    init = (jnp.full((1, TQ_ATT), -jnp.inf, F32), jnp.zeros((VT_ROWS, TQ_ATT), F32))

    def score_steps(t, st_ref, tile_max):
        t = jnp.minimum(t, npair - 1)
        qrows = pl.ds(pl.multiple_of((t // nkv) * TQ_ATT, TQ_ATT), TQ_ATT)
        ks = pl.multiple_of((t % nkv) * TK_ATT, TK_ATT)

        def step(h, s):
            def run():
                hs = slice(h * HEAD_PAD, (h + 1) * HEAD_PAD)
                sc = lax.dot_general(k_ref[pl.ds(ks + s * KSUB_ATT, KSUB_ATT), hs], q_ref[qrows, hs], nt,
                                     preferred_element_type=F32)
                st_ref[h, s * KSUB_ATT:(s + 1) * KSUB_ATT, :] = sc
                top = jnp.max(sc, axis=0, keepdims=True)
                tile_max[h] = top if s == 0 else jnp.maximum(tile_max[h], top)
            return run
        return [step(h, s) for h in range(2) for s in range(nsub)]

    def softmax_steps(t, st_ref, stats, tile_max, out):
        kj = t % nkv
        steps = []
        for h in range(2):
            state = {}

            def start(h=h, state=state):
                m = jnp.where(kj == 0, init[0], stats[h][0])
                acc = jnp.where(kj == 0, init[1], stats[h][1])
                state["mn"] = jnp.maximum(m, tile_max[h])
                state["pv"] = jnp.exp2(m - state["mn"]) * acc

            def part(s, h=h, state=state):
                def run():
                    x = st_ref[h, s * KSUB_ATT:(s + 1) * KSUB_ATT, :] - state["mn"]
                    pb = jnp.exp2(x.astype(BF16))
                    c, r = divmod(s, sub_per_chunk)
                    state["pv"] = state["pv"] + jnp.dot(
                        vt_ref[0, kj * VT_CHUNKS + c, h * VT_ROWS:(h + 1) * VT_ROWS,
                               r * KSUB_ATT:(r + 1) * KSUB_ATT],
                        pb, preferred_element_type=F32)
                    if s == nsub - 1:
                        out[h] = (state["mn"], state["pv"])
                return run
            steps += [start] + [part(s) for s in range(nsub)]
        return steps

    def interleave(mxu_steps, vpu_steps):
        n = max(len(mxu_steps), len(vpu_steps))
        for i in range(n):
            if i < len(mxu_steps):
                mxu_steps[i]()
            if i < len(vpu_steps):
                vpu_steps[i]()

    def body(jj, carry):
        stats, max_even = carry
        t0 = 2 * jj
        mid, out, max_odd, max_next = {}, {}, {}, {}
        interleave(score_steps(t0 + 1, sto_ref, max_odd), softmax_steps(t0, ste_ref, stats, max_even, mid))
        interleave(score_steps(t0 + 2, ste_ref, max_next),
                   softmax_steps(t0 + 1, sto_ref, (mid[0], mid[1]), max_odd, out))

        @pl.when((t0 + 1) % nkv == nkv - 1)
        def _():
            qi = t0 // nkv
            for h in range(2):
                acc = out[h][1]
                o_ref[0, qi, h * MLA_D_V:(h + 1) * MLA_D_V, :] = acc[0:MLA_D_V] / acc[MLA_D_V:MLA_D_V + 1]
        return (out[0], out[1]), (max_next[0], max_next[1])

    max_first = {}
    for run in score_steps(0, ste_ref, max_first):
        run()
    lax.fori_loop(0, npair // 2, body, ((init, init), (max_first[0], max_first[1])))


def _attention(q, k, vt, batch, seq):
    nq = seq // TQ_ATT
    nvt = seq // TM_PROJ
    assert (seq // TK_ATT) % 2 == 0, seq
    return pl.pallas_call(
        functools.partial(_attn_kernel, seq=seq),
        grid=(batch, MLA_HEADS // 2),
        in_specs=[pl.BlockSpec((seq, 2 * HEAD_PAD), lambda b, hp: (b, hp)),
                  pl.BlockSpec((seq, 2 * HEAD_PAD), lambda b, hp: (b, hp)),
                  pl.BlockSpec((1, nvt, 2 * VT_ROWS, TM_PROJ), lambda b, hp: (b, 0, hp, 0))],
        out_specs=pl.BlockSpec((1, nq, 2 * MLA_D_V, TQ_ATT), lambda b, hp: (b, 0, hp, 0)),
        out_shape=jax.ShapeDtypeStruct((batch, nq, MLA_HEADS * MLA_D_V, TQ_ATT), F32),
        scratch_shapes=[pltpu.VMEM((2, TK_ATT, TQ_ATT), F32),
                        pltpu.VMEM((2, TK_ATT, TQ_ATT), F32)],
        compiler_params=_cparams(("parallel", "parallel")),
        name="attention",
    )(q, k, vt)


def _layer_norm(z, w, b):
    mu = jnp.mean(z, axis=-1, keepdims=True)
    zc = z - mu
    var = jnp.mean(zc * zc, axis=-1, keepdims=True)
    return zc * lax.rsqrt(var + LN_EPS) * w + b


def _split_bf16(a):
    hi = a.astype(BF16)
    lo = (a - hi.astype(F32)).astype(BF16)
    return hi, lo


def _outproj_kernel(x_ref, ret_ref, att_ref, wo1_ref, wo2_ref, onw_ref, lw_ref, lb_ref, wr_ref,
                    y_ref, lg_ref):
    att = att_ref[0, 0].T
    an = att * lax.rsqrt(jnp.mean(att * att, axis=-1, keepdims=True) + RMS_EPS) * onw_ref[...]
    mix = jnp.dot(ret_ref[...], wo1_ref[...], preferred_element_type=F32)
    mix += jnp.dot(an.astype(BF16), wo2_ref[...], preferred_element_type=F32)
    y = _layer_norm(DN_ALPHA * x_ref[...] + mix, lw_ref[...], lb_ref[...])
    y_ref[...] = y
    yh, yl = _split_bf16(y)
    wh, wl = _split_bf16(wr_ref[...])
    nt = (((1,), (1,)), ((), ()))
    lt = lax.dot_general(wh, yh, nt, preferred_element_type=F32)
    lt += lax.dot_general(wh, yl, nt, preferred_element_type=F32)
    lt += lax.dot_general(wl, yh, nt, preferred_element_type=F32)
    for cblk in range(lt.shape[1] // LANES):
        lg_ref[cblk] = lt[:, cblk * LANES:(cblk + 1) * LANES]


def _outproj(x2d, ret, att_t, wo1, wo2, onw, lw, lb, wrt):
    t = x2d.shape[0]
    tm = TM_PROJ
    nseq = att_t.shape[1]
    assert att_t.shape[3] == tm
    row = lambda i: (i, 0)
    const = lambda i: (0, 0)
    return pl.pallas_call(
        _outproj_kernel,
        grid=(t // tm,),
        in_specs=[pl.BlockSpec((tm, D_MODEL), row),
                  pl.BlockSpec((tm, RET_WIDTH), row),
                  pl.BlockSpec((1, 1, RET_WIDTH, tm), lambda i: (i // nseq, i % nseq, 0, 0)),
                  pl.BlockSpec((RET_WIDTH, D_MODEL), const),
                  pl.BlockSpec((RET_WIDTH, D_MODEL), const),
                  pl.BlockSpec((1, RET_WIDTH), const),
                  pl.BlockSpec((1, D_MODEL), const),
                  pl.BlockSpec((1, D_MODEL), const),
                  pl.BlockSpec((N_EXPERTS, D_MODEL), const)],
        out_specs=[pl.BlockSpec((tm, D_MODEL), row),
                   pl.BlockSpec((tm // LANES, N_EXPERTS, LANES), lambda i: (i, 0, 0))],
        out_shape=[jax.ShapeDtypeStruct((t, D_MODEL), F32),
                   jax.ShapeDtypeStruct((t // LANES, N_EXPERTS, LANES), F32)],
        compiler_params=_cparams(("parallel",)),
        name="outproj",
    )(x2d, ret, att_t, wo1, wo2, onw, lw, lb, wrt)


def _route_kernel(lg_ref, rank_ref, gate_ref, cnt_ref, off_ref, blk_ref, boff_ref, *, cap, nb):
    e = N_EXPERTS
    ntiles = nb // BLOCKS_PER_TILE
    lg = lg_ref[...]
    ex = jnp.exp(lg - jnp.max(lg, axis=1, keepdims=True))
    aff = ex / jnp.sum(ex, axis=1, keepdims=True)
    gate_ref[...] = aff
    key = pltpu.bitcast(aff, jnp.int32)

    def count(mask):
        part = jnp.sum(mask.astype(F32), axis=0)
        return jnp.broadcast_to(jnp.sum(part, axis=1, keepdims=True), (e, LANES))

    def bisect(_, lohi):
        lo, hi = lohi
        mid = lo + lax.shift_right_logical(hi - lo, 1)
        ok = count(key >= mid[None]) >= float(cap)
        return jnp.where(ok, mid, lo), jnp.where(ok, hi, mid)

    lo0 = jnp.zeros((e, LANES), jnp.int32)
    hi0 = jnp.full((e, LANES), 0x7F800000, jnp.int32)
    thr, _ = lax.fori_loop(0, 31, bisect, (lo0, hi0))
    gt = key > thr[None]
    eq = key == thr[None]
    need = float(cap) - count(gt)

    ii = lax.broadcasted_iota(jnp.int32, (LANES, LANES), 0)
    jj = lax.broadcasted_iota(jnp.int32, (LANES, LANES), 1)
    upper = (ii <= jj).astype(BF16)
    ones = jnp.ones((LANES, LANES), BF16)

    def block_scan(mask):
        m2 = mask.astype(BF16).reshape(nb * e, LANES)
        incl = jnp.dot(m2, upper, preferred_element_type=F32).reshape(nb, e, LANES)
        tot = jnp.dot(m2, ones, preferred_element_type=F32).reshape(nb, e, LANES)
        return incl, tot

    def leading_excl_scan(src_ref, dst_ref, n):
        def step(j, carry):
            dst_ref[j] = carry
            return carry + src_ref[j]
        return lax.fori_loop(0, n, step, jnp.zeros((e, LANES), F32))

    eqf = eq.astype(F32)
    incl, tot = block_scan(eq)
    blk_ref[...] = tot
    leading_excl_scan(blk_ref, boff_ref, nb)
    eq_before = incl - eqf + boff_ref[...]
    sel = jnp.logical_or(gt, jnp.logical_and(eq, eq_before < need[None]))

    self32 = sel.astype(F32)
    incl, tot = block_scan(sel)
    local = (incl - self32).reshape(ntiles, BLOCKS_PER_TILE, e, LANES)
    tot4 = tot.reshape(ntiles, BLOCKS_PER_TILE, e, LANES)
    ranks = [local[:, 0]]
    run = tot4[:, 0]
    for b in range(1, BLOCKS_PER_TILE):
        ranks.append(local[:, b] + run)
        run = run + tot4[:, b]
    rank = jnp.stack(ranks, axis=1).reshape(nb, e, LANES)
    rank_ref[...] = jnp.where(sel, rank, NOT_ROUTED)
    cnt_ref[...] = run
    blk_ref[0:ntiles] = run
    leading_excl_scan(blk_ref, boff_ref, ntiles)
    off_ref[...] = boff_ref[0:ntiles]


def _route(logits, cap):
    nb = logits.shape[0]
    ntiles = nb // BLOCKS_PER_TILE
    big = jax.ShapeDtypeStruct((nb, N_EXPERTS, LANES), F32)
    small = jax.ShapeDtypeStruct((ntiles, N_EXPERTS, LANES), F32)
    return pl.pallas_call(
        functools.partial(_route_kernel, cap=cap, nb=nb),
        out_shape=[big, big, small, small],
        scratch_shapes=[pltpu.VMEM((nb, N_EXPERTS, LANES), F32),
                        pltpu.VMEM((nb, N_EXPERTS, LANES), F32)],
        compiler_params=pltpu.CompilerParams(vmem_limit_bytes=VMEM_LIMIT),
        name="route",
    )(logits)


def _lead(off_ref, tile, ex):
    pos = off_ref[tile * N_EXPERTS + ex]
    lead = jnp.bitwise_and(pos, ROW_ALIGN - 1)
    return pos - lead, lead


def _overflow_chunks(lead, cnt):
    return lax.shift_right_logical(jnp.maximum(lead + cnt - SLAB_ROWS, 0) + (CHUNK_ROWS - 1), 6)


def _compact_kernel(off_ref, cnt_ref, y_ref, rank_ref, xg_ref, p_ref, pc_ref, st_ref, ost_ref,
                    carry_ref, zero_ref, sem, osem, *, cap, cap_rows):
    i = pl.program_id(0)
    nt = pl.num_programs(0)
    slot = i % 2

    def slab(tile, slt, ex):
        start, _ = _lead(off_ref, tile, ex)
        return pltpu.make_async_copy(
            st_ref.at[slt, pl.ds(ex * SLAB_ROWS, SLAB_ROWS)],
            xg_ref.at[pl.ds(pl.multiple_of(ex * cap_rows + start, ROW_ALIGN), SLAB_ROWS)],
            sem.at[slt])

    @pl.when(i == 0)
    def _():
        carry_ref[...] = jnp.zeros_like(carry_ref)
        zero_ref[...] = jnp.zeros_like(zero_ref)
        tails = [pltpu.make_async_copy(zero_ref, xg_ref.at[pl.ds(ex * cap_rows + cap, cap_rows - cap)],
                                       osem.at[0]) for ex in range(N_EXPERTS)]
        for cp in tails:
            cp.start()
        for cp in tails:
            cp.wait()

    xb = y_ref[...].astype(BF16)
    rank = jnp.concatenate([rank_ref[b] for b in range(BLOCKS_PER_TILE)], axis=1)
    jslab = lax.broadcasted_iota(jnp.int32, (SLAB_ROWS, TM_MOE), 0).astype(F32)
    jgrp = lax.broadcasted_iota(jnp.int32, (ROW_ALIGN, TM_MOE), 0).astype(F32)
    jchunk = lax.broadcasted_iota(jnp.int32, (CHUNK_ROWS, TM_MOE), 0).astype(F32)
    shifted, last_group = [], []
    for ex in range(N_EXPERTS):
        _, lead = _lead(off_ref, i, ex)
        sh = rank[ex:ex + 1, :] + lead.astype(F32)
        grp = lax.shift_right_logical(lead + cnt_ref[i * N_EXPERTS + ex], 4)
        p_ref[ex * SLAB_ROWS:(ex + 1) * SLAB_ROWS, :] = (sh == jslab).astype(BF16)
        pc_ref[ex * ROW_ALIGN:(ex + 1) * ROW_ALIGN, :] = (
            sh - (grp * ROW_ALIGN).astype(F32) == jgrp).astype(BF16)
        shifted.append(sh)
        last_group.append(grp)
    rows = jnp.dot(p_ref[...], xb, preferred_element_type=F32)
    tail = jnp.dot(pc_ref[...], xb, preferred_element_type=F32)
    for ex in range(N_EXPERTS):
        lo = ex * SLAB_ROWS
        old = carry_ref[ex]
        st_ref[slot, lo:lo + ROW_ALIGN, :] = (rows[lo:lo + ROW_ALIGN] + old).astype(BF16)
        st_ref[slot, lo + ROW_ALIGN:lo + SLAB_ROWS, :] = rows[lo + ROW_ALIGN:lo + SLAB_ROWS].astype(BF16)
        carry_ref[ex] = tail[ex * ROW_ALIGN:(ex + 1) * ROW_ALIGN] + jnp.where(last_group[ex] == 0, old, 0.0)

    @pl.when(i >= 1)
    def _():
        for ex in range(N_EXPERTS):
            slab(i - 1, 1 - slot, ex).wait()

    for ex in range(N_EXPERTS):
        slab(i, slot, ex).start()

    for ex in range(N_EXPERTS):
        start, lead = _lead(off_ref, i, ex)

        def chunk(c, carry, ex=ex, start=start):
            first = SLAB_ROWS + c * CHUNK_ROWS
            pc = (shifted[ex] == jchunk + first.astype(F32)).astype(BF16)
            ost_ref[...] = jnp.dot(pc, xb, preferred_element_type=F32).astype(BF16)
            cp = pltpu.make_async_copy(
                ost_ref,
                xg_ref.at[pl.ds(pl.multiple_of(ex * cap_rows + start + first, ROW_ALIGN), CHUNK_ROWS)],
                osem.at[0])
            cp.start()
            cp.wait()
            return carry

        lax.fori_loop(0, _overflow_chunks(lead, cnt_ref[i * N_EXPERTS + ex]), chunk, 0)

    @pl.when(i == nt - 1)
    def _():
        for ex in range(N_EXPERTS):
            slab(i, slot, ex).wait()


def _compact(off, cnt, y1, rank_em, cap, cap_rows):
    n = y1.shape[0]
    ntiles = n // TM_MOE
    gs = pltpu.PrefetchScalarGridSpec(
        num_scalar_prefetch=2,
        grid=(ntiles,),
        in_specs=[pl.BlockSpec((TM_MOE, D_MODEL), lambda i, o, c: (i, 0)),
                  pl.BlockSpec((BLOCKS_PER_TILE, N_EXPERTS, LANES), lambda i, o, c: (i, 0, 0))],
        out_specs=pl.BlockSpec(memory_space=pl.ANY),
        scratch_shapes=[pltpu.VMEM((N_EXPERTS * SLAB_ROWS, TM_MOE), BF16),
                        pltpu.VMEM((N_EXPERTS * ROW_ALIGN, TM_MOE), BF16),
                        pltpu.VMEM((2, N_EXPERTS * SLAB_ROWS, D_MODEL), BF16),
                        pltpu.VMEM((CHUNK_ROWS, D_MODEL), BF16),
                        pltpu.VMEM((N_EXPERTS, ROW_ALIGN, D_MODEL), F32),
                        pltpu.VMEM((cap_rows - cap, D_MODEL), BF16),
                        pltpu.SemaphoreType.DMA((2,)),
                        pltpu.SemaphoreType.DMA((1,))])
    return pl.pallas_call(
        functools.partial(_compact_kernel, cap=cap, cap_rows=cap_rows),
        grid_spec=gs,
        out_shape=jax.ShapeDtypeStruct((N_EXPERTS * cap_rows, D_MODEL), BF16),
        compiler_params=_cparams(("arbitrary",)),
        name="compact",
    )(off, cnt, y1, rank_em)


def _ffn_kernel(x_ref, wg_ref, wu_ref, wd_ref, y_ref, *, nvalid):
    j = pl.program_id(1)

    @pl.when(j < nvalid)
    def _():
        x = x_ref[...]
        hg = jnp.dot(x, wg_ref[0], preferred_element_type=F32)
        hu = jnp.dot(x, wu_ref[0], preferred_element_type=F32)
        h = (hg / (1.0 + jnp.exp(-hg)) * hu).astype(BF16)
        y_ref[...] = jnp.dot(h, wd_ref[0], preferred_element_type=F32).astype(BF16)

    @pl.when(j >= nvalid)
    def _():
        y_ref[...] = jnp.zeros_like(y_ref)


def _ffn(xg, wg, wu, wd, cap, cap_rows):
    nvalid = cap // TM_FFN
    nt = cap_rows // TM_FFN
    wmap = lambda ex, j: (ex, 0, 0)
    return pl.pallas_call(
        functools.partial(_ffn_kernel, nvalid=nvalid),
        grid=(N_EXPERTS, nt),
        in_specs=[pl.BlockSpec((TM_FFN, D_MODEL), lambda ex, j: (ex * nt + jnp.minimum(j, nvalid - 1), 0)),
                  pl.BlockSpec((1, D_MODEL, D_MODEL), wmap),
                  pl.BlockSpec((1, D_MODEL, D_MODEL), wmap),
                  pl.BlockSpec((1, D_MODEL, D_MODEL), wmap)],
        out_specs=pl.BlockSpec((TM_FFN, D_MODEL), lambda ex, j: (ex * nt + j, 0)),
        out_shape=jax.ShapeDtypeStruct((N_EXPERTS * cap_rows, D_MODEL), BF16),
        compiler_params=_cparams(("arbitrary", "arbitrary")),
        name="ffn",
    )(xg, wg, wu, wd)


def _combine_kernel(off_ref, cnt_ref, y1_ref, rank_ref, gate_ref, yh_ref, lw_ref, lb_ref,
                    o_ref, ybuf, obuf, acc_ref, sem, osem, *, cap_rows):
    i = pl.program_id(0)
    nt = pl.num_programs(0)
    slot = i % 2

    ncol = N_EXPERTS * SLAB_ROWS

    def fetch(tile, slt, fn):
        for ex in range(N_EXPERTS):
            start, _ = _lead(off_ref, tile, ex)
            fn(pltpu.make_async_copy(
                yh_ref.at[pl.ds(pl.multiple_of(ex * cap_rows + start, ROW_ALIGN), SLAB_ROWS)],
                ybuf.at[slt, pl.ds(ex * SLAB_ROWS, SLAB_ROWS)],
                sem.at[slt]))

    @pl.when(i == 0)
    def _():
        fetch(0, 0, lambda cp: cp.start())

    @pl.when(i + 1 < nt)
    def _():
        fetch(i + 1, 1 - slot, lambda cp: cp.start())

    rank = rank_ref[...]
    gate = gate_ref[...]
    er = lax.broadcasted_iota(jnp.int32, (N_EXPERTS, ncol), 0)
    ec = lax.broadcasted_iota(jnp.int32, (N_EXPERTS, ncol), 1)
    expand = jnp.logical_and(ec >= er * SLAB_ROWS, ec < (er + 1) * SLAB_ROWS).astype(BF16)
    rexp = jnp.dot(rank.astype(BF16), expand, preferred_element_type=F32)
    gexp = jnp.dot(gate.astype(BF16), expand, preferred_element_type=F32)
    col = lax.broadcasted_iota(jnp.int32, (1, ncol), 1)
    want = jnp.zeros((1, ncol), F32)
    for ex in range(N_EXPERTS):
        _, lead = _lead(off_ref, i, ex)
        inside = jnp.logical_and(col >= ex * SLAB_ROWS, col < (ex + 1) * SLAB_ROWS)
        want = jnp.where(inside, (col - ex * SLAB_ROWS - lead).astype(F32), want)
    gmat = jnp.where(rexp == want, gexp, 0.0).astype(BF16)

    fetch(i, slot, lambda cp: cp.wait())
    acc_ref[...] = jnp.dot(gmat, ybuf[slot], preferred_element_type=F32)

    jc = lax.broadcasted_iota(jnp.int32, (TM_MOE, CHUNK_ROWS), 1).astype(F32)
    for ex in range(N_EXPERTS):
        start, lead = _lead(off_ref, i, ex)

        def chunk(c, carry, ex=ex, start=start, lead=lead):
            first = SLAB_ROWS + c * CHUNK_ROWS
            cp = pltpu.make_async_copy(
                yh_ref.at[pl.ds(pl.multiple_of(ex * cap_rows + start + first, ROW_ALIGN), CHUNK_ROWS)],
                obuf, osem.at[0])
            cp.start()
            cp.wait()
            gc = jnp.where(rank[:, ex:ex + 1] + lead.astype(F32) == jc + first.astype(F32),
                           gate[:, ex:ex + 1].astype(BF16).astype(F32), 0.0).astype(BF16)
            acc_ref[...] += jnp.dot(gc, obuf[...], preferred_element_type=F32)
            return carry

        lax.fori_loop(0, _overflow_chunks(lead, cnt_ref[i * N_EXPERTS + ex]), chunk, 0)

    o_ref[...] = _layer_norm(DN_ALPHA * y1_ref[...] + acc_ref[...], lw_ref[...], lb_ref[...])


def _combine(off, cnt, y1, rank_tm, gate_tm, yh, lw, lb, cap_rows):
    n = y1.shape[0]
    ntiles = n // TM_MOE
    row = lambda i, o, c: (i, 0)
    const = lambda i, o, c: (0, 0)
    gs = pltpu.PrefetchScalarGridSpec(
        num_scalar_prefetch=2,
        grid=(ntiles,),
        in_specs=[pl.BlockSpec((TM_MOE, D_MODEL), row),
                  pl.BlockSpec((TM_MOE, N_EXPERTS), row),
                  pl.BlockSpec((TM_MOE, N_EXPERTS), row),
                  pl.BlockSpec(memory_space=pl.ANY),
                  pl.BlockSpec((1, D_MODEL), const),
                  pl.BlockSpec((1, D_MODEL), const)],
        out_specs=pl.BlockSpec((TM_MOE, D_MODEL), row),
        scratch_shapes=[pltpu.VMEM((2, N_EXPERTS * SLAB_ROWS, D_MODEL), BF16),
                        pltpu.VMEM((CHUNK_ROWS, D_MODEL), BF16),
                        pltpu.VMEM((TM_MOE, D_MODEL), F32),
                        pltpu.SemaphoreType.DMA((2,)),
                        pltpu.SemaphoreType.DMA((1,))])
    return pl.pallas_call(
        functools.partial(_combine_kernel, cap_rows=cap_rows),
        grid_spec=gs,
        out_shape=jax.ShapeDtypeStruct((n, D_MODEL), F32),
        compiler_params=_cparams(("arbitrary",)),
        name="combine",
    )(off, cnt, y1, rank_tm, gate_tm, yh, lw, lb)


def _rope_tables(seq):
    pos = jnp.arange(seq, dtype=F32)[:, None]
    inv_m = 1.0 / (ROPE_BASE ** (jnp.arange(0, MLA_D_ROPE, 2, dtype=F32) / MLA_D_ROPE))
    am = pos * inv_m[None, :]
    one = jnp.ones((seq, MLA_D_NOPE), F32)
    zero_n = jnp.zeros((seq, MLA_D_NOPE), F32)
    zero_p = jnp.zeros((seq, HEAD_PAD - MLA_D_NOPE - MLA_D_ROPE), F32)
    cm = jnp.concatenate([one, jnp.cos(am), jnp.cos(am), zero_p], axis=1)
    sm = jnp.concatenate([zero_n, jnp.sin(am), jnp.sin(am), zero_p], axis=1)
    inv_r = 1.0 / (ROPE_BASE ** (jnp.arange(0, RET_DK, 2, dtype=F32) / RET_DK))
    ar = pos * inv_r[None, :]
    cosr = jnp.concatenate([jnp.cos(ar), jnp.cos(ar)], axis=1)
    sinr = jnp.concatenate([-jnp.sin(ar), jnp.sin(ar)], axis=1)
    return cm, sm, cosr, sinr


def _prep_weights(w_in, mla_w_uq, mla_w_ukv):
    half = MLA_D_ROPE // 2
    base = 4 * RET_WIDTH + MLA_Q_LORA + MLA_KV_LORA
    w_kr = w_in[:, base:base + MLA_D_ROPE]
    zn = jnp.zeros((D_MODEL, MLA_D_NOPE), F32)
    zp = jnp.zeros((D_MODEL, HEAD_PAD - MLA_D_NOPE - MLA_D_ROPE), F32)
    kr_p = jnp.concatenate([zn, w_kr, zp], axis=1)
    kr_rot = jnp.concatenate([zn, -w_kr[:, half:], w_kr[:, :half], zp], axis=1)
    w_all = jnp.concatenate([w_in[:, :base], kr_p, kr_rot], axis=1).astype(BF16)

    wq = mla_w_uq.reshape(MLA_Q_LORA, MLA_HEADS, MLA_D_NOPE + MLA_D_ROPE)
    nope, ropew = wq[..., :MLA_D_NOPE], wq[..., MLA_D_NOPE:]
    zq = jnp.zeros((MLA_Q_LORA, MLA_HEADS, HEAD_PAD - MLA_D_NOPE - MLA_D_ROPE), F32)
    q_p = jnp.concatenate([nope, ropew, zq], axis=-1).reshape(MLA_Q_LORA, -1)
    q_rot = jnp.concatenate([jnp.zeros_like(nope), -ropew[..., half:], ropew[..., :half], zq],
                            axis=-1).reshape(MLA_Q_LORA, -1)
    wq2 = jnp.concatenate([q_p, q_rot], axis=1).astype(BF16)

    wkv3 = mla_w_ukv.reshape(MLA_KV_LORA, MLA_HEADS, MLA_D_NOPE + MLA_D_V)
    k_p = jnp.concatenate([wkv3[..., :MLA_D_NOPE],
                           jnp.zeros((MLA_KV_LORA, MLA_HEADS, HEAD_PAD - MLA_D_NOPE), F32)],
                          axis=-1).reshape(MLA_KV_LORA, -1)
    v_w = wkv3[..., MLA_D_NOPE:].reshape(MLA_KV_LORA, -1)
    wkv = jnp.concatenate([k_p, v_w], axis=1).astype(BF16)
    return w_all, wq2, wkv


def _mixer_ln(x, wts):
    batch, seq, _ = x.shape
    x2d = x.reshape(batch * seq, D_MODEL)
    cm, sm, cosr, sinr = _rope_tables(seq)
    rq, rk, rv, rg, q, k, v = _inproj(x2d, seq, wts["w_all"], wts["wq2"], wts["wkv"],
                                      wts["qnw"], wts["kvnw"], cm, sm)
    ret = _retention(rq, rk, rv, rg, cosr, sinr, wts["dfw"], wts["dbw"], wts["gnw"], batch, seq)
    att = _attention(q, k, v, batch, seq)
    return _outproj(x2d, ret, att, wts["wo1"], wts["wo2"], wts["onw"], wts["ln1w"], wts["ln1b"],
                    wts["wrt"])


def _moe_ln(y1, logits, wts):
    n = y1.shape[0]
    cap = EC_CAPACITY_FACTOR * n // N_EXPERTS
    assert cap % TM_FFN == 0, (n, cap)
    cap_rows = cap + TM_FFN
    rank_em, gate_em, cnt_rep, off_rep = _route(logits, cap)
    cnt = cnt_rep[:, :, 0].astype(jnp.int32).reshape(-1)
    off = off_rep[:, :, 0].astype(jnp.int32).reshape(-1)
    to_tm = lambda a: jnp.transpose(a, (0, 2, 1)).reshape(n, N_EXPERTS)
    xg = _compact(off, cnt, y1, rank_em, cap, cap_rows)
    yh = _ffn(xg, wts["wg"], wts["wu"], wts["wd"], cap, cap_rows)
    return _combine(off, cnt, y1, to_tm(rank_em), to_tm(gate_em), yh, wts["ln2w"], wts["ln2b"],
                    cap_rows)


def _layer(x, wts):
    batch, seq, _ = x.shape
    y1, logits = _mixer_ln(x, wts)
    return _moe_ln(y1, logits, wts).reshape(batch, seq, D_MODEL)


def kernel(x_prompt, x_sample, w_in, ret_decay_fwd, ret_decay_bwd, ret_gn_w, mla_q_norm_w, mla_w_uq,
           mla_kv_norm_w, mla_w_ukv, mla_out_norm_w, w_out, ln1_w, ln1_b, w_router, w_gate, w_up,
           w_down, ln2_w, ln2_b):
    depth = w_in.shape[0]
    y_prompt, y_sample = x_prompt, x_sample
    for l in range(depth):
        w_all, wq2, wkv = _prep_weights(w_in[l], mla_w_uq[l], mla_w_ukv[l])
        rep = lambda a: jnp.broadcast_to(a.astype(F32)[:, None, None], (RET_HEADS, 1, LANES))
        wts = dict(
            w_all=w_all, wq2=wq2, wkv=wkv,
            qnw=mla_q_norm_w[l][None, :], kvnw=mla_kv_norm_w[l][None, :],
            dfw=rep(ret_decay_fwd[l]), dbw=rep(ret_decay_bwd[l]), gnw=ret_gn_w[l][None, :],
            wo1=w_out[l][:RET_WIDTH].astype(BF16), wo2=w_out[l][RET_WIDTH:].astype(BF16),
            onw=mla_out_norm_w[l][None, :], ln1w=ln1_w[l][None, :], ln1b=ln1_b[l][None, :],
            wrt=w_router[l].T,
            wg=w_gate[l].astype(BF16), wu=w_up[l].astype(BF16), wd=w_down[l].astype(BF16),
            ln2w=ln2_w[l][None, :], ln2b=ln2_b[l][None, :])
        y_prompt = _layer(y_prompt, wts)
        y_sample = _layer(y_sample, wts)
    return (y_prompt, y_sample)
```

```python
import functools
import math

import numpy as np
import jax
import jax.numpy as jnp
from jax import lax
from jax.experimental import pallas as pl
from jax.experimental.pallas import tpu as pltpu

F32 = jnp.float32
BF16 = jnp.bfloat16

D_MODEL = 1024
RET_WIDTH = 512
RET_HEADS = 4
RET_DK = 128
RET_CHUNK = 128
MLA_HEADS = 8
MLA_D_NOPE = 64
MLA_D_ROPE = 32
MLA_D_V = 64
MLA_Q_LORA = 256
MLA_KV_LORA = 128
N_EXPERTS = 16
EC_CAPACITY_FACTOR = 2
ROPE_BASE = 10000.0
LN_EPS = 1e-5
RMS_EPS = 1e-6
DN_ALPHA = 2.0 ** 0.25

LANES = 128
HEAD_PAD = 128
PROJ_COLS = 4 * RET_WIDTH + MLA_Q_LORA + MLA_KV_LORA + 2 * LANES
VMEM_LIMIT = 56 * 1024 * 1024

TM_PROJ = 512
RET_STEP_CHUNKS = 8
TQ_ATT = 512
VT_CHUNKS = 2
VT_ROWS = 80
TK_ATT = VT_CHUNKS * TM_PROJ
KSUB_ATT = 256
TM_MOE = 256
ROW_ALIGN = 16
CHUNK_ROWS = 64
SLAB_ROWS = ROW_ALIGN + CHUNK_ROWS
SLAB_PIECES = ((0, 32), (32, 16), (48, 32))
THRESHOLD_STEPS = 64
NOT_ROUTED = -512.0
TM_FFN = 512
BLOCKS_PER_TILE = TM_MOE // LANES

_QK_SCALE = (MLA_D_NOPE + MLA_D_ROPE) ** -0.5 * math.log2(math.e)


def _cparams(sem):
    return pltpu.CompilerParams(dimension_semantics=sem, vmem_limit_bytes=VMEM_LIMIT)


def _inproj_kernel(x_ref, w_ref, wq_ref, wkv_ref, qnw_ref, kvnw_ref, cm_ref, sm_ref,
                   rq_ref, rk_ref, rv_ref, rg_ref, qt_ref, k_ref, vt_ref):
    xb = x_ref[...].astype(BF16)
    proj = jnp.dot(xb, w_ref[...], preferred_element_type=F32)
    rq_ref[...] = proj[:, 0:512]
    rk_ref[...] = proj[:, 512:1024]
    rv_ref[...] = proj[:, 1024:1536]
    rg_ref[...] = proj[:, 1536:2048]
    cm = cm_ref[...]
    sm = sm_ref[...]

    cq = proj[:, 2048:2304]
    cqn = cq * lax.rsqrt(jnp.mean(cq * cq, axis=-1, keepdims=True) + RMS_EPS) * qnw_ref[...]
    qq = jnp.dot(cqn.astype(BF16), wq_ref[...], preferred_element_type=F32)

    ckv = proj[:, 2304:2432]
    ckvn = ckv * lax.rsqrt(jnp.mean(ckv * ckv, axis=-1, keepdims=True) + RMS_EPS) * kvnw_ref[...]
    kv = jnp.dot(ckvn.astype(BF16), wkv_ref[...], preferred_element_type=F32)

    krope = proj[:, 2432:2560] * cm + proj[:, 2560:2688] * sm
    for h in range(MLA_HEADS):
        lo = h * HEAD_PAD
        qh = qq[:, lo:lo + HEAD_PAD] * cm + qq[:, 1024 + lo:1024 + lo + HEAD_PAD] * sm
        qt_ref[0, 0, lo:lo + HEAD_PAD, :] = (qh * _QK_SCALE).T.astype(BF16)
        k_ref[:, lo:lo + HEAD_PAD] = (kv[:, lo:lo + HEAD_PAD] + krope).astype(BF16)
    vt = kv[:, 1024:1536].T.astype(BF16)
    pad_rows = VT_ROWS - MLA_D_V
    ones_row = (lax.broadcasted_iota(jnp.int32, (pad_rows, vt.shape[1]), 0) == 0).astype(BF16)
    for h in range(MLA_HEADS):
        vt_ref[0, 0, h * VT_ROWS:h * VT_ROWS + MLA_D_V, :] = vt[h * MLA_D_V:(h + 1) * MLA_D_V, :]
        vt_ref[0, 0, h * VT_ROWS + MLA_D_V:(h + 1) * VT_ROWS, :] = ones_row


def _inproj(x2d, seq, w_all, wq2, wkv, qnw, kvnw, cm, sm):
    t = x2d.shape[0]
    tm = TM_PROJ
    nseq = seq // tm
    row = lambda i: (i, 0)
    const = lambda i: (0, 0)
    tab = lambda i: (i % nseq, 0)
    outs = [jax.ShapeDtypeStruct((t, RET_WIDTH), F32)] * 4 + [
        jax.ShapeDtypeStruct((t // seq, nseq, MLA_HEADS * HEAD_PAD, tm), BF16),
        jax.ShapeDtypeStruct((t, MLA_HEADS * HEAD_PAD), BF16),
        jax.ShapeDtypeStruct((t // seq, nseq, MLA_HEADS * VT_ROWS, tm), BF16)]
    return pl.pallas_call(
        _inproj_kernel,
        grid=(t // tm,),
        in_specs=[pl.BlockSpec((tm, D_MODEL), row),
                  pl.BlockSpec((D_MODEL, PROJ_COLS), const),
                  pl.BlockSpec((MLA_Q_LORA, 2 * MLA_HEADS * HEAD_PAD), const),
                  pl.BlockSpec((MLA_KV_LORA, MLA_HEADS * (HEAD_PAD + MLA_D_V)), const),
                  pl.BlockSpec((1, MLA_Q_LORA), const),
                  pl.BlockSpec((1, MLA_KV_LORA), const),
                  pl.BlockSpec((tm, LANES), tab),
                  pl.BlockSpec((tm, LANES), tab)],
        out_specs=[pl.BlockSpec((tm, RET_WIDTH), row)] * 4 + [
            pl.BlockSpec((1, 1, MLA_HEADS * HEAD_PAD, tm), lambda i: (i // nseq, i % nseq, 0, 0)),
            pl.BlockSpec((tm, MLA_HEADS * HEAD_PAD), row),
            pl.BlockSpec((1, 1, MLA_HEADS * VT_ROWS, tm), lambda i: (i // nseq, i % nseq, 0, 0))],
        out_shape=outs,
        compiler_params=_cparams(("parallel",)),
        name="inproj",
    )(x2d, w_all, wq2, wkv, qnw, kvnw, cm, sm)


def _log_sigmoid(x):
    return jnp.minimum(x, 0.0) - jnp.log(1.0 + jnp.exp(-jnp.abs(x)))


def _ret_kernel(q_ref, k_ref, v_ref, g_ref, cos_ref, sin_ref, df_ref, db_ref, gnw_ref,
                o_ref, sf_ref, sb_ref, sbst_ref, kvf_ref, kb_ref, vb_ref, *, nsteps):
    phase = pl.program_id(2)
    step = pl.program_id(3)
    c = RET_CHUNK
    lgf = _log_sigmoid(df_ref[0])
    lgb = _log_sigmoid(db_ref[0])
    rpos = lax.broadcasted_iota(jnp.int32, (c, c), 0).astype(F32)
    cpos = lax.broadcasted_iota(jnp.int32, (c, c), 1).astype(F32)
    tn = (((0,), (0,)), ((), ()))

    def rope(x, cs, sn):
        return x * cs + pltpu.roll(x, RET_DK // 2, axis=1) * sn

    @pl.when(jnp.logical_and(phase == 0, step == 0))
    def _():
        sb_ref[...] = jnp.zeros_like(sb_ref)

    @pl.when(jnp.logical_and(phase == 1, step == 0))
    def _():
        sf_ref[...] = jnp.zeros_like(sf_ref)

    @pl.when(phase == 0)
    def _():
        kwb = jnp.exp(lgb * rpos)
        kwf = jnp.exp(lgf * (float(c - 1) - rpos))
        cdb = jnp.exp(lgb * float(c))
        kvstep = nsteps - 1 - step
        kvs = {}
        for g in range(RET_STEP_CHUNKS):
            rows = slice(g * c, (g + 1) * c)
            idx = kvstep * RET_STEP_CHUNKS + g
            kr = rope(k_ref[rows, :], cos_ref[rows, :], sin_ref[rows, :]) * (RET_DK ** -0.5)
            vb = v_ref[rows, :].astype(BF16)
            kb_ref[idx] = kr.astype(BF16)
            vb_ref[idx] = vb
            kw = jnp.concatenate([(kr * kwb).astype(BF16), (kr * kwf).astype(BF16)], axis=1)
            kvs[g] = lax.dot_general(kw, vb, tn, preferred_element_type=F32)
        sb = sb_ref[...]
        for g in range(RET_STEP_CHUNKS - 1, -1, -1):
            idx = kvstep * RET_STEP_CHUNKS + g
            sbst_ref[idx] = sb.astype(BF16)
            kvf_ref[idx] = kvs[g][c:2 * c]
            sb = cdb * sb + kvs[g][0:c]
        sb_ref[...] = sb

    @pl.when(phase == 1)
    def _():
        diff = rpos - cpos
        dmat = jnp.where(diff >= 0, jnp.exp(lgf * jnp.maximum(diff, 0.0)),
                         jnp.exp(lgb * jnp.maximum(-diff, 0.0)))
        qwf = jnp.exp(lgf * (rpos + 1.0))
        qwb = jnp.exp(lgb * (float(c) - rpos))
        cdf = jnp.exp(lgf * float(c))
        gnw = gnw_ref[...]
        qrs, scs, outs = {}, {}, {}
        sfs = {0: sf_ref[...]}

        def scores(g):
            rows = slice(g * c, (g + 1) * c)
            idx = step * RET_STEP_CHUNKS + g
            qrs[g] = rope(q_ref[rows, :], cos_ref[rows, :], sin_ref[rows, :])
            scs[g] = lax.dot_general(qrs[g].astype(BF16), kb_ref[idx], (((1,), (1,)), ((), ())),
                                     preferred_element_type=F32)
            sfs[g + 1] = cdf * sfs[g] + kvf_ref[idx]

        def mix(g):
            idx = step * RET_STEP_CHUNKS + g
            qr = qrs.pop(g)
            lhs = jnp.concatenate([(scs.pop(g) * dmat).astype(BF16), (qr * qwf).astype(BF16),
                                   (qr * qwb).astype(BF16)], axis=1)
            rhs = jnp.concatenate([vb_ref[idx], sfs[g].astype(BF16), sbst_ref[idx]], axis=0)
            outs[g] = jnp.dot(lhs, rhs, preferred_element_type=F32)

        def finish(g):
            rows = slice(g * c, (g + 1) * c)
            o = outs.pop(g)
            mu = jnp.mean(o, axis=-1, keepdims=True)
            oc = o - mu
            var = jnp.mean(oc * oc, axis=-1, keepdims=True)
            on = oc * lax.rsqrt(var + LN_EPS) * gnw
            gt = g_ref[rows, :]
            o_ref[rows, :] = (gt / (1.0 + jnp.exp(-gt)) * on).astype(o_ref.dtype)

        for s in range(RET_STEP_CHUNKS + 4):
            if s - 4 >= 0:
                finish(s - 4)
            if 0 <= s - 2 < RET_STEP_CHUNKS:
                mix(s - 2)
            if s < RET_STEP_CHUNKS:
                scores(s)
        sf_ref[...] = sfs[RET_STEP_CHUNKS]


def _retention(rq, rk, rv, rg, cosr, sinr, dfw, dbw, gnw, batch, seq):
    t = rq.shape[0]
    ts = RET_STEP_CHUNKS * RET_CHUNK
    nsteps = seq // ts
    nchunks = seq // RET_CHUNK
    kvi = lambda p, i: jnp.where(p == 0, nsteps - 1 - i, 0)
    qi = lambda p, i: jnp.where(p == 0, 0, i)
    kv_map = lambda b, h, p, i: (b * nsteps + kvi(p, i), h)
    q_map = lambda b, h, p, i: (b * nsteps + qi(p, i), h)
    tab_map = lambda b, h, p, i: (jnp.where(p == 0, nsteps - 1 - i, i), 0)
    head3 = lambda b, h, p, i: (h, 0, 0)
    return pl.pallas_call(
        functools.partial(_ret_kernel, nsteps=nsteps),
        grid=(batch, RET_HEADS, 2, nsteps),
        in_specs=[pl.BlockSpec((ts, RET_DK), q_map),
                  pl.BlockSpec((ts, RET_DK), kv_map),
                  pl.BlockSpec((ts, RET_DK), kv_map),
                  pl.BlockSpec((ts, RET_DK), q_map),
                  pl.BlockSpec((ts, RET_DK), tab_map),
                  pl.BlockSpec((ts, RET_DK), tab_map),
                  pl.BlockSpec((1, 1, LANES), head3),
                  pl.BlockSpec((1, 1, LANES), head3),
                  pl.BlockSpec((1, RET_DK), lambda b, h, p, i: (0, h))],
        out_specs=pl.BlockSpec((ts, RET_DK), q_map),
        out_shape=jax.ShapeDtypeStruct((t, RET_WIDTH), BF16),
        scratch_shapes=[pltpu.VMEM((RET_DK, RET_DK), F32),
                        pltpu.VMEM((RET_DK, RET_DK), F32),
                        pltpu.VMEM((nchunks, RET_DK, RET_DK), BF16),
                        pltpu.VMEM((nchunks, RET_DK, RET_DK), F32),
                        pltpu.VMEM((nchunks, RET_CHUNK, RET_DK), BF16),
                        pltpu.VMEM((nchunks, RET_CHUNK, RET_DK), BF16)],
        compiler_params=_cparams(("parallel", "parallel", "arbitrary", "arbitrary")),
        name="retention",
    )(rq, rk, rv, rg, cosr, sinr, dfw, dbw, gnw)


def _attn_kernel(qt_ref, k_ref, vt_ref, o_ref, ste_ref, sto_ref, acc_ref, *, seq):
    nkv = seq // TK_ATT
    npair = (seq // TQ_ATT) * nkv
    nt = (((1,), (1,)), ((), ()))
    nsub = TK_ATT // KSUB_ATT
    sub_per_chunk = TM_PROJ // KSUB_ATT
    neg_inf = jnp.full((1, TQ_ATT), -jnp.inf, F32)

    def score_steps(t, st_ref, tile_max):
        t = jnp.minimum(t, npair - 1)
        qi = t // nkv
        ks = pl.multiple_of((t % nkv) * TK_ATT, TK_ATT)

        def step(h, s):
            def run():
                hs = slice(h * HEAD_PAD, (h + 1) * HEAD_PAD)
                sub = slice(s * KSUB_ATT, (s + 1) * KSUB_ATT)
                st_ref[h, sub, :] = jnp.dot(k_ref[pl.ds(ks + s * KSUB_ATT, KSUB_ATT), hs], qt_ref[0, qi, hs, :],
                                            preferred_element_type=F32)
                top = jnp.max(st_ref[h, sub, :], axis=0, keepdims=True)
                tile_max[h] = top if s == 0 else jnp.maximum(tile_max[h], top)
            return run
        return [step(h, s) for h in range(2) for s in range(nsub)]

    def softmax_steps(t, st_ref, run_max, tile_max, out):
        kj = t % nkv
        steps = []
        for h in range(2):
            state = {}

            def start(h=h, state=state):
                m = jnp.where(kj == 0, neg_inf, run_max[h])
                state["mn"] = jnp.maximum(m, tile_max[h])
                state["alpha"] = jnp.exp2(m - state["mn"])

            def part(s, h=h, state=state):
                def run():
                    x = st_ref[h, s * KSUB_ATT:(s + 1) * KSUB_ATT, :] - state["mn"]
                    pb = jnp.exp2(x.astype(BF16))
                    c, r = divmod(s, sub_per_chunk)
                    pv = jnp.dot(vt_ref[0, kj * VT_CHUNKS + c, h * VT_ROWS:(h + 1) * VT_ROWS,
                                        r * KSUB_ATT:(r + 1) * KSUB_ATT],
                                 pb, preferred_element_type=F32)
                    state["pv"] = pv if s == 0 else state["pv"] + pv
                    if s == nsub - 1:
                        old = jnp.where(kj == 0, 0.0, acc_ref[h])
                        acc_ref[h] = state["alpha"] * old + state["pv"]
                        out[h] = state["mn"]
                return run
            steps += [start] + [part(s) for s in range(nsub)]
        return steps

    def interleave(mxu_steps, vpu_steps):
        n = max(len(mxu_steps), len(vpu_steps))
        for i in range(n):
            if i < len(mxu_steps):
                mxu_steps[i]()
            if i < len(vpu_steps):
                vpu_steps[i]()

    def body(jj, carry):
        run_max, max_even = carry
        t0 = 2 * jj
        mid, out, max_odd, max_next = {}, {}, {}, {}
        interleave(score_steps(t0 + 1, sto_ref, max_odd), softmax_steps(t0, ste_ref, run_max, max_even, mid))
        interleave(score_steps(t0 + 2, ste_ref, max_next),
                   softmax_steps(t0 + 1, sto_ref, (mid[0], mid[1]), max_odd, out))

        @pl.when((t0 + 1) % nkv == nkv - 1)
        def _():
            qi = t0 // nkv
            for h in range(2):
                acc = acc_ref[h]
                o_ref[0, qi, h * MLA_D_V:(h + 1) * MLA_D_V, :] = acc[0:MLA_D_V] / acc[MLA_D_V:MLA_D_V + 1]
        return (out[0], out[1]), (max_next[0], max_next[1])

    acc_ref[...] = jnp.zeros_like(acc_ref)
    max_first = {}
    for run in score_steps(0, ste_ref, max_first):
        run()
    lax.fori_loop(0, npair // 2, body, ((neg_inf, neg_inf), (max_first[0], max_first[1])))


def _attention(qt, k, vt, batch, seq):
    nq = seq // TQ_ATT
    nvt = seq // TM_PROJ
    assert (seq // TK_ATT) % 2 == 0, seq
    return pl.pallas_call(
        functools.partial(_attn_kernel, seq=seq),
        grid=(batch, MLA_HEADS // 2),
        in_specs=[pl.BlockSpec((1, nq, 2 * HEAD_PAD, TQ_ATT), lambda b, hp: (b, 0, hp, 0)),
                  pl.BlockSpec((seq, 2 * HEAD_PAD), lambda b, hp: (b, hp)),
                  pl.BlockSpec((1, nvt, 2 * VT_ROWS, TM_PROJ), lambda b, hp: (b, 0, hp, 0))],
        out_specs=pl.BlockSpec((1, nq, 2 * MLA_D_V, TQ_ATT), lambda b, hp: (b, 0, hp, 0)),
        out_shape=jax.ShapeDtypeStruct((batch, nq, MLA_HEADS * MLA_D_V, TQ_ATT), F32),
        scratch_shapes=[pltpu.VMEM((2, TK_ATT, TQ_ATT), F32),
                        pltpu.VMEM((2, TK_ATT, TQ_ATT), F32),
                        pltpu.VMEM((2, VT_ROWS, TQ_ATT), F32)],
        compiler_params=_cparams(("parallel", "parallel")),
        name="attention",
    )(qt, k, vt)


def _layer_norm(z, w, b):
    mu = jnp.mean(z, axis=-1, keepdims=True)
    zc = z - mu
    var = jnp.mean(zc * zc, axis=-1, keepdims=True)
    return zc * lax.rsqrt(var + LN_EPS) * w + b


def _split_bf16(a):
    hi = a.astype(BF16)
    lo = (a - hi.astype(F32)).astype(BF16)
    return hi, lo


def _outproj_kernel(x_ref, ret_ref, att_ref, wo1_ref, wo2_ref, onw_ref, lw_ref, lb_ref, wr_ref,
                    y_ref, lg_ref):
    wh, wl = _split_bf16(wr_ref[...])
    nt = (((1,), (1,)), ((), ()))
    nsub = x_ref.shape[0] // LANES
    pre = {}

    def project(r):
        rows = slice(r * LANES, (r + 1) * LANES)
        att = att_ref[0, 0, :, rows].T
        an = att * lax.rsqrt(jnp.mean(att * att, axis=-1, keepdims=True) + RMS_EPS) * onw_ref[...]
        mix = jnp.dot(ret_ref[rows, :], wo1_ref[...], preferred_element_type=F32)
        mix += jnp.dot(an.astype(BF16), wo2_ref[...], preferred_element_type=F32)
        pre[r] = DN_ALPHA * x_ref[rows, :] + mix

    def normalize(r):
        rows = slice(r * LANES, (r + 1) * LANES)
        y = _layer_norm(pre.pop(r), lw_ref[...], lb_ref[...])
        y_ref[rows, :] = y
        yh, yl = _split_bf16(y)
        lt = lax.dot_general(wh, yh, nt, preferred_element_type=F32)
        lt += lax.dot_general(wh, yl, nt, preferred_element_type=F32)
        lt += lax.dot_general(wl, yh, nt, preferred_element_type=F32)
        lg_ref[r] = lt

    for r in range(nsub + 1):
        if r < nsub:
            project(r)
        if r >= 1:
            normalize(r - 1)


def _outproj(x2d, ret, att_t, wo1, wo2, onw, lw, lb, wrt):
    t = x2d.shape[0]
    tm = TM_PROJ
    nseq = att_t.shape[1]
    assert att_t.shape[3] == tm
    row = lambda i: (i, 0)
    const = lambda i: (0, 0)
    return pl.pallas_call(
        _outproj_kernel,
        grid=(t // tm,),
        in_specs=[pl.BlockSpec((tm, D_MODEL), row),
                  pl.BlockSpec((tm, RET_WIDTH), row),
                  pl.BlockSpec((1, 1, RET_WIDTH, tm), lambda i: (i // nseq, i % nseq, 0, 0)),
                  pl.BlockSpec((RET_WIDTH, D_MODEL), const),
                  pl.BlockSpec((RET_WIDTH, D_MODEL), const),
                  pl.BlockSpec((1, RET_WIDTH), const),
                  pl.BlockSpec((1, D_MODEL), const),
                  pl.BlockSpec((1, D_MODEL), const),
                  pl.BlockSpec((N_EXPERTS, D_MODEL), const)],
        out_specs=[pl.BlockSpec((tm, D_MODEL), row),
                   pl.BlockSpec((tm // LANES, N_EXPERTS, LANES), lambda i: (i, 0, 0))],
        out_shape=[jax.ShapeDtypeStruct((t, D_MODEL), F32),
                   jax.ShapeDtypeStruct((t // LANES, N_EXPERTS, LANES), F32)],
        compiler_params=_cparams(("parallel",)),
        name="outproj",
    )(x2d, ret, att_t, wo1, wo2, onw, lw, lb, wrt)


def _route_kernel(lg_ref, rank_ref, gate_ref, cnt_ref, off_ref, blk_ref, boff_ref, *, cap, nb):
    e = N_EXPERTS
    ntiles = nb // BLOCKS_PER_TILE
    lg = lg_ref[...]
    ex = jnp.exp(lg - jnp.max(lg, axis=1, keepdims=True))
    aff = ex / jnp.sum(ex, axis=1, keepdims=True)
    gate_ref[...] = aff

    def count(mask):
        part = jnp.sum(mask.astype(F32), axis=0)
        return jnp.broadcast_to(jnp.sum(part, axis=1, keepdims=True), (e, LANES))

    def bisect(_, lohi):
        lo, hi = lohi
        mid = 0.5 * (lo + hi)
        ok = count(aff >= mid[None]) >= float(cap)
        return jnp.where(ok, mid, lo), jnp.where(ok, hi, mid)

    lo0 = jnp.zeros((e, LANES), F32)
    hi0 = jnp.full((e, LANES), 2.0, F32)
    lo, hi = lax.fori_loop(0, THRESHOLD_STEPS, bisect, (lo0, hi0))
    gt = aff >= hi[None]
    eq = jnp.logical_and(aff >= lo[None], jnp.logical_not(gt))
    need = float(cap) - count(gt)

    ii = lax.broadcasted_iota(jnp.int32, (LANES, LANES), 0)
    jj = lax.broadcasted_iota(jnp.int32, (LANES, LANES), 1)
    upper = (ii <= jj).astype(BF16)
    ones = jnp.ones((LANES, LANES), BF16)

    def block_scan(mask):
        m2 = mask.astype(BF16).reshape(nb * e, LANES)
        incl = jnp.dot(m2, upper, preferred_element_type=F32).reshape(nb, e, LANES)
        tot = jnp.dot(m2, ones, preferred_element_type=F32).reshape(nb, e, LANES)
        return incl, tot

    def leading_excl_scan(src_ref, dst_ref, n):
        def step(j, carry):
            dst_ref[j] = carry
            return carry + src_ref[j]
        return lax.fori_loop(0, n, step, jnp.zeros((e, LANES), F32))

    eqf = eq.astype(F32)
    incl, tot = block_scan(eq)
    blk_ref[...] = tot
    leading_excl_scan(blk_ref, boff_ref, nb)
    eq_before = incl - eqf + boff_ref[...]
    sel = jnp.logical_or(gt, jnp.logical_and(eq, eq_before < need[None]))

    self32 = sel.astype(F32)
    incl, tot = block_scan(sel)
    local = (incl - self32).reshape(ntiles, BLOCKS_PER_TILE, e, LANES)
    tot4 = tot.reshape(ntiles, BLOCKS_PER_TILE, e, LANES)
    ranks = [local[:, 0]]
    run = tot4[:, 0]
    for b in range(1, BLOCKS_PER_TILE):
        ranks.append(local[:, b] + run)
        run = run + tot4[:, b]
    rank = jnp.stack(ranks, axis=1).reshape(nb, e, LANES)
    rank_ref[...] = jnp.where(sel, rank, NOT_ROUTED)
    cnt_ref[...] = run
    blk_ref[0:ntiles] = run
    leading_excl_scan(blk_ref, boff_ref, ntiles)
    off_ref[...] = boff_ref[0:ntiles]


def _route(logits, cap):
    nb = logits.shape[0]
    ntiles = nb // BLOCKS_PER_TILE
    big = jax.ShapeDtypeStruct((nb, N_EXPERTS, LANES), F32)
    small = jax.ShapeDtypeStruct((ntiles, N_EXPERTS, LANES), F32)
    return pl.pallas_call(
        functools.partial(_route_kernel, cap=cap, nb=nb),
        out_shape=[big, big, small, small],
        scratch_shapes=[pltpu.VMEM((nb, N_EXPERTS, LANES), F32),
                        pltpu.VMEM((nb, N_EXPERTS, LANES), F32)],
        compiler_params=pltpu.CompilerParams(vmem_limit_bytes=VMEM_LIMIT),
        name="route",
    )(logits)


def _lead(off_ref, tile, ex):
    pos = off_ref[tile * N_EXPERTS + ex]
    lead = lax.rem(pos, ROW_ALIGN)
    return pos - lead, lead


def _overflow_chunks(lead, cnt):
    return lax.div(jnp.maximum(lead + cnt - SLAB_ROWS, 0) + (CHUNK_ROWS - 1), CHUNK_ROWS)


def _compact_kernel(off_ref, cnt_ref, y_ref, rank_ref, xg_ref, p_ref, pc_ref, st_ref, ost_ref,
                    carry_ref, zero_ref, sem, osem, *, cap, cap_rows):
    i = pl.program_id(0)
    nt = pl.num_programs(0)
    slot = i % 2

    def slab(tile, slt, ex, fn):
        start, lead = _lead(off_ref, tile, ex)
        used = lead + cnt_ref[tile * N_EXPERTS + ex]
        for first, rows in SLAB_PIECES:
            def piece(first=first, rows=rows):
                fn(pltpu.make_async_copy(
                    st_ref.at[slt, pl.ds(ex * SLAB_ROWS + first, rows)],
                    xg_ref.at[pl.ds(pl.multiple_of(ex * cap_rows + start + first, ROW_ALIGN), rows)],
                    sem.at[slt]))
            if first == 0:
                piece()
            else:
                pl.when(used > first)(piece)

    @pl.when(i == 0)
    def _():
        carry_ref[...] = jnp.zeros_like(carry_ref)
        zero_ref[...] = jnp.zeros_like(zero_ref)
        tails = [pltpu.make_async_copy(zero_ref, xg_ref.at[pl.ds(ex * cap_rows + cap, cap_rows - cap)],
                                       osem.at[0]) for ex in range(N_EXPERTS)]
        for cp in tails:
            cp.start()
        for cp in tails:
            cp.wait()

    xb = y_ref[...].astype(BF16)
    rank = jnp.concatenate([rank_ref[b] for b in range(BLOCKS_PER_TILE)], axis=1)
    jslab = lax.broadcasted_iota(jnp.int32, (SLAB_ROWS, TM_MOE), 0).astype(F32)
    jgrp = lax.broadcasted_iota(jnp.int32, (ROW_ALIGN, TM_MOE), 0).astype(F32)
    jchunk = lax.broadcasted_iota(jnp.int32, (CHUNK_ROWS, TM_MOE), 0).astype(F32)
    shifted, last_group = [], []
    for ex in range(N_EXPERTS):
        _, lead = _lead(off_ref, i, ex)
        sh = rank[ex:ex + 1, :] + lead.astype(F32)
        grp = lax.div(lead + cnt_ref[i * N_EXPERTS + ex], ROW_ALIGN)
        p_ref[ex * SLAB_ROWS:(ex + 1) * SLAB_ROWS, :] = (sh == jslab).astype(BF16)
        pc_ref[ex * ROW_ALIGN:(ex + 1) * ROW_ALIGN, :] = (
            sh - (grp * ROW_ALIGN).astype(F32) == jgrp).astype(BF16)
        shifted.append(sh)
        last_group.append(grp)
    rows = jnp.dot(p_ref[...], xb, preferred_element_type=F32)
    tail = jnp.dot(pc_ref[...], xb, preferred_element_type=F32)
    for ex in range(N_EXPERTS):
        lo = ex * SLAB_ROWS
        old = carry_ref[ex]
        st_ref[slot, lo:lo + ROW_ALIGN, :] = (rows[lo:lo + ROW_ALIGN] + old).astype(BF16)
        st_ref[slot, lo + ROW_ALIGN:lo + SLAB_ROWS, :] = rows[lo + ROW_ALIGN:lo + SLAB_ROWS].astype(BF16)
        carry_ref[ex] = tail[ex * ROW_ALIGN:(ex + 1) * ROW_ALIGN] + jnp.where(last_group[ex] == 0, old, 0.0)

    @pl.when(i >= 1)
    def _():
        for ex in range(N_EXPERTS):
            slab(i - 1, 1 - slot, ex, lambda cp: cp.wait())

    for ex in range(N_EXPERTS):
        slab(i, slot, ex, lambda cp: cp.start())

    for ex in range(N_EXPERTS):
        start, lead = _lead(off_ref, i, ex)

        def chunk(c, carry, ex=ex, start=start):
            first = SLAB_ROWS + c * CHUNK_ROWS
            pc = (shifted[ex] == jchunk + first.astype(F32)).astype(BF16)
            ost_ref[...] = jnp.dot(pc, xb, preferred_element_type=F32).astype(BF16)
            cp = pltpu.make_async_copy(
                ost_ref,
                xg_ref.at[pl.ds(pl.multiple_of(ex * cap_rows + start + first, ROW_ALIGN), CHUNK_ROWS)],
                osem.at[0])
            cp.start()
            cp.wait()
            return carry

        lax.fori_loop(0, _overflow_chunks(lead, cnt_ref[i * N_EXPERTS + ex]), chunk, 0)

    @pl.when(i == nt - 1)
    def _():
        for ex in range(N_EXPERTS):
            slab(i, slot, ex, lambda cp: cp.wait())


def _compact(off, cnt, y1, rank_em, cap, cap_rows):
    n = y1.shape[0]
    ntiles = n // TM_MOE
    gs = pltpu.PrefetchScalarGridSpec(
        num_scalar_prefetch=2,
        grid=(ntiles,),
        in_specs=[pl.BlockSpec((TM_MOE, D_MODEL), lambda i, o, c: (i, 0)),
                  pl.BlockSpec((BLOCKS_PER_TILE, N_EXPERTS, LANES), lambda i, o, c: (i, 0, 0))],
        out_specs=pl.BlockSpec(memory_space=pl.ANY),
        scratch_shapes=[pltpu.VMEM((N_EXPERTS * SLAB_ROWS, TM_MOE), BF16),
                        pltpu.VMEM((N_EXPERTS * ROW_ALIGN, TM_MOE), BF16),
                        pltpu.VMEM((2, N_EXPERTS * SLAB_ROWS, D_MODEL), BF16),
                        pltpu.VMEM((CHUNK_ROWS, D_MODEL), BF16),
                        pltpu.VMEM((N_EXPERTS, ROW_ALIGN, D_MODEL), F32),
                        pltpu.VMEM((cap_rows - cap, D_MODEL), BF16),
                        pltpu.SemaphoreType.DMA((2,)),
                        pltpu.SemaphoreType.DMA((1,))])
    return pl.pallas_call(
        functools.partial(_compact_kernel, cap=cap, cap_rows=cap_rows),
        grid_spec=gs,
        out_shape=jax.ShapeDtypeStruct((N_EXPERTS * cap_rows, D_MODEL), BF16),
        compiler_params=_cparams(("arbitrary",)),
        name="compact",
    )(off, cnt, y1, rank_em)


def _ffn_kernel(x_ref, wg_ref, wu_ref, wd_ref, y_ref, *, nvalid):
    j = pl.program_id(1)

    @pl.when(j < nvalid)
    def _():
        x = x_ref[...]
        hg = jnp.dot(x, wg_ref[0], preferred_element_type=F32)
        hu = jnp.dot(x, wu_ref[0], preferred_element_type=F32)
        h = (hg / (1.0 + jnp.exp(-hg)) * hu).astype(BF16)
        y_ref[...] = jnp.dot(h, wd_ref[0], preferred_element_type=F32).astype(BF16)

    @pl.when(j >= nvalid)
    def _():
        y_ref[...] = jnp.zeros_like(y_ref)


def _ffn(xg, wg, wu, wd, cap, cap_rows):
    nvalid = cap // TM_FFN
    nt = cap_rows // TM_FFN
    wmap = lambda ex, j: (ex, 0, 0)
    return pl.pallas_call(
        functools.partial(_ffn_kernel, nvalid=nvalid),
        grid=(N_EXPERTS, nt),
        in_specs=[pl.BlockSpec((TM_FFN, D_MODEL), lambda ex, j: (ex * nt + jnp.minimum(j, nvalid - 1), 0)),
                  pl.BlockSpec((1, D_MODEL, D_MODEL), wmap),
                  pl.BlockSpec((1, D_MODEL, D_MODEL), wmap),
                  pl.BlockSpec((1, D_MODEL, D_MODEL), wmap)],
        out_specs=pl.BlockSpec((TM_FFN, D_MODEL), lambda ex, j: (ex * nt + j, 0)),
        out_shape=jax.ShapeDtypeStruct((N_EXPERTS * cap_rows, D_MODEL), BF16),
        compiler_params=_cparams(("arbitrary", "arbitrary")),
        name="ffn",
    )(xg, wg, wu, wd)


def _combine_kernel(off_ref, cnt_ref, y1_ref, rank_ref, gate_ref, yh_ref, lw_ref, lb_ref,
                    o_ref, ybuf, obuf, acc_ref, sem, osem, *, cap_rows):
    i = pl.program_id(0)
    nt = pl.num_programs(0)
    slot = i % 2

    ncol = N_EXPERTS * SLAB_ROWS

    def fetch(tile, slt, fn):
        for ex in range(N_EXPERTS):
            start, lead = _lead(off_ref, tile, ex)
            used = lead + cnt_ref[tile * N_EXPERTS + ex]
            for first, rows in SLAB_PIECES:
                def piece(ex=ex, start=start, first=first, rows=rows):
                    fn(pltpu.make_async_copy(
                        yh_ref.at[pl.ds(pl.multiple_of(ex * cap_rows + start + first, ROW_ALIGN), rows)],
                        ybuf.at[slt, pl.ds(ex * SLAB_ROWS + first, rows)],
                        sem.at[slt]))
                if first == 0:
                    piece()
                else:
                    pl.when(used > first)(piece)

    @pl.when(i == 0)
    def _():
        ybuf[...] = jnp.zeros_like(ybuf)
        fetch(0, 0, lambda cp: cp.start())

    @pl.when(i + 1 < nt)
    def _():
        fetch(i + 1, 1 - slot, lambda cp: cp.start())

    rank = rank_ref[...]
    gate = gate_ref[...]
    er = lax.broadcasted_iota(jnp.int32, (N_EXPERTS, ncol), 0)
    ec = lax.broadcasted_iota(jnp.int32, (N_EXPERTS, ncol), 1)
    expand = jnp.logical_and(ec >= er * SLAB_ROWS, ec < (er + 1) * SLAB_ROWS).astype(BF16)
    rexp = jnp.dot(rank.astype(BF16), expand, preferred_element_type=F32)
    gexp = jnp.dot(gate.astype(BF16), expand, preferred_element_type=F32)
    col = lax.broadcasted_iota(jnp.int32, (1, ncol), 1)
    want = jnp.zeros((1, ncol), F32)
    for ex in range(N_EXPERTS):
        _, lead = _lead(off_ref, i, ex)
        inside = jnp.logical_and(col >= ex * SLAB_ROWS, col < (ex + 1) * SLAB_ROWS)
        want = jnp.where(inside, (col - ex * SLAB_ROWS - lead).astype(F32), want)
    gmat = jnp.where(rexp == want, gexp, 0.0).astype(BF16)

    fetch(i, slot, lambda cp: cp.wait())
    acc_ref[...] = jnp.dot(gmat, ybuf[slot], preferred_element_type=F32)

    jc = lax.broadcasted_iota(jnp.int32, (TM_MOE, CHUNK_ROWS), 1).astype(F32)
    for ex in range(N_EXPERTS):
        start, lead = _lead(off_ref, i, ex)

        def chunk(c, carry, ex=ex, start=start, lead=lead):
            first = SLAB_ROWS + c * CHUNK_ROWS
            cp = pltpu.make_async_copy(
                yh_ref.at[pl.ds(pl.multiple_of(ex * cap_rows + start + first, ROW_ALIGN), CHUNK_ROWS)],
                obuf, osem.at[0])
            cp.start()
            cp.wait()
            gc = jnp.where(rank[:, ex:ex + 1] + lead.astype(F32) == jc + first.astype(F32),
                           gate[:, ex:ex + 1].astype(BF16).astype(F32), 0.0).astype(BF16)
            acc_ref[...] += jnp.dot(gc, obuf[...], preferred_element_type=F32)
            return carry

        lax.fori_loop(0, _overflow_chunks(lead, cnt_ref[i * N_EXPERTS + ex]), chunk, 0)

    o_ref[...] = _layer_norm(DN_ALPHA * y1_ref[...] + acc_ref[...], lw_ref[...], lb_ref[...])


def _combine(off, cnt, y1, rank_tm, gate_tm, yh, lw, lb, cap_rows):
    n = y1.shape[0]
    ntiles = n // TM_MOE
    row = lambda i, o, c: (i, 0)
    const = lambda i, o, c: (0, 0)
    gs = pltpu.PrefetchScalarGridSpec(
        num_scalar_prefetch=2,
        grid=(ntiles,),
        in_specs=[pl.BlockSpec((TM_MOE, D_MODEL), row),
                  pl.BlockSpec((TM_MOE, N_EXPERTS), row),
                  pl.BlockSpec((TM_MOE, N_EXPERTS), row),
                  pl.BlockSpec(memory_space=pl.ANY),
                  pl.BlockSpec((1, D_MODEL), const),
                  pl.BlockSpec((1, D_MODEL), const)],
        out_specs=pl.BlockSpec((TM_MOE, D_MODEL), row),
        scratch_shapes=[pltpu.VMEM((2, N_EXPERTS * SLAB_ROWS, D_MODEL), BF16),
                        pltpu.VMEM((CHUNK_ROWS, D_MODEL), BF16),
                        pltpu.VMEM((TM_MOE, D_MODEL), F32),
                        pltpu.SemaphoreType.DMA((2,)),
                        pltpu.SemaphoreType.DMA((1,))])
    return pl.pallas_call(
        functools.partial(_combine_kernel, cap_rows=cap_rows),
        grid_spec=gs,
        out_shape=jax.ShapeDtypeStruct((n, D_MODEL), F32),
        compiler_params=_cparams(("arbitrary",)),
        name="combine",
    )(off, cnt, y1, rank_tm, gate_tm, yh, lw, lb)


def _rope_tables(seq):
    pos = jnp.arange(seq, dtype=F32)[:, None]
    inv_m = 1.0 / (ROPE_BASE ** (jnp.arange(0, MLA_D_ROPE, 2, dtype=F32) / MLA_D_ROPE))
    am = pos * inv_m[None, :]
    one = jnp.ones((seq, MLA_D_NOPE), F32)
    zero_n = jnp.zeros((seq, MLA_D_NOPE), F32)
    zero_p = jnp.zeros((seq, HEAD_PAD - MLA_D_NOPE - MLA_D_ROPE), F32)
    cm = jnp.concatenate([one, jnp.cos(am), jnp.cos(am), zero_p], axis=1)
    sm = jnp.concatenate([zero_n, jnp.sin(am), jnp.sin(am), zero_p], axis=1)
    inv_r = 1.0 / (ROPE_BASE ** (jnp.arange(0, RET_DK, 2, dtype=F32) / RET_DK))
    ar = pos * inv_r[None, :]
    cosr = jnp.concatenate([jnp.cos(ar), jnp.cos(ar)], axis=1)
    sinr = jnp.concatenate([-jnp.sin(ar), jnp.sin(ar)], axis=1)
    return cm, sm, cosr, sinr


def _prep_weights(w_in, mla_w_uq, mla_w_ukv):
    half = MLA_D_ROPE // 2
    base = 4 * RET_WIDTH + MLA_Q_LORA + MLA_KV_LORA
    w_kr = w_in[:, base:base + MLA_D_ROPE]
    zn = jnp.zeros((D_MODEL, MLA_D_NOPE), F32)
    zp = jnp.zeros((D_MODEL, HEAD_PAD - MLA_D_NOPE - MLA_D_ROPE), F32)
    kr_p = jnp.concatenate([zn, w_kr, zp], axis=1)
    kr_rot = jnp.concatenate([zn, -w_kr[:, half:], w_kr[:, :half], zp], axis=1)
    w_all = jnp.concatenate([w_in[:, :base], kr_p, kr_rot], axis=1).astype(BF16)

    wq = mla_w_uq.reshape(MLA_Q_LORA, MLA_HEADS, MLA_D_NOPE + MLA_D_ROPE)
    nope, ropew = wq[..., :MLA_D_NOPE], wq[..., MLA_D_NOPE:]
    zq = jnp.zeros((MLA_Q_LORA, MLA_HEADS, HEAD_PAD - MLA_D_NOPE - MLA_D_ROPE), F32)
    q_p = jnp.concatenate([nope, ropew, zq], axis=-1).reshape(MLA_Q_LORA, -1)
    q_rot = jnp.concatenate([jnp.zeros_like(nope), -ropew[..., half:], ropew[..., :half], zq],
                            axis=-1).reshape(MLA_Q_LORA, -1)
    wq2 = jnp.concatenate([q_p, q_rot], axis=1).astype(BF16)

    wkv3 = mla_w_ukv.reshape(MLA_KV_LORA, MLA_HEADS, MLA_D_NOPE + MLA_D_V)
    k_p = jnp.concatenate([wkv3[..., :MLA_D_NOPE],
                           jnp.zeros((MLA_KV_LORA, MLA_HEADS, HEAD_PAD - MLA_D_NOPE), F32)],
                          axis=-1).reshape(MLA_KV_LORA, -1)
    v_w = wkv3[..., MLA_D_NOPE:].reshape(MLA_KV_LORA, -1)
    wkv = jnp.concatenate([k_p, v_w], axis=1).astype(BF16)
    return w_all, wq2, wkv


def _mixer_ln(x, wts):
    batch, seq, _ = x.shape
    x2d = x.reshape(batch * seq, D_MODEL)
    cm, sm, cosr, sinr = _rope_tables(seq)
    rq, rk, rv, rg, q, k, v = _inproj(x2d, seq, wts["w_all"], wts["wq2"], wts["wkv"],
                                      wts["qnw"], wts["kvnw"], cm, sm)
    ret = _retention(rq, rk, rv, rg, cosr, sinr, wts["dfw"], wts["dbw"], wts["gnw"], batch, seq)
    att = _attention(q, k, v, batch, seq)
    return _outproj(x2d, ret, att, wts["wo1"], wts["wo2"], wts["onw"], wts["ln1w"], wts["ln1b"],
                    wts["wrt"])


def _moe_ln(y1, logits, wts):
    n = y1.shape[0]
    cap = EC_CAPACITY_FACTOR * n // N_EXPERTS
    assert cap % TM_FFN == 0, (n, cap)
    cap_rows = cap + TM_FFN
    rank_em, gate_em, cnt_rep, off_rep = _route(logits, cap)
    cnt = cnt_rep[:, :, 0].astype(jnp.int32).reshape(-1)
    off = off_rep[:, :, 0].astype(jnp.int32).reshape(-1)
    to_tm = lambda a: jnp.transpose(a, (0, 2, 1)).reshape(n, N_EXPERTS)
    xg = _compact(off, cnt, y1, rank_em, cap, cap_rows)
    yh = _ffn(xg, wts["wg"], wts["wu"], wts["wd"], cap, cap_rows)
    return _combine(off, cnt, y1, to_tm(rank_em), to_tm(gate_em), yh, wts["ln2w"], wts["ln2b"],
                    cap_rows)


def _layer(x, wts):
    batch, seq, _ = x.shape
    y1, logits = _mixer_ln(x, wts)
    return _moe_ln(y1, logits, wts).reshape(batch, seq, D_MODEL)


def kernel(x_prompt, x_sample, w_in, ret_decay_fwd, ret_decay_bwd, ret_gn_w, mla_q_norm_w, mla_w_uq,
           mla_kv_norm_w, mla_w_ukv, mla_out_norm_w, w_out, ln1_w, ln1_b, w_router, w_gate, w_up,
           w_down, ln2_w, ln2_b):
    depth = w_in.shape[0]
    y_prompt, y_sample = x_prompt, x_sample
    for l in range(depth):
        w_all, wq2, wkv = _prep_weights(w_in[l], mla_w_uq[l], mla_w_ukv[l])
        rep = lambda a: jnp.broadcast_to(a.astype(F32)[:, None, None], (RET_HEADS, 1, LANES))
        wts = dict(
            w_all=w_all, wq2=wq2, wkv=wkv,
            qnw=mla_q_norm_w[l][None, :], kvnw=mla_kv_norm_w[l][None, :],
            dfw=rep(ret_decay_fwd[l]), dbw=rep(ret_decay_bwd[l]), gnw=ret_gn_w[l][None, :],
            wo1=w_out[l][:RET_WIDTH].astype(BF16), wo2=w_out[l][RET_WIDTH:].astype(BF16),
            onw=mla_out_norm_w[l][None, :], ln1w=ln1_w[l][None, :], ln1b=ln1_b[l][None, :],
            wrt=w_router[l].T,
            wg=w_gate[l].astype(BF16), wu=w_up[l].astype(BF16), wd=w_down[l].astype(BF16),
            ln2w=ln2_w[l][None, :], ln2b=ln2_b[l][None, :])
        y_prompt = _layer(y_prompt, wts)
        y_sample = _layer(y_sample, wts)
    return (y_prompt, y_sample)
```

```python
import functools
import math

import numpy as np
import jax
import jax.numpy as jnp
from jax import lax
from jax.experimental import pallas as pl
from jax.experimental.pallas import tpu as pltpu

F32 = jnp.float32
BF16 = jnp.bfloat16

D_MODEL = 1024
RET_WIDTH = 512
RET_HEADS = 4
RET_DK = 128
RET_CHUNK = 128
MLA_HEADS = 8
MLA_D_NOPE = 64
MLA_D_ROPE = 32
MLA_D_V = 64
MLA_Q_LORA = 256
MLA_KV_LORA = 128
N_EXPERTS = 16
EC_CAPACITY_FACTOR = 2
ROPE_BASE = 10000.0
LN_EPS = 1e-5
RMS_EPS = 1e-6
DN_ALPHA = 2.0 ** 0.25

LANES = 128
HEAD_PAD = 128
PROJ_COLS = 4 * RET_WIDTH + MLA_Q_LORA + MLA_KV_LORA + 2 * LANES
VMEM_LIMIT = 56 * 1024 * 1024

TM_PROJ = 512
RET_STEP_CHUNKS = 16
TQ_ATT = 512
VT_CHUNKS = 2
VT_ROWS = 80
TK_ATT = VT_CHUNKS * TM_PROJ
KSUB_ATT = 256
TM_MOE = 256
ROW_ALIGN = 16
CHUNK_ROWS = 64
SLAB_ROWS = ROW_ALIGN + CHUNK_ROWS
SLAB_PIECES = ((0, 48), (48, 32))
THRESHOLD_STEPS = 64
NOT_ROUTED = -512.0
TM_FFN = 512
BLOCKS_PER_TILE = TM_MOE // LANES

_QK_SCALE = (MLA_D_NOPE + MLA_D_ROPE) ** -0.5 * math.log2(math.e)


def _cparams(sem):
    return pltpu.CompilerParams(dimension_semantics=sem, vmem_limit_bytes=VMEM_LIMIT)


def _inproj_kernel(x_ref, w_ref, wq_ref, wkv_ref, qnw_ref, kvnw_ref, cm_ref, sm_ref,
                   rq_ref, rk_ref, rv_ref, rg_ref, qt_ref, k_ref, vt_ref):
    xb = x_ref[...].astype(BF16)
    proj = jnp.dot(xb, w_ref[...], preferred_element_type=F32)
    rq_ref[...] = proj[:, 0:512]
    rk_ref[...] = proj[:, 512:1024]
    rv_ref[...] = proj[:, 1024:1536]
    rg_ref[...] = proj[:, 1536:2048]
    cm = cm_ref[...]
    sm = sm_ref[...]

    cq = proj[:, 2048:2304]
    cqn = cq * lax.rsqrt(jnp.mean(cq * cq, axis=-1, keepdims=True) + RMS_EPS) * qnw_ref[...]
    qq = jnp.dot(cqn.astype(BF16), wq_ref[...], preferred_element_type=F32)

    ckv = proj[:, 2304:2432]
    ckvn = ckv * lax.rsqrt(jnp.mean(ckv * ckv, axis=-1, keepdims=True) + RMS_EPS) * kvnw_ref[...]
    kv = jnp.dot(ckvn.astype(BF16), wkv_ref[...], preferred_element_type=F32)

    krope = proj[:, 2432:2560] * cm + proj[:, 2560:2688] * sm
    for h in range(MLA_HEADS):
        lo = h * HEAD_PAD
        qh = qq[:, lo:lo + HEAD_PAD] * cm + qq[:, 1024 + lo:1024 + lo + HEAD_PAD] * sm
        qt_ref[0, 0, lo:lo + HEAD_PAD, :] = (qh * _QK_SCALE).T.astype(BF16)
        k_ref[:, lo:lo + HEAD_PAD] = (kv[:, lo:lo + HEAD_PAD] + krope).astype(BF16)
    vt = kv[:, 1024:1536].T.astype(BF16)
    pad_rows = VT_ROWS - MLA_D_V
    ones_row = (lax.broadcasted_iota(jnp.int32, (pad_rows, vt.shape[1]), 0) == 0).astype(BF16)
    for h in range(MLA_HEADS):
        vt_ref[0, 0, h * VT_ROWS:h * VT_ROWS + MLA_D_V, :] = vt[h * MLA_D_V:(h + 1) * MLA_D_V, :]
        vt_ref[0, 0, h * VT_ROWS + MLA_D_V:(h + 1) * VT_ROWS, :] = ones_row


def _inproj(x2d, seq, w_all, wq2, wkv, qnw, kvnw, cm, sm):
    t = x2d.shape[0]
    tm = TM_PROJ
    nseq = seq // tm
    row = lambda i: (i, 0)
    const = lambda i: (0, 0)
    tab = lambda i: (i % nseq, 0)
    outs = [jax.ShapeDtypeStruct((t, RET_WIDTH), F32)] * 4 + [
        jax.ShapeDtypeStruct((t // seq, nseq, MLA_HEADS * HEAD_PAD, tm), BF16),
        jax.ShapeDtypeStruct((t, MLA_HEADS * HEAD_PAD), BF16),
        jax.ShapeDtypeStruct((t // seq, nseq, MLA_HEADS * VT_ROWS, tm), BF16)]
    return pl.pallas_call(
        _inproj_kernel,
        grid=(t // tm,),
        in_specs=[pl.BlockSpec((tm, D_MODEL), row),
                  pl.BlockSpec((D_MODEL, PROJ_COLS), const),
                  pl.BlockSpec((MLA_Q_LORA, 2 * MLA_HEADS * HEAD_PAD), const),
                  pl.BlockSpec((MLA_KV_LORA, MLA_HEADS * (HEAD_PAD + MLA_D_V)), const),
                  pl.BlockSpec((1, MLA_Q_LORA), const),
                  pl.BlockSpec((1, MLA_KV_LORA), const),
                  pl.BlockSpec((tm, LANES), tab),
                  pl.BlockSpec((tm, LANES), tab)],
        out_specs=[pl.BlockSpec((tm, RET_WIDTH), row)] * 4 + [
            pl.BlockSpec((1, 1, MLA_HEADS * HEAD_PAD, tm), lambda i: (i // nseq, i % nseq, 0, 0)),
            pl.BlockSpec((tm, MLA_HEADS * HEAD_PAD), row),
            pl.BlockSpec((1, 1, MLA_HEADS * VT_ROWS, tm), lambda i: (i // nseq, i % nseq, 0, 0))],
        out_shape=outs,
        compiler_params=_cparams(("parallel",)),
        name="inproj",
    )(x2d, w_all, wq2, wkv, qnw, kvnw, cm, sm)


def _log_sigmoid(x):
    return jnp.minimum(x, 0.0) - jnp.log(1.0 + jnp.exp(-jnp.abs(x)))


def _ret_kernel(q_ref, k_ref, v_ref, g_ref, cos_ref, sin_ref, df_ref, db_ref, gnw_ref,
                o_ref, sf_ref, sb_ref, sbst_ref, kvf_ref, kb_ref, vb_ref, *, nsteps):
    phase = pl.program_id(2)
    step = pl.program_id(3)
    c = RET_CHUNK
    lgf = _log_sigmoid(df_ref[0])
    lgb = _log_sigmoid(db_ref[0])
    rpos = lax.broadcasted_iota(jnp.int32, (c, c), 0).astype(F32)
    cpos = lax.broadcasted_iota(jnp.int32, (c, c), 1).astype(F32)
    tn = (((0,), (0,)), ((), ()))

    def rope(x, cs, sn):
        return x * cs + pltpu.roll(x, RET_DK // 2, axis=1) * sn

    @pl.when(jnp.logical_and(phase == 0, step == 0))
    def _():
        sb_ref[...] = jnp.zeros_like(sb_ref)

    @pl.when(jnp.logical_and(phase == 1, step == 0))
    def _():
        sf_ref[...] = jnp.zeros_like(sf_ref)

    @pl.when(phase == 0)
    def _():
        kwb = jnp.exp(lgb * rpos)
        kwf = jnp.exp(lgf * (float(c - 1) - rpos))
        cdb = jnp.exp(lgb * float(c))
        kvstep = nsteps - 1 - step
        kvs = {}
        for g in range(RET_STEP_CHUNKS):
            rows = slice(g * c, (g + 1) * c)
            idx = kvstep * RET_STEP_CHUNKS + g
            kr = rope(k_ref[rows, :], cos_ref[rows, :], sin_ref[rows, :]) * (RET_DK ** -0.5)
            vb = v_ref[rows, :].astype(BF16)
            kb_ref[idx] = kr.astype(BF16)
            vb_ref[idx] = vb
            kw = jnp.concatenate([(kr * kwb).astype(BF16), (kr * kwf).astype(BF16)], axis=1)
            kvs[g] = lax.dot_general(kw, vb, tn, preferred_element_type=F32)
        sb = sb_ref[...]
        for g in range(RET_STEP_CHUNKS - 1, -1, -1):
            idx = kvstep * RET_STEP_CHUNKS + g
            sbst_ref[idx] = sb.astype(BF16)
            kvf_ref[idx] = kvs[g][c:2 * c]
            sb = cdb * sb + kvs[g][0:c]
        sb_ref[...] = sb

    @pl.when(phase == 1)
    def _():
        diff = rpos - cpos
        dmat = jnp.where(diff >= 0, jnp.exp(lgf * jnp.maximum(diff, 0.0)),
                         jnp.exp(lgb * jnp.maximum(-diff, 0.0)))
        qwf = jnp.exp(lgf * (rpos + 1.0))
        qwb = jnp.exp(lgb * (float(c) - rpos))
        cdf = jnp.exp(lgf * float(c))
        gnw = gnw_ref[...]
        qrs, scs, outs = {}, {}, {}
        sfs = {0: sf_ref[...]}

        def scores(g):
            rows = slice(g * c, (g + 1) * c)
            idx = step * RET_STEP_CHUNKS + g
            qrs[g] = rope(q_ref[rows, :], cos_ref[rows, :], sin_ref[rows, :])
            scs[g] = lax.dot_general(qrs[g].astype(BF16), kb_ref[idx], (((1,), (1,)), ((), ())),
                                     preferred_element_type=F32)
            sfs[g + 1] = cdf * sfs[g] + kvf_ref[idx]

        def mix(g):
            idx = step * RET_STEP_CHUNKS + g
            qr = qrs.pop(g)
            lhs = jnp.concatenate([(scs.pop(g) * dmat).astype(BF16), (qr * qwf).astype(BF16),
                                   (qr * qwb).astype(BF16)], axis=1)
            rhs = jnp.concatenate([vb_ref[idx], sfs[g].astype(BF16), sbst_ref[idx]], axis=0)
            outs[g] = jnp.dot(lhs, rhs, preferred_element_type=F32)

        def finish(g):
            rows = slice(g * c, (g + 1) * c)
            o = outs.pop(g)
            mu = jnp.mean(o, axis=-1, keepdims=True)
            oc = o - mu
            var = jnp.mean(oc * oc, axis=-1, keepdims=True)
            on = oc * lax.rsqrt(var + LN_EPS) * gnw
            gt = g_ref[rows, :]
            o_ref[rows, :] = (gt / (1.0 + jnp.exp(-gt)) * on).astype(o_ref.dtype)

        for s in range(RET_STEP_CHUNKS + 4):
            if s - 4 >= 0:
                finish(s - 4)
            if 0 <= s - 2 < RET_STEP_CHUNKS:
                mix(s - 2)
            if s < RET_STEP_CHUNKS:
                scores(s)
        sf_ref[...] = sfs[RET_STEP_CHUNKS]


def _retention(rq, rk, rv, rg, cosr, sinr, dfw, dbw, gnw, batch, seq):
    t = rq.shape[0]
    ts = RET_STEP_CHUNKS * RET_CHUNK
    nsteps = seq // ts
    nchunks = seq // RET_CHUNK
    kvi = lambda p, i: jnp.where(p == 0, nsteps - 1 - i, 0)
    qi = lambda p, i: jnp.where(p == 0, 0, i)
    kv_map = lambda b, h, p, i: (b * nsteps + kvi(p, i), h)
    q_map = lambda b, h, p, i: (b * nsteps + qi(p, i), h)
    tab_map = lambda b, h, p, i: (jnp.where(p == 0, nsteps - 1 - i, i), 0)
    head3 = lambda b, h, p, i: (h, 0, 0)
    return pl.pallas_call(
        functools.partial(_ret_kernel, nsteps=nsteps),
        grid=(batch, RET_HEADS, 2, nsteps),
        in_specs=[pl.BlockSpec((ts, RET_DK), q_map),
                  pl.BlockSpec((ts, RET_DK), kv_map),
                  pl.BlockSpec((ts, RET_DK), kv_map),
                  pl.BlockSpec((ts, RET_DK), q_map),
                  pl.BlockSpec((ts, RET_DK), tab_map),
                  pl.BlockSpec((ts, RET_DK), tab_map),
                  pl.BlockSpec((1, 1, LANES), head3),
                  pl.BlockSpec((1, 1, LANES), head3),
                  pl.BlockSpec((1, RET_DK), lambda b, h, p, i: (0, h))],
        out_specs=pl.BlockSpec((ts, RET_DK), q_map),
        out_shape=jax.ShapeDtypeStruct((t, RET_WIDTH), BF16),
        scratch_shapes=[pltpu.VMEM((RET_DK, RET_DK), F32),
                        pltpu.VMEM((RET_DK, RET_DK), F32),
                        pltpu.VMEM((nchunks, RET_DK, RET_DK), BF16),
                        pltpu.VMEM((nchunks, RET_DK, RET_DK), F32),
                        pltpu.VMEM((nchunks, RET_CHUNK, RET_DK), BF16),
                        pltpu.VMEM((nchunks, RET_CHUNK, RET_DK), BF16)],
        compiler_params=_cparams(("parallel", "parallel", "arbitrary", "arbitrary")),
        name="retention",
    )(rq, rk, rv, rg, cosr, sinr, dfw, dbw, gnw)


def _attn_kernel(qt_ref, k_ref, vt_ref, o_ref, ste_ref, sto_ref, acc_ref, *, seq):
    nkv = seq // TK_ATT
    npair = (seq // TQ_ATT) * nkv
    nt = (((1,), (1,)), ((), ()))
    nsub = TK_ATT // KSUB_ATT
    sub_per_chunk = TM_PROJ // KSUB_ATT
    neg_inf = jnp.full((1, TQ_ATT), -jnp.inf, F32)

    def score_steps(t, st_ref, tile_max):
        t = jnp.minimum(t, npair - 1)
        qi = t // nkv
        ks = pl.multiple_of((t % nkv) * TK_ATT, TK_ATT)

        def step(h, s):
            def run():
                hs = slice(h * HEAD_PAD, (h + 1) * HEAD_PAD)
                sub = slice(s * KSUB_ATT, (s + 1) * KSUB_ATT)
                st_ref[h, sub, :] = jnp.dot(k_ref[pl.ds(ks + s * KSUB_ATT, KSUB_ATT), hs], qt_ref[0, qi, hs, :],
                                            preferred_element_type=F32)
                top = jnp.max(st_ref[h, sub, :], axis=0, keepdims=True)
                tile_max[h] = top if s == 0 else jnp.maximum(tile_max[h], top)
            return run
        return [step(h, s) for h in range(2) for s in range(nsub)]

    def softmax_steps(t, st_ref, run_max, tile_max, out):
        kj = t % nkv
        steps = []
        for h in range(2):
            state = {}

            def start(h=h, state=state):
                m = jnp.where(kj == 0, neg_inf, run_max[h])
                state["mn"] = jnp.maximum(m, tile_max[h])
                state["alpha"] = jnp.exp2(m - state["mn"])

            def part(s, h=h, state=state):
                def run():
                    x = st_ref[h, s * KSUB_ATT:(s + 1) * KSUB_ATT, :] - state["mn"]
                    pb = jnp.exp2(x.astype(BF16))
                    c, r = divmod(s, sub_per_chunk)
                    pv = jnp.dot(vt_ref[0, kj * VT_CHUNKS + c, h * VT_ROWS:(h + 1) * VT_ROWS,
                                        r * KSUB_ATT:(r + 1) * KSUB_ATT],
                                 pb, preferred_element_type=F32)
                    state["pv"] = pv if s == 0 else state["pv"] + pv
                    if s == nsub - 1:
                        old = jnp.where(kj == 0, 0.0, acc_ref[h])
                        acc_ref[h] = state["alpha"] * old + state["pv"]
                        out[h] = state["mn"]
                return run
            steps += [start] + [part(s) for s in range(nsub)]
        return steps

    def interleave(mxu_steps, vpu_steps):
        n = max(len(mxu_steps), len(vpu_steps))
        for i in range(n):
            if i < len(mxu_steps):
                mxu_steps[i]()
            if i < len(vpu_steps):
                vpu_steps[i]()

    def body(jj, carry):
        run_max, max_even = carry
        t0 = 2 * jj
        mid, out, max_odd, max_next = {}, {}, {}, {}
        interleave(score_steps(t0 + 1, sto_ref, max_odd), softmax_steps(t0, ste_ref, run_max, max_even, mid))
        interleave(score_steps(t0 + 2, ste_ref, max_next),
                   softmax_steps(t0 + 1, sto_ref, (mid[0], mid[1]), max_odd, out))

        @pl.when((t0 + 1) % nkv == nkv - 1)
        def _():
            qi = t0 // nkv
            for h in range(2):
                acc = acc_ref[h]
                o_ref[0, qi, h * MLA_D_V:(h + 1) * MLA_D_V, :] = acc[0:MLA_D_V] / acc[MLA_D_V:MLA_D_V + 1]
        return (out[0], out[1]), (max_next[0], max_next[1])

    acc_ref[...] = jnp.zeros_like(acc_ref)
    max_first = {}
    for run in score_steps(0, ste_ref, max_first):
        run()
    lax.fori_loop(0, npair // 2, body, ((neg_inf, neg_inf), (max_first[0], max_first[1])))


def _attention(qt, k, vt, batch, seq):
    nq = seq // TQ_ATT
    nvt = seq // TM_PROJ
    assert (seq // TK_ATT) % 2 == 0, seq
    return pl.pallas_call(
        functools.partial(_attn_kernel, seq=seq),
        grid=(batch, MLA_HEADS // 2),
        in_specs=[pl.BlockSpec((1, nq, 2 * HEAD_PAD, TQ_ATT), lambda b, hp: (b, 0, hp, 0)),
                  pl.BlockSpec((seq, 2 * HEAD_PAD), lambda b, hp: (b, hp)),
                  pl.BlockSpec((1, nvt, 2 * VT_ROWS, TM_PROJ), lambda b, hp: (b, 0, hp, 0))],
        out_specs=pl.BlockSpec((1, nq, 2 * MLA_D_V, TQ_ATT), lambda b, hp: (b, 0, hp, 0)),
        out_shape=jax.ShapeDtypeStruct((batch, nq, MLA_HEADS * MLA_D_V, TQ_ATT), F32),
        scratch_shapes=[pltpu.VMEM((2, TK_ATT, TQ_ATT), F32),
                        pltpu.VMEM((2, TK_ATT, TQ_ATT), F32),
                        pltpu.VMEM((2, VT_ROWS, TQ_ATT), F32)],
        compiler_params=_cparams(("parallel", "parallel")),
        name="attention",
    )(qt, k, vt)


def _layer_norm(z, w, b):
    mu = jnp.mean(z, axis=-1, keepdims=True)
    zc = z - mu
    var = jnp.mean(zc * zc, axis=-1, keepdims=True)
    return zc * lax.rsqrt(var + LN_EPS) * w + b


def _split_bf16(a):
    hi = a.astype(BF16)
    lo = (a - hi.astype(F32)).astype(BF16)
    return hi, lo


def _outproj_kernel(x_ref, ret_ref, att_ref, wo1_ref, wo2_ref, onw_ref, lw_ref, lb_ref, wr_ref,
                    y_ref, lg_ref):
    wh, wl = _split_bf16(wr_ref[...])
    nt = (((1,), (1,)), ((), ()))
    nsub = x_ref.shape[0] // LANES
    pre = {}

    def project(r):
        rows = slice(r * LANES, (r + 1) * LANES)
        att = att_ref[0, 0, :, rows].T
        an = att * lax.rsqrt(jnp.mean(att * att, axis=-1, keepdims=True) + RMS_EPS) * onw_ref[...]
        mix = jnp.dot(ret_ref[rows, :], wo1_ref[...], preferred_element_type=F32)
        mix += jnp.dot(an.astype(BF16), wo2_ref[...], preferred_element_type=F32)
        pre[r] = DN_ALPHA * x_ref[rows, :] + mix

    def normalize(r):
        rows = slice(r * LANES, (r + 1) * LANES)
        y = _layer_norm(pre.pop(r), lw_ref[...], lb_ref[...])
        y_ref[rows, :] = y
        yh, yl = _split_bf16(y)
        lt = lax.dot_general(wh, yh, nt, preferred_element_type=F32)
        lt += lax.dot_general(wh, yl, nt, preferred_element_type=F32)
        lt += lax.dot_general(wl, yh, nt, preferred_element_type=F32)
        lg_ref[r] = lt

    for r in range(nsub + 1):
        if r < nsub:
            project(r)
        if r >= 1:
            normalize(r - 1)


def _outproj(x2d, ret, att_t, wo1, wo2, onw, lw, lb, wrt):
    t = x2d.shape[0]
    tm = TM_PROJ
    nseq = att_t.shape[1]
    assert att_t.shape[3] == tm
    row = lambda i: (i, 0)
    const = lambda i: (0, 0)
    return pl.pallas_call(
        _outproj_kernel,
        grid=(t // tm,),
        in_specs=[pl.BlockSpec((tm, D_MODEL), row),
                  pl.BlockSpec((tm, RET_WIDTH), row),
                  pl.BlockSpec((1, 1, RET_WIDTH, tm), lambda i: (i // nseq, i % nseq, 0, 0)),
                  pl.BlockSpec((RET_WIDTH, D_MODEL), const),
                  pl.BlockSpec((RET_WIDTH, D_MODEL), const),
                  pl.BlockSpec((1, RET_WIDTH), const),
                  pl.BlockSpec((1, D_MODEL), const),
                  pl.BlockSpec((1, D_MODEL), const),
                  pl.BlockSpec((N_EXPERTS, D_MODEL), const)],
        out_specs=[pl.BlockSpec((tm, D_MODEL), row),
                   pl.BlockSpec((tm // LANES, N_EXPERTS, LANES), lambda i: (i, 0, 0))],
        out_shape=[jax.ShapeDtypeStruct((t, D_MODEL), F32),
                   jax.ShapeDtypeStruct((t // LANES, N_EXPERTS, LANES), F32)],
        compiler_params=_cparams(("parallel",)),
        name="outproj",
    )(x2d, ret, att_t, wo1, wo2, onw, lw, lb, wrt)


def _route_kernel(lg_ref, rank_ref, gate_ref, cnt_ref, off_ref, blk_ref, boff_ref, *, cap, nb):
    e = N_EXPERTS
    ntiles = nb // BLOCKS_PER_TILE
    lg = lg_ref[...]
    ex = jnp.exp(lg - jnp.max(lg, axis=1, keepdims=True))
    aff = ex / jnp.sum(ex, axis=1, keepdims=True)
    gate_ref[...] = aff

    def count(mask):
        part = jnp.sum(mask.astype(F32), axis=0)
        return jnp.broadcast_to(jnp.sum(part, axis=1, keepdims=True), (e, LANES))

    def bisect(_, lohi):
        lo, hi = lohi
        mid = 0.5 * (lo + hi)
        ok = count(aff >= mid[None]) >= float(cap)
        return jnp.where(ok, mid, lo), jnp.where(ok, hi, mid)

    lo0 = jnp.zeros((e, LANES), F32)
    hi0 = jnp.full((e, LANES), 2.0, F32)
    lo, hi = lax.fori_loop(0, THRESHOLD_STEPS, bisect, (lo0, hi0))
    gt = aff >= hi[None]
    eq = jnp.logical_and(aff >= lo[None], jnp.logical_not(gt))
    need = float(cap) - count(gt)

    ii = lax.broadcasted_iota(jnp.int32, (LANES, LANES), 0)
    jj = lax.broadcasted_iota(jnp.int32, (LANES, LANES), 1)
    upper = (ii <= jj).astype(BF16)
    ones = jnp.ones((LANES, LANES), BF16)

    def block_scan(mask):
        m2 = mask.astype(BF16).reshape(nb * e, LANES)
        incl = jnp.dot(m2, upper, preferred_element_type=F32).reshape(nb, e, LANES)
        tot = jnp.dot(m2, ones, preferred_element_type=F32).reshape(nb, e, LANES)
        return incl, tot

    def leading_excl_scan(src_ref, dst_ref, n):
        def step(j, carry):
            dst_ref[j] = carry
            return carry + src_ref[j]
        return lax.fori_loop(0, n, step, jnp.zeros((e, LANES), F32))

    eqf = eq.astype(F32)
    incl, tot = block_scan(eq)
    blk_ref[...] = tot
    leading_excl_scan(blk_ref, boff_ref, nb)
    eq_before = incl - eqf + boff_ref[...]
    sel = jnp.logical_or(gt, jnp.logical_and(eq, eq_before < need[None]))

    self32 = sel.astype(F32)
    incl, tot = block_scan(sel)
    local = (incl - self32).reshape(ntiles, BLOCKS_PER_TILE, e, LANES)
    tot4 = tot.reshape(ntiles, BLOCKS_PER_TILE, e, LANES)
    ranks = [local[:, 0]]
    run = tot4[:, 0]
    for b in range(1, BLOCKS_PER_TILE):
        ranks.append(local[:, b] + run)
        run = run + tot4[:, b]
    rank = jnp.stack(ranks, axis=1).reshape(nb, e, LANES)
    rank_ref[...] = jnp.where(sel, rank, NOT_ROUTED)
    cnt_ref[...] = run
    blk_ref[0:ntiles] = run
    leading_excl_scan(blk_ref, boff_ref, ntiles)
    off_ref[...] = boff_ref[0:ntiles]


def _route(logits, cap):
    nb = logits.shape[0]
    ntiles = nb // BLOCKS_PER_TILE
    big = jax.ShapeDtypeStruct((nb, N_EXPERTS, LANES), F32)
    small = jax.ShapeDtypeStruct((ntiles, N_EXPERTS, LANES), F32)
    return pl.pallas_call(
        functools.partial(_route_kernel, cap=cap, nb=nb),
        out_shape=[big, big, small, small],
        scratch_shapes=[pltpu.VMEM((nb, N_EXPERTS, LANES), F32),
                        pltpu.VMEM((nb, N_EXPERTS, LANES), F32)],
        compiler_params=pltpu.CompilerParams(vmem_limit_bytes=VMEM_LIMIT),
        name="route",
    )(logits)


def _lead(off_ref, tile, ex):
    pos = off_ref[tile * N_EXPERTS + ex]
    lead = lax.rem(pos, ROW_ALIGN)
    return pos - lead, lead


def _overflow_chunks(lead, cnt):
    return lax.div(jnp.maximum(lead + cnt - SLAB_ROWS, 0) + (CHUNK_ROWS - 1), CHUNK_ROWS)


def _compact_kernel(off_ref, cnt_ref, y_ref, rank_ref, xg_ref, p_ref, pc_ref, st_ref, ost_ref,
                    carry_ref, zero_ref, sem, osem, *, cap, cap_rows):
    i = pl.program_id(0)
    nt = pl.num_programs(0)
    slot = i % 2

    def slab(tile, slt, ex, fn):
        start, lead = _lead(off_ref, tile, ex)
        used = lead + cnt_ref[tile * N_EXPERTS + ex]
        for first, rows in SLAB_PIECES:
            def piece(first=first, rows=rows):
                fn(pltpu.make_async_copy(
                    st_ref.at[slt, pl.ds(ex * SLAB_ROWS + first, rows)],
                    xg_ref.at[pl.ds(pl.multiple_of(ex * cap_rows + start + first, ROW_ALIGN), rows)],
                    sem.at[slt]))
            if first == 0:
                piece()
            else:
                pl.when(used > first)(piece)

    @pl.when(i == 0)
    def _():
        carry_ref[...] = jnp.zeros_like(carry_ref)
        zero_ref[...] = jnp.zeros_like(zero_ref)
        tails = [pltpu.make_async_copy(zero_ref, xg_ref.at[pl.ds(ex * cap_rows + cap, cap_rows - cap)],
                                       osem.at[0]) for ex in range(N_EXPERTS)]
        for cp in tails:
            cp.start()
        for cp in tails:
            cp.wait()

    xb = y_ref[...].astype(BF16)
    rank = jnp.concatenate([rank_ref[b] for b in range(BLOCKS_PER_TILE)], axis=1)
    jslab = lax.broadcasted_iota(jnp.int32, (SLAB_ROWS, TM_MOE), 0).astype(F32)
    jgrp = lax.broadcasted_iota(jnp.int32, (ROW_ALIGN, TM_MOE), 0).astype(F32)
    last_group = []
    for ex in range(N_EXPERTS):
        _, lead = _lead(off_ref, i, ex)
        sh = rank[ex:ex + 1, :] + lead.astype(F32)
        grp = lax.div(lead + cnt_ref[i * N_EXPERTS + ex], ROW_ALIGN)
        p_ref[ex * SLAB_ROWS:(ex + 1) * SLAB_ROWS, :] = (sh == jslab).astype(BF16)
        pc_ref[ex * ROW_ALIGN:(ex + 1) * ROW_ALIGN, :] = (
            sh - (grp * ROW_ALIGN).astype(F32) == jgrp).astype(BF16)
        last_group.append(grp)
    rows = jnp.dot(p_ref[...], xb, preferred_element_type=F32)
    tail = jnp.dot(pc_ref[...], xb, preferred_element_type=F32)
    for ex in range(N_EXPERTS):
        lo = ex * SLAB_ROWS
        old = carry_ref[ex]
        st_ref[slot, lo:lo + ROW_ALIGN, :] = (rows[lo:lo + ROW_ALIGN] + old).astype(BF16)
        st_ref[slot, lo + ROW_ALIGN:lo + SLAB_ROWS, :] = rows[lo + ROW_ALIGN:lo + SLAB_ROWS].astype(BF16)
        carry_ref[ex] = tail[ex * ROW_ALIGN:(ex + 1) * ROW_ALIGN] + jnp.where(last_group[ex] == 0, old, 0.0)

    @pl.when(i >= 1)
    def _():
        for ex in range(N_EXPERTS):
            slab(i - 1, 1 - slot, ex, lambda cp: cp.wait())

    for ex in range(N_EXPERTS):
        slab(i, slot, ex, lambda cp: cp.start())

    extra = [_overflow_chunks(_lead(off_ref, i, ex)[1], cnt_ref[i * N_EXPERTS + ex]) for ex in range(N_EXPERTS)]

    @pl.when(functools.reduce(lambda a, b: a + b, extra) > 0)
    def _():
        xo = y_ref[...].astype(BF16)
        jchunk = lax.broadcasted_iota(jnp.int32, (CHUNK_ROWS, TM_MOE), 0).astype(F32)
        for ex in range(N_EXPERTS):
            start, lead = _lead(off_ref, i, ex)
            sh = (jnp.concatenate([rank_ref[b, ex:ex + 1, :] for b in range(BLOCKS_PER_TILE)], axis=1)
                  + lead.astype(F32))

            def chunk(c, carry, ex=ex, start=start, sh=sh):
                first = SLAB_ROWS + c * CHUNK_ROWS
                pc = (sh == jchunk + first.astype(F32)).astype(BF16)
                ost_ref[...] = jnp.dot(pc, xo, preferred_element_type=F32).astype(BF16)
                cp = pltpu.make_async_copy(
                    ost_ref,
                    xg_ref.at[pl.ds(pl.multiple_of(ex * cap_rows + start + first, ROW_ALIGN), CHUNK_ROWS)],
                    osem.at[0])
                cp.start()
                cp.wait()
                return carry

            lax.fori_loop(0, extra[ex], chunk, 0)

    @pl.when(i == nt - 1)
    def _():
        for ex in range(N_EXPERTS):
            slab(i, slot, ex, lambda cp: cp.wait())


def _compact(off, cnt, y1, rank_em, cap, cap_rows):
    n = y1.shape[0]
    ntiles = n // TM_MOE
    gs = pltpu.PrefetchScalarGridSpec(
        num_scalar_prefetch=2,
        grid=(ntiles,),
        in_specs=[pl.BlockSpec((TM_MOE, D_MODEL), lambda i, o, c: (i, 0)),
                  pl.BlockSpec((BLOCKS_PER_TILE, N_EXPERTS, LANES), lambda i, o, c: (i, 0, 0))],
        out_specs=pl.BlockSpec(memory_space=pl.ANY),
        scratch_shapes=[pltpu.VMEM((N_EXPERTS * SLAB_ROWS, TM_MOE), BF16),
                        pltpu.VMEM((N_EXPERTS * ROW_ALIGN, TM_MOE), BF16),
                        pltpu.VMEM((2, N_EXPERTS * SLAB_ROWS, D_MODEL), BF16),
                        pltpu.VMEM((CHUNK_ROWS, D_MODEL), BF16),
                        pltpu.VMEM((N_EXPERTS, ROW_ALIGN, D_MODEL), F32),
                        pltpu.VMEM((cap_rows - cap, D_MODEL), BF16),
                        pltpu.SemaphoreType.DMA((2,)),
                        pltpu.SemaphoreType.DMA((1,))])
    return pl.pallas_call(
        functools.partial(_compact_kernel, cap=cap, cap_rows=cap_rows),
        grid_spec=gs,
        out_shape=jax.ShapeDtypeStruct((N_EXPERTS * cap_rows, D_MODEL), BF16),
        compiler_params=_cparams(("arbitrary",)),
        name="compact",
    )(off, cnt, y1, rank_em)


def _ffn_kernel(x_ref, wg_ref, wu_ref, wd_ref, y_ref, wgb_ref, wub_ref, wdb_ref, *, nvalid):
    j = pl.program_id(1)

    @pl.when(j == 0)
    def _():
        wgb_ref[...] = wg_ref[0].astype(BF16)
        wub_ref[...] = wu_ref[0].astype(BF16)
        wdb_ref[...] = wd_ref[0].astype(BF16)

    @pl.when(j < nvalid)
    def _():
        x = x_ref[...]
        hg = jnp.dot(x, wgb_ref[...], preferred_element_type=F32)
        hu = jnp.dot(x, wub_ref[...], preferred_element_type=F32)
        h = (hg / (1.0 + jnp.exp(-hg)) * hu).astype(BF16)
        y_ref[...] = jnp.dot(h, wdb_ref[...], preferred_element_type=F32).astype(BF16)

    @pl.when(j >= nvalid)
    def _():
        y_ref[...] = jnp.zeros_like(y_ref)


def _ffn(xg, wg, wu, wd, cap, cap_rows):
    nvalid = cap // TM_FFN
    nt = cap_rows // TM_FFN
    wmap = lambda ex, j: (ex, 0, 0)
    return pl.pallas_call(
        functools.partial(_ffn_kernel, nvalid=nvalid),
        grid=(N_EXPERTS, nt),
        in_specs=[pl.BlockSpec((TM_FFN, D_MODEL), lambda ex, j: (ex * nt + jnp.minimum(j, nvalid - 1), 0)),
                  pl.BlockSpec((1, D_MODEL, D_MODEL), wmap),
                  pl.BlockSpec((1, D_MODEL, D_MODEL), wmap),
                  pl.BlockSpec((1, D_MODEL, D_MODEL), wmap)],
        out_specs=pl.BlockSpec((TM_FFN, D_MODEL), lambda ex, j: (ex * nt + j, 0)),
        out_shape=jax.ShapeDtypeStruct((N_EXPERTS * cap_rows, D_MODEL), BF16),
        scratch_shapes=[pltpu.VMEM((D_MODEL, D_MODEL), BF16)] * 3,
        compiler_params=_cparams(("arbitrary", "arbitrary")),
        name="ffn",
    )(xg, wg, wu, wd)


def _combine_kernel(off_ref, cnt_ref, y1_ref, rank_ref, gate_ref, yh_ref, lw_ref, lb_ref,
                    o_ref, ybuf, obuf, acc_ref, sem, osem, *, cap_rows):
    i = pl.program_id(0)
    nt = pl.num_programs(0)
    slot = i % 2

    ncol = N_EXPERTS * SLAB_ROWS

    def fetch(tile, slt, fn):
        for ex in range(N_EXPERTS):
            start, lead = _lead(off_ref, tile, ex)
            used = lead + cnt_ref[tile * N_EXPERTS + ex]
            for first, rows in SLAB_PIECES:
                def piece(ex=ex, start=start, first=first, rows=rows):
                    fn(pltpu.make_async_copy(
                        yh_ref.at[pl.ds(pl.multiple_of(ex * cap_rows + start + first, ROW_ALIGN), rows)],
                        ybuf.at[slt, pl.ds(ex * SLAB_ROWS + first, rows)],
                        sem.at[slt]))
                if first == 0:
                    piece()
                else:
                    pl.when(used > first)(piece)

    @pl.when(i == 0)
    def _():
        ybuf[...] = jnp.zeros_like(ybuf)
        fetch(0, 0, lambda cp: cp.start())

    @pl.when(i + 1 < nt)
    def _():
        fetch(i + 1, 1 - slot, lambda cp: cp.start())

    rank = rank_ref[...]
    gate = gate_ref[...]
    er = lax.broadcasted_iota(jnp.int32, (N_EXPERTS, ncol), 0)
    ec = lax.broadcasted_iota(jnp.int32, (N_EXPERTS, ncol), 1)
    expand = jnp.logical_and(ec >= er * SLAB_ROWS, ec < (er + 1) * SLAB_ROWS).astype(BF16)
    rexp = jnp.dot(rank.astype(BF16), expand, preferred_element_type=F32)
    gexp = jnp.dot(gate.astype(BF16), expand, preferred_element_type=F32)
    col = lax.broadcasted_iota(jnp.int32, (1, ncol), 1)
    want = jnp.zeros((1, ncol), F32)
    for ex in range(N_EXPERTS):
        _, lead = _lead(off_ref, i, ex)
        inside = jnp.logical_and(col >= ex * SLAB_ROWS, col < (ex + 1) * SLAB_ROWS)
        want = jnp.where(inside, (col - ex * SLAB_ROWS - lead).astype(F32), want)
    gmat = jnp.where(rexp == want, gexp, 0.0).astype(BF16)

    fetch(i, slot, lambda cp: cp.wait())
    extra = [_overflow_chunks(_lead(off_ref, i, ex)[1], cnt_ref[i * N_EXPERTS + ex]) for ex in range(N_EXPERTS)]
    any_extra = functools.reduce(lambda a, b: a + b, extra)

    @pl.when(any_extra == 0)
    def _():
        half = TM_MOE // 2
        mix = [jnp.dot(gmat[r * half:(r + 1) * half], ybuf[slot], preferred_element_type=F32) for r in range(2)]
        for r in range(2):
            rows = slice(r * half, (r + 1) * half)
            o_ref[rows, :] = _layer_norm(DN_ALPHA * y1_ref[rows, :] + mix[r], lw_ref[...], lb_ref[...])

    @pl.when(any_extra > 0)
    def _():
        acc_ref[...] = jnp.dot(gmat, ybuf[slot], preferred_element_type=F32)
        jc = lax.broadcasted_iota(jnp.int32, (TM_MOE, CHUNK_ROWS), 1).astype(F32)
        for ex in range(N_EXPERTS):
            start, lead = _lead(off_ref, i, ex)

            def chunk(c, carry, ex=ex, start=start, lead=lead):
                first = SLAB_ROWS + c * CHUNK_ROWS
                cp = pltpu.make_async_copy(
                    yh_ref.at[pl.ds(pl.multiple_of(ex * cap_rows + start + first, ROW_ALIGN), CHUNK_ROWS)],
                    obuf, osem.at[0])
                cp.start()
                cp.wait()
                gc = jnp.where(rank[:, ex:ex + 1] + lead.astype(F32) == jc + first.astype(F32),
                               gate[:, ex:ex + 1].astype(BF16).astype(F32), 0.0).astype(BF16)
                acc_ref[...] += jnp.dot(gc, obuf[...], preferred_element_type=F32)
                return carry

            lax.fori_loop(0, extra[ex], chunk, 0)
        o_ref[...] = _layer_norm(DN_ALPHA * y1_ref[...] + acc_ref[...], lw_ref[...], lb_ref[...])


def _combine(off, cnt, y1, rank_tm, gate_tm, yh, lw, lb, cap_rows):
    n = y1.shape[0]
    ntiles = n // TM_MOE
    row = lambda i, o, c: (i, 0)
    const = lambda i, o, c: (0, 0)
    gs = pltpu.PrefetchScalarGridSpec(
        num_scalar_prefetch=2,
        grid=(ntiles,),
        in_specs=[pl.BlockSpec((TM_MOE, D_MODEL), row),
                  pl.BlockSpec((TM_MOE, N_EXPERTS), row),
                  pl.BlockSpec((TM_MOE, N_EXPERTS), row),
                  pl.BlockSpec(memory_space=pl.ANY),
                  pl.BlockSpec((1, D_MODEL), const),
                  pl.BlockSpec((1, D_MODEL), const)],
        out_specs=pl.BlockSpec((TM_MOE, D_MODEL), row),
        scratch_shapes=[pltpu.VMEM((2, N_EXPERTS * SLAB_ROWS, D_MODEL), BF16),
                        pltpu.VMEM((CHUNK_ROWS, D_MODEL), BF16),
                        pltpu.VMEM((TM_MOE, D_MODEL), F32),
                        pltpu.SemaphoreType.DMA((2,)),
                        pltpu.SemaphoreType.DMA((1,))])
    return pl.pallas_call(
        functools.partial(_combine_kernel, cap_rows=cap_rows),
        grid_spec=gs,
        out_shape=jax.ShapeDtypeStruct((n, D_MODEL), F32),
        compiler_params=_cparams(("arbitrary",)),
        name="combine",
    )(off, cnt, y1, rank_tm, gate_tm, yh, lw, lb)


def _rope_tables(seq):
    pos = jnp.arange(seq, dtype=F32)[:, None]
    inv_m = 1.0 / (ROPE_BASE ** (jnp.arange(0, MLA_D_ROPE, 2, dtype=F32) / MLA_D_ROPE))
    am = pos * inv_m[None, :]
    one = jnp.ones((seq, MLA_D_NOPE), F32)
    zero_n = jnp.zeros((seq, MLA_D_NOPE), F32)
    zero_p = jnp.zeros((seq, HEAD_PAD - MLA_D_NOPE - MLA_D_ROPE), F32)
    cm = jnp.concatenate([one, jnp.cos(am), jnp.cos(am), zero_p], axis=1)
    sm = jnp.concatenate([zero_n, jnp.sin(am), jnp.sin(am), zero_p], axis=1)
    inv_r = 1.0 / (ROPE_BASE ** (jnp.arange(0, RET_DK, 2, dtype=F32) / RET_DK))
    ar = pos * inv_r[None, :]
    cosr = jnp.concatenate([jnp.cos(ar), jnp.cos(ar)], axis=1)
    sinr = jnp.concatenate([-jnp.sin(ar), jnp.sin(ar)], axis=1)
    return cm, sm, cosr, sinr


def _prep_weights(w_in, mla_w_uq, mla_w_ukv):
    half = MLA_D_ROPE // 2
    base = 4 * RET_WIDTH + MLA_Q_LORA + MLA_KV_LORA
    w_kr = w_in[:, base:base + MLA_D_ROPE]
    zn = jnp.zeros((D_MODEL, MLA_D_NOPE), F32)
    zp = jnp.zeros((D_MODEL, HEAD_PAD - MLA_D_NOPE - MLA_D_ROPE), F32)
    kr_p = jnp.concatenate([zn, w_kr, zp], axis=1)
    kr_rot = jnp.concatenate([zn, -w_kr[:, half:], w_kr[:, :half], zp], axis=1)
    w_all = jnp.concatenate([w_in[:, :base], kr_p, kr_rot], axis=1).astype(BF16)

    wq = mla_w_uq.reshape(MLA_Q_LORA, MLA_HEADS, MLA_D_NOPE + MLA_D_ROPE)
    nope, ropew = wq[..., :MLA_D_NOPE], wq[..., MLA_D_NOPE:]
    zq = jnp.zeros((MLA_Q_LORA, MLA_HEADS, HEAD_PAD - MLA_D_NOPE - MLA_D_ROPE), F32)
    q_p = jnp.concatenate([nope, ropew, zq], axis=-1).reshape(MLA_Q_LORA, -1)
    q_rot = jnp.concatenate([jnp.zeros_like(nope), -ropew[..., half:], ropew[..., :half], zq],
                            axis=-1).reshape(MLA_Q_LORA, -1)
    wq2 = jnp.concatenate([q_p, q_rot], axis=1).astype(BF16)

    wkv3 = mla_w_ukv.reshape(MLA_KV_LORA, MLA_HEADS, MLA_D_NOPE + MLA_D_V)
    k_p = jnp.concatenate([wkv3[..., :MLA_D_NOPE],
                           jnp.zeros((MLA_KV_LORA, MLA_HEADS, HEAD_PAD - MLA_D_NOPE), F32)],
                          axis=-1).reshape(MLA_KV_LORA, -1)
    v_w = wkv3[..., MLA_D_NOPE:].reshape(MLA_KV_LORA, -1)
    wkv = jnp.concatenate([k_p, v_w], axis=1).astype(BF16)
    return w_all, wq2, wkv


def _mixer_ln(x, wts):
    batch, seq, _ = x.shape
    x2d = x.reshape(batch * seq, D_MODEL)
    cm, sm, cosr, sinr = _rope_tables(seq)
    rq, rk, rv, rg, q, k, v = _inproj(x2d, seq, wts["w_all"], wts["wq2"], wts["wkv"],
                                      wts["qnw"], wts["kvnw"], cm, sm)
    ret = _retention(rq, rk, rv, rg, cosr, sinr, wts["dfw"], wts["dbw"], wts["gnw"], batch, seq)
    att = _attention(q, k, v, batch, seq)
    return _outproj(x2d, ret, att, wts["wo1"], wts["wo2"], wts["onw"], wts["ln1w"], wts["ln1b"],
                    wts["wrt"])


def _moe_ln(y1, logits, wts):
    n = y1.shape[0]
    cap = EC_CAPACITY_FACTOR * n // N_EXPERTS
    assert cap % TM_FFN == 0, (n, cap)
    cap_rows = cap + TM_FFN
    rank_em, gate_em, cnt_rep, off_rep = _route(logits, cap)
    cnt = cnt_rep[:, :, 0].astype(jnp.int32).reshape(-1)
    off = off_rep[:, :, 0].astype(jnp.int32).reshape(-1)
    to_tm = lambda a: jnp.transpose(a, (0, 2, 1)).reshape(n, N_EXPERTS)
    xg = _compact(off, cnt, y1, rank_em, cap, cap_rows)
    yh = _ffn(xg, wts["wg"], wts["wu"], wts["wd"], cap, cap_rows)
    return _combine(off, cnt, y1, to_tm(rank_em), to_tm(gate_em), yh, wts["ln2w"], wts["ln2b"],
                    cap_rows)


def _layer(x, wts):
    batch, seq, _ = x.shape
    y1, logits = _mixer_ln(x, wts)
    return _moe_ln(y1, logits, wts).reshape(batch, seq, D_MODEL)


def kernel(x_prompt, x_sample, w_in, ret_decay_fwd, ret_decay_bwd, ret_gn_w, mla_q_norm_w, mla_w_uq,
           mla_kv_norm_w, mla_w_ukv, mla_out_norm_w, w_out, ln1_w, ln1_b, w_router, w_gate, w_up,
           w_down, ln2_w, ln2_b):
    depth = w_in.shape[0]
    y_prompt, y_sample = x_prompt, x_sample
    for l in range(depth):
        w_all, wq2, wkv = _prep_weights(w_in[l], mla_w_uq[l], mla_w_ukv[l])
        rep = lambda a: jnp.broadcast_to(a.astype(F32)[:, None, None], (RET_HEADS, 1, LANES))
        wts = dict(
            w_all=w_all, wq2=wq2, wkv=wkv,
            qnw=mla_q_norm_w[l][None, :], kvnw=mla_kv_norm_w[l][None, :],
            dfw=rep(ret_decay_fwd[l]), dbw=rep(ret_decay_bwd[l]), gnw=ret_gn_w[l][None, :],
            wo1=w_out[l][:RET_WIDTH].astype(BF16), wo2=w_out[l][RET_WIDTH:].astype(BF16),
            onw=mla_out_norm_w[l][None, :], ln1w=ln1_w[l][None, :], ln1b=ln1_b[l][None, :],
            wrt=w_router[l].T,
            wg=w_gate[l], wu=w_up[l], wd=w_down[l],
            ln2w=ln2_w[l][None, :], ln2b=ln2_b[l][None, :])
        y_prompt = _layer(y_prompt, wts)
        y_sample = _layer(y_sample, wts)
    return (y_prompt, y_sample)
```

```python
import functools
import math

import numpy as np
import jax
import jax.numpy as jnp
from jax import lax
from jax.experimental import pallas as pl
from jax.experimental.pallas import tpu as pltpu

F32 = jnp.float32
BF16 = jnp.bfloat16

D_MODEL = 1024
RET_WIDTH = 512
RET_HEADS = 4
RET_DK = 128
RET_CHUNK = 128
MLA_HEADS = 8
MLA_D_NOPE = 64
MLA_D_ROPE = 32
MLA_D_V = 64
MLA_Q_LORA = 256
MLA_KV_LORA = 128
N_EXPERTS = 16
EC_CAPACITY_FACTOR = 2
ROPE_BASE = 10000.0
LN_EPS = 1e-5
RMS_EPS = 1e-6
DN_ALPHA = 2.0 ** 0.25

LANES = 128
HEAD_PAD = 128
PROJ_COLS = 4 * RET_WIDTH + MLA_Q_LORA + MLA_KV_LORA + 2 * LANES
VMEM_LIMIT = 56 * 1024 * 1024

TM_PROJ = 512
RET_STEP_CHUNKS = 16
TQ_ATT = 512
VT_CHUNKS = 2
VT_ROWS = 80
TK_ATT = VT_CHUNKS * TM_PROJ
KSUB_ATT = 256
TM_MOE = 256
ROW_ALIGN = 16
CHUNK_ROWS = 64
SLAB_ROWS = ROW_ALIGN + CHUNK_ROWS
SLAB_PIECES = ((0, 48), (48, 32))
THRESHOLD_STEPS = 64
NOT_ROUTED = -512.0
TM_FFN = 512
BLOCKS_PER_TILE = TM_MOE // LANES

_QK_SCALE = (MLA_D_NOPE + MLA_D_ROPE) ** -0.5 * math.log2(math.e)


def _cparams(sem):
    return pltpu.CompilerParams(dimension_semantics=sem, vmem_limit_bytes=VMEM_LIMIT)


def _inproj_kernel(x_ref, w_ref, wq_ref, wkv_ref, qnw_ref, kvnw_ref, cm_ref, sm_ref,
                   rq_ref, rk_ref, rv_ref, rg_ref, qt_ref, k_ref, vt_ref):
    xb = x_ref[...].astype(BF16)
    proj = jnp.dot(xb, w_ref[...], preferred_element_type=F32)
    rq_ref[...] = proj[:, 0:512]
    rk_ref[...] = proj[:, 512:1024]
    rv_ref[...] = proj[:, 1024:1536]
    rg_ref[...] = proj[:, 1536:2048]
    cm = cm_ref[...]
    sm = sm_ref[...]

    cq = proj[:, 2048:2304]
    cqn = cq * lax.rsqrt(jnp.mean(cq * cq, axis=-1, keepdims=True) + RMS_EPS) * qnw_ref[...]
    qq = jnp.dot(cqn.astype(BF16), wq_ref[...], preferred_element_type=F32)

    ckv = proj[:, 2304:2432]
    ckvn = ckv * lax.rsqrt(jnp.mean(ckv * ckv, axis=-1, keepdims=True) + RMS_EPS) * kvnw_ref[...]
    kv = jnp.dot(ckvn.astype(BF16), wkv_ref[...], preferred_element_type=F32)

    krope = proj[:, 2432:2560] * cm + proj[:, 2560:2688] * sm
    for h in range(MLA_HEADS):
        lo = h * HEAD_PAD
        qh = qq[:, lo:lo + HEAD_PAD] * cm + qq[:, 1024 + lo:1024 + lo + HEAD_PAD] * sm
        qt_ref[0, 0, lo:lo + HEAD_PAD, :] = (qh * _QK_SCALE).T.astype(BF16)
        k_ref[:, lo:lo + HEAD_PAD] = (kv[:, lo:lo + HEAD_PAD] + krope).astype(BF16)
    vt = kv[:, 1024:1536].T.astype(BF16)
    pad_rows = VT_ROWS - MLA_D_V
    ones_row = (lax.broadcasted_iota(jnp.int32, (pad_rows, vt.shape[1]), 0) == 0).astype(BF16)
    for h in range(MLA_HEADS):
        vt_ref[0, 0, h * VT_ROWS:h * VT_ROWS + MLA_D_V, :] = vt[h * MLA_D_V:(h + 1) * MLA_D_V, :]
        vt_ref[0, 0, h * VT_ROWS + MLA_D_V:(h + 1) * VT_ROWS, :] = ones_row


def _inproj(x2d, seq, w_all, wq2, wkv, qnw, kvnw, cm, sm):
    t = x2d.shape[0]
    tm = TM_PROJ
    nseq = seq // tm
    row = lambda i: (i, 0)
    const = lambda i: (0, 0)
    tab = lambda i: (i % nseq, 0)
    outs = [jax.ShapeDtypeStruct((t, RET_WIDTH), F32)] * 4 + [
        jax.ShapeDtypeStruct((t // seq, nseq, MLA_HEADS * HEAD_PAD, tm), BF16),
        jax.ShapeDtypeStruct((t, MLA_HEADS * HEAD_PAD), BF16),
        jax.ShapeDtypeStruct((t // seq, nseq, MLA_HEADS * VT_ROWS, tm), BF16)]
    return pl.pallas_call(
        _inproj_kernel,
        grid=(t // tm,),
        in_specs=[pl.BlockSpec((tm, D_MODEL), row),
                  pl.BlockSpec((D_MODEL, PROJ_COLS), const),
                  pl.BlockSpec((MLA_Q_LORA, 2 * MLA_HEADS * HEAD_PAD), const),
                  pl.BlockSpec((MLA_KV_LORA, MLA_HEADS * (HEAD_PAD + MLA_D_V)), const),
                  pl.BlockSpec((1, MLA_Q_LORA), const),
                  pl.BlockSpec((1, MLA_KV_LORA), const),
                  pl.BlockSpec((tm, LANES), tab),
                  pl.BlockSpec((tm, LANES), tab)],
        out_specs=[pl.BlockSpec((tm, RET_WIDTH), row)] * 4 + [
            pl.BlockSpec((1, 1, MLA_HEADS * HEAD_PAD, tm), lambda i: (i // nseq, i % nseq, 0, 0)),
            pl.BlockSpec((tm, MLA_HEADS * HEAD_PAD), row),
            pl.BlockSpec((1, 1, MLA_HEADS * VT_ROWS, tm), lambda i: (i // nseq, i % nseq, 0, 0))],
        out_shape=outs,
        compiler_params=_cparams(("parallel",)),
        name="inproj",
    )(x2d, w_all, wq2, wkv, qnw, kvnw, cm, sm)


def _log_sigmoid(x):
    return jnp.minimum(x, 0.0) - jnp.log(1.0 + jnp.exp(-jnp.abs(x)))


def _ret_kernel(q_ref, k_ref, v_ref, g_ref, cos_ref, sin_ref, df_ref, db_ref, gnw_ref,
                o_ref, sf_ref, sb_ref, sbst_ref, kvf_ref, kb_ref, vb_ref, *, nsteps):
    phase = pl.program_id(2)
    step = pl.program_id(3)
    c = RET_CHUNK
    lgf = _log_sigmoid(df_ref[0])
    lgb = _log_sigmoid(db_ref[0])
    rpos = lax.broadcasted_iota(jnp.int32, (c, c), 0).astype(F32)
    cpos = lax.broadcasted_iota(jnp.int32, (c, c), 1).astype(F32)
    tn = (((0,), (0,)), ((), ()))

    def rope(x, cs, sn):
        return x * cs + pltpu.roll(x, RET_DK // 2, axis=1) * sn

    @pl.when(jnp.logical_and(phase == 0, step == 0))
    def _():
        sb_ref[...] = jnp.zeros_like(sb_ref)

    @pl.when(jnp.logical_and(phase == 1, step == 0))
    def _():
        sf_ref[...] = jnp.zeros_like(sf_ref)

    @pl.when(phase == 0)
    def _():
        kwb = jnp.exp(lgb * rpos)
        kwf = jnp.exp(lgf * (float(c - 1) - rpos))
        cdb = jnp.exp(lgb * float(c))
        kvstep = nsteps - 1 - step
        kvs = {}
        for g in range(RET_STEP_CHUNKS):
            rows = slice(g * c, (g + 1) * c)
            idx = kvstep * RET_STEP_CHUNKS + g
            kr = rope(k_ref[rows, :], cos_ref[rows, :], sin_ref[rows, :]) * (RET_DK ** -0.5)
            vb = v_ref[rows, :].astype(BF16)
            kb_ref[idx] = kr.astype(BF16)
            vb_ref[idx] = vb
            kw = jnp.concatenate([(kr * kwb).astype(BF16), (kr * kwf).astype(BF16)], axis=1)
            kvs[g] = lax.dot_general(kw, vb, tn, preferred_element_type=F32)
        sb = sb_ref[...]
        for g in range(RET_STEP_CHUNKS - 1, -1, -1):
            idx = kvstep * RET_STEP_CHUNKS + g
            sbst_ref[idx] = sb.astype(BF16)
            kvf_ref[idx] = kvs[g][c:2 * c]
            sb = cdb * sb + kvs[g][0:c]
        sb_ref[...] = sb

    @pl.when(phase == 1)
    def _():
        diff = rpos - cpos
        dmat = jnp.where(diff >= 0, jnp.exp(lgf * jnp.maximum(diff, 0.0)),
                         jnp.exp(lgb * jnp.maximum(-diff, 0.0)))
        qwf = jnp.exp(lgf * (rpos + 1.0))
        qwb = jnp.exp(lgb * (float(c) - rpos))
        cdf = jnp.exp(lgf * float(c))
        gnw = gnw_ref[...]
        qrs, scs, outs = {}, {}, {}
        sfs = {0: sf_ref[...]}

        def scores(g):
            rows = slice(g * c, (g + 1) * c)
            idx = step * RET_STEP_CHUNKS + g
            qrs[g] = rope(q_ref[rows, :], cos_ref[rows, :], sin_ref[rows, :])
            scs[g] = lax.dot_general(qrs[g].astype(BF16), kb_ref[idx], (((1,), (1,)), ((), ())),
                                     preferred_element_type=F32)
            sfs[g + 1] = cdf * sfs[g] + kvf_ref[idx]

        def mix(g):
            idx = step * RET_STEP_CHUNKS + g
            qr = qrs.pop(g)
            lhs = jnp.concatenate([(scs.pop(g) * dmat).astype(BF16), (qr * qwf).astype(BF16),
                                   (qr * qwb).astype(BF16)], axis=1)
            rhs = jnp.concatenate([vb_ref[idx], sfs[g].astype(BF16), sbst_ref[idx]], axis=0)
            outs[g] = jnp.dot(lhs, rhs, preferred_element_type=F32)

        def finish(g):
            rows = slice(g * c, (g + 1) * c)
            o = outs.pop(g)
            mu = jnp.mean(o, axis=-1, keepdims=True)
            oc = o - mu
            var = jnp.mean(oc * oc, axis=-1, keepdims=True)
            on = oc * lax.rsqrt(var + LN_EPS) * gnw
            gt = g_ref[rows, :]
            o_ref[rows, :] = (gt / (1.0 + jnp.exp(-gt)) * on).astype(o_ref.dtype)

        for s in range(RET_STEP_CHUNKS + 4):
            if s - 4 >= 0:
                finish(s - 4)
            if 0 <= s - 2 < RET_STEP_CHUNKS:
                mix(s - 2)
            if s < RET_STEP_CHUNKS:
                scores(s)
        sf_ref[...] = sfs[RET_STEP_CHUNKS]


def _retention(rq, rk, rv, rg, cosr, sinr, dfw, dbw, gnw, batch, seq):
    t = rq.shape[0]
    ts = RET_STEP_CHUNKS * RET_CHUNK
    nsteps = seq // ts
    nchunks = seq // RET_CHUNK
    kvi = lambda p, i: jnp.where(p == 0, nsteps - 1 - i, 0)
    qi = lambda p, i: jnp.where(p == 0, 0, i)
    kv_map = lambda b, h, p, i: (b * nsteps + kvi(p, i), h)
    q_map = lambda b, h, p, i: (b * nsteps + qi(p, i), h)
    tab_map = lambda b, h, p, i: (jnp.where(p == 0, nsteps - 1 - i, i), 0)
    head3 = lambda b, h, p, i: (h, 0, 0)
    return pl.pallas_call(
        functools.partial(_ret_kernel, nsteps=nsteps),
        grid=(batch, RET_HEADS, 2, nsteps),
        in_specs=[pl.BlockSpec((ts, RET_DK), q_map),
                  pl.BlockSpec((ts, RET_DK), kv_map),
                  pl.BlockSpec((ts, RET_DK), kv_map),
                  pl.BlockSpec((ts, RET_DK), q_map),
                  pl.BlockSpec((ts, RET_DK), tab_map),
                  pl.BlockSpec((ts, RET_DK), tab_map),
                  pl.BlockSpec((1, 1, LANES), head3),
                  pl.BlockSpec((1, 1, LANES), head3),
                  pl.BlockSpec((1, RET_DK), lambda b, h, p, i: (0, h))],
        out_specs=pl.BlockSpec((ts, RET_DK), q_map),
        out_shape=jax.ShapeDtypeStruct((t, RET_WIDTH), BF16),
        scratch_shapes=[pltpu.VMEM((RET_DK, RET_DK), F32),
                        pltpu.VMEM((RET_DK, RET_DK), F32),
                        pltpu.VMEM((nchunks, RET_DK, RET_DK), BF16),
                        pltpu.VMEM((nchunks, RET_DK, RET_DK), F32),
                        pltpu.VMEM((nchunks, RET_CHUNK, RET_DK), BF16),
                        pltpu.VMEM((nchunks, RET_CHUNK, RET_DK), BF16)],
        compiler_params=_cparams(("parallel", "parallel", "arbitrary", "arbitrary")),
        name="retention",
    )(rq, rk, rv, rg, cosr, sinr, dfw, dbw, gnw)


def _attn_kernel(qt_ref, k_ref, vt_ref, o_ref, ste_ref, sto_ref, acc_ref, *, seq):
    nkv = seq // TK_ATT
    npair = (seq // TQ_ATT) * nkv
    nt = (((1,), (1,)), ((), ()))
    nsub = TK_ATT // KSUB_ATT
    sub_per_chunk = TM_PROJ // KSUB_ATT
    neg_inf = jnp.full((1, TQ_ATT), -jnp.inf, F32)

    def score_steps(t, st_ref, tile_max):
        t = jnp.minimum(t, npair - 1)
        qi = t // nkv
        ks = pl.multiple_of((t % nkv) * TK_ATT, TK_ATT)

        def step(h, s):
            def run():
                hs = slice(h * HEAD_PAD, (h + 1) * HEAD_PAD)
                sub = slice(s * KSUB_ATT, (s + 1) * KSUB_ATT)
                st_ref[h, sub, :] = jnp.dot(k_ref[pl.ds(ks + s * KSUB_ATT, KSUB_ATT), hs], qt_ref[0, qi, hs, :],
                                            preferred_element_type=F32)
                top = jnp.max(st_ref[h, sub, :], axis=0, keepdims=True)
                tile_max[h] = top if s == 0 else jnp.maximum(tile_max[h], top)
            return run
        return [step(h, s) for h in range(2) for s in range(nsub)]

    def softmax_steps(t, st_ref, run_max, tile_max, out):
        kj = t % nkv
        steps = []
        for h in range(2):
            state = {}

            def start(h=h, state=state):
                m = jnp.where(kj == 0, neg_inf, run_max[h])
                state["mn"] = jnp.maximum(m, tile_max[h])
                state["alpha"] = jnp.exp2(m - state["mn"])

            def part(s, h=h, state=state):
                def run():
                    x = st_ref[h, s * KSUB_ATT:(s + 1) * KSUB_ATT, :] - state["mn"]
                    pb = jnp.exp2(x.astype(BF16))
                    c, r = divmod(s, sub_per_chunk)
                    pv = jnp.dot(vt_ref[0, kj * VT_CHUNKS + c, h * VT_ROWS:(h + 1) * VT_ROWS,
                                        r * KSUB_ATT:(r + 1) * KSUB_ATT],
                                 pb, preferred_element_type=F32)
                    state["pv"] = pv if s == 0 else state["pv"] + pv
                    if s == nsub - 1:
                        old = jnp.where(kj == 0, 0.0, acc_ref[h])
                        acc_ref[h] = state["alpha"] * old + state["pv"]
                        out[h] = state["mn"]
                return run
            steps += [start] + [part(s) for s in range(nsub)]
        return steps

    def interleave(mxu_steps, vpu_steps):
        n = max(len(mxu_steps), len(vpu_steps))
        for i in range(n):
            if i < len(mxu_steps):
                mxu_steps[i]()
            if i < len(vpu_steps):
                vpu_steps[i]()

    def body(jj, carry):
        run_max, max_even = carry
        t0 = 2 * jj
        mid, out, max_odd, max_next = {}, {}, {}, {}
        interleave(score_steps(t0 + 1, sto_ref, max_odd), softmax_steps(t0, ste_ref, run_max, max_even, mid))
        interleave(score_steps(t0 + 2, ste_ref, max_next),
                   softmax_steps(t0 + 1, sto_ref, (mid[0], mid[1]), max_odd, out))

        @pl.when((t0 + 1) % nkv == nkv - 1)
        def _():
            qi = t0 // nkv
            for h in range(2):
                acc = acc_ref[h]
                o_ref[0, qi, h * MLA_D_V:(h + 1) * MLA_D_V, :] = acc[0:MLA_D_V] / acc[MLA_D_V:MLA_D_V + 1]
        return (out[0], out[1]), (max_next[0], max_next[1])

    acc_ref[...] = jnp.zeros_like(acc_ref)
    max_first = {}
    for run in score_steps(0, ste_ref, max_first):
        run()
    lax.fori_loop(0, npair // 2, body, ((neg_inf, neg_inf), (max_first[0], max_first[1])))


def _attention(qt, k, vt, batch, seq):
    nq = seq // TQ_ATT
    nvt = seq // TM_PROJ
    assert (seq // TK_ATT) % 2 == 0, seq
    return pl.pallas_call(
        functools.partial(_attn_kernel, seq=seq),
        grid=(batch, MLA_HEADS // 2),
        in_specs=[pl.BlockSpec((1, nq, 2 * HEAD_PAD, TQ_ATT), lambda b, hp: (b, 0, hp, 0)),
                  pl.BlockSpec((seq, 2 * HEAD_PAD), lambda b, hp: (b, hp)),
                  pl.BlockSpec((1, nvt, 2 * VT_ROWS, TM_PROJ), lambda b, hp: (b, 0, hp, 0))],
        out_specs=pl.BlockSpec((1, nq, 2 * MLA_D_V, TQ_ATT), lambda b, hp: (b, 0, hp, 0)),
        out_shape=jax.ShapeDtypeStruct((batch, nq, MLA_HEADS * MLA_D_V, TQ_ATT), F32),
        scratch_shapes=[pltpu.VMEM((2, TK_ATT, TQ_ATT), F32),
                        pltpu.VMEM((2, TK_ATT, TQ_ATT), F32),
                        pltpu.VMEM((2, VT_ROWS, TQ_ATT), F32)],
        compiler_params=_cparams(("parallel", "parallel")),
        name="attention",
    )(qt, k, vt)


def _layer_norm(z, w, b):
    mu = jnp.mean(z, axis=-1, keepdims=True)
    zc = z - mu
    var = jnp.mean(zc * zc, axis=-1, keepdims=True)
    return zc * lax.rsqrt(var + LN_EPS) * w + b


def _split_bf16(a):
    hi = a.astype(BF16)
    lo = (a - hi.astype(F32)).astype(BF16)
    return hi, lo


def _outproj_kernel(x_ref, ret_ref, att_ref, wo1_ref, wo2_ref, onw_ref, lw_ref, lb_ref, wr_ref,
                    y_ref, lg_ref):
    wh, wl = _split_bf16(wr_ref[...])
    nt = (((1,), (1,)), ((), ()))
    nsub = x_ref.shape[0] // LANES
    pre = {}

    def project(r):
        rows = slice(r * LANES, (r + 1) * LANES)
        att = att_ref[0, 0, :, rows].T
        an = att * lax.rsqrt(jnp.mean(att * att, axis=-1, keepdims=True) + RMS_EPS) * onw_ref[...]
        mix = jnp.dot(ret_ref[rows, :], wo1_ref[...], preferred_element_type=F32)
        mix += jnp.dot(an.astype(BF16), wo2_ref[...], preferred_element_type=F32)
        pre[r] = DN_ALPHA * x_ref[rows, :] + mix

    def normalize(r):
        rows = slice(r * LANES, (r + 1) * LANES)
        y = _layer_norm(pre.pop(r), lw_ref[...], lb_ref[...])
        y_ref[rows, :] = y
        yh, yl = _split_bf16(y)
        lt = lax.dot_general(wh, yh, nt, preferred_element_type=F32)
        lt += lax.dot_general(wh, yl, nt, preferred_element_type=F32)
        lt += lax.dot_general(wl, yh, nt, preferred_element_type=F32)
        lg_ref[r] = lt

    for r in range(nsub + 1):
        if r < nsub:
            project(r)
        if r >= 1:
            normalize(r - 1)


def _outproj(x2d, ret, att_t, wo1, wo2, onw, lw, lb, wrt):
    t = x2d.shape[0]
    tm = TM_PROJ
    nseq = att_t.shape[1]
    assert att_t.shape[3] == tm
    row = lambda i: (i, 0)
    const = lambda i: (0, 0)
    return pl.pallas_call(
        _outproj_kernel,
        grid=(t // tm,),
        in_specs=[pl.BlockSpec((tm, D_MODEL), row),
                  pl.BlockSpec((tm, RET_WIDTH), row),
                  pl.BlockSpec((1, 1, RET_WIDTH, tm), lambda i: (i // nseq, i % nseq, 0, 0)),
                  pl.BlockSpec((RET_WIDTH, D_MODEL), const),
                  pl.BlockSpec((RET_WIDTH, D_MODEL), const),
                  pl.BlockSpec((1, RET_WIDTH), const),
                  pl.BlockSpec((1, D_MODEL), const),
                  pl.BlockSpec((1, D_MODEL), const),
                  pl.BlockSpec((N_EXPERTS, D_MODEL), const)],
        out_specs=[pl.BlockSpec((tm, D_MODEL), row),
                   pl.BlockSpec((tm // LANES, N_EXPERTS, LANES), lambda i: (i, 0, 0))],
        out_shape=[jax.ShapeDtypeStruct((t, D_MODEL), F32),
                   jax.ShapeDtypeStruct((t // LANES, N_EXPERTS, LANES), F32)],
        compiler_params=_cparams(("parallel",)),
        name="outproj",
    )(x2d, ret, att_t, wo1, wo2, onw, lw, lb, wrt)


def _route_kernel(lg_ref, rank_ref, gate_ref, cnt_ref, off_ref, blk_ref, boff_ref, *, cap, nb):
    e = N_EXPERTS
    ntiles = nb // BLOCKS_PER_TILE
    lg = lg_ref[...]
    ex = jnp.exp(lg - jnp.max(lg, axis=1, keepdims=True))
    aff = ex / jnp.sum(ex, axis=1, keepdims=True)
    gate_ref[...] = aff

    def count(mask):
        part = jnp.sum(mask.astype(F32), axis=0)
        return jnp.broadcast_to(jnp.sum(part, axis=1, keepdims=True), (e, LANES))

    def bisect(_, lohi):
        lo, hi = lohi
        mid = 0.5 * (lo + hi)
        ok = count(aff >= mid[None]) >= float(cap)
        return jnp.where(ok, mid, lo), jnp.where(ok, hi, mid)

    lo0 = jnp.zeros((e, LANES), F32)
    hi0 = jnp.full((e, LANES), 2.0, F32)
    lo, hi = lax.fori_loop(0, THRESHOLD_STEPS, bisect, (lo0, hi0))
    gt = aff >= hi[None]
    eq = jnp.logical_and(aff >= lo[None], jnp.logical_not(gt))
    need = float(cap) - count(gt)

    ii = lax.broadcasted_iota(jnp.int32, (LANES, LANES), 0)
    jj = lax.broadcasted_iota(jnp.int32, (LANES, LANES), 1)
    upper = (ii <= jj).astype(BF16)
    ones = jnp.ones((LANES, LANES), BF16)

    def block_scan(mask):
        m2 = mask.astype(BF16).reshape(nb * e, LANES)
        incl = jnp.dot(m2, upper, preferred_element_type=F32).reshape(nb, e, LANES)
        tot = jnp.dot(m2, ones, preferred_element_type=F32).reshape(nb, e, LANES)
        return incl, tot

    def leading_excl_scan(src_ref, dst_ref, n):
        def step(j, carry):
            dst_ref[j] = carry
            return carry + src_ref[j]
        return lax.fori_loop(0, n, step, jnp.zeros((e, LANES), F32))

    eqf = eq.astype(F32)
    incl, tot = block_scan(eq)
    blk_ref[...] = tot
    leading_excl_scan(blk_ref, boff_ref, nb)
    eq_before = incl - eqf + boff_ref[...]
    sel = jnp.logical_or(gt, jnp.logical_and(eq, eq_before < need[None]))

    self32 = sel.astype(F32)
    incl, tot = block_scan(sel)
    local = (incl - self32).reshape(ntiles, BLOCKS_PER_TILE, e, LANES)
    tot4 = tot.reshape(ntiles, BLOCKS_PER_TILE, e, LANES)
    ranks = [local[:, 0]]
    run = tot4[:, 0]
    for b in range(1, BLOCKS_PER_TILE):
        ranks.append(local[:, b] + run)
        run = run + tot4[:, b]
    rank = jnp.stack(ranks, axis=1).reshape(nb, e, LANES)
    rank_ref[...] = jnp.where(sel, rank, NOT_ROUTED)
    cnt_ref[...] = run
    blk_ref[0:ntiles] = run
    leading_excl_scan(blk_ref, boff_ref, ntiles)
    off_ref[...] = boff_ref[0:ntiles]


def _route(logits, cap):
    nb = logits.shape[0]
    ntiles = nb // BLOCKS_PER_TILE
    big = jax.ShapeDtypeStruct((nb, N_EXPERTS, LANES), F32)
    small = jax.ShapeDtypeStruct((ntiles, N_EXPERTS, LANES), F32)
    return pl.pallas_call(
        functools.partial(_route_kernel, cap=cap, nb=nb),
        out_shape=[big, big, small, small],
        scratch_shapes=[pltpu.VMEM((nb, N_EXPERTS, LANES), F32),
                        pltpu.VMEM((nb, N_EXPERTS, LANES), F32)],
        compiler_params=pltpu.CompilerParams(vmem_limit_bytes=VMEM_LIMIT),
        name="route",
    )(logits)


def _lead(off_ref, tile, ex):
    pos = off_ref[tile * N_EXPERTS + ex]
    lead = lax.rem(pos, ROW_ALIGN)
    return pos - lead, lead


def _overflow_chunks(lead, cnt):
    return lax.div(jnp.maximum(lead + cnt - SLAB_ROWS, 0) + (CHUNK_ROWS - 1), CHUNK_ROWS)


def _compact_kernel(off_ref, cnt_ref, y_ref, rank_ref, xg_ref, p_ref, pc_ref, st_ref, ost_ref,
                    carry_ref, zero_ref, sem, osem, *, cap, cap_rows):
    i = pl.program_id(0)
    nt = pl.num_programs(0)
    slot = i % 2

    def slab(tile, slt, ex, fn):
        start, lead = _lead(off_ref, tile, ex)
        used = lead + cnt_ref[tile * N_EXPERTS + ex]
        for first, rows in SLAB_PIECES:
            def piece(first=first, rows=rows):
                fn(pltpu.make_async_copy(
                    st_ref.at[slt, pl.ds(ex * SLAB_ROWS + first, rows)],
                    xg_ref.at[pl.ds(pl.multiple_of(ex * cap_rows + start + first, ROW_ALIGN), rows)],
                    sem.at[slt]))
            if first == 0:
                piece()
            else:
                pl.when(used > first)(piece)

    @pl.when(i == 0)
    def _():
        carry_ref[...] = jnp.zeros_like(carry_ref)
        zero_ref[...] = jnp.zeros_like(zero_ref)
        tails = [pltpu.make_async_copy(zero_ref, xg_ref.at[pl.ds(ex * cap_rows + cap, cap_rows - cap)],
                                       osem.at[0]) for ex in range(N_EXPERTS)]
        for cp in tails:
            cp.start()
        for cp in tails:
            cp.wait()

    xb = y_ref[...].astype(BF16)
    rank = jnp.concatenate([rank_ref[b] for b in range(BLOCKS_PER_TILE)], axis=1)
    jslab = lax.broadcasted_iota(jnp.int32, (SLAB_ROWS, TM_MOE), 0).astype(F32)
    jgrp = lax.broadcasted_iota(jnp.int32, (ROW_ALIGN, TM_MOE), 0).astype(F32)
    last_group = []
    for ex in range(N_EXPERTS):
        _, lead = _lead(off_ref, i, ex)
        sh = rank[ex:ex + 1, :] + lead.astype(F32)
        grp = lax.div(lead + cnt_ref[i * N_EXPERTS + ex], ROW_ALIGN)
        p_ref[ex * SLAB_ROWS:(ex + 1) * SLAB_ROWS, :] = (sh == jslab).astype(BF16)
        pc_ref[ex * ROW_ALIGN:(ex + 1) * ROW_ALIGN, :] = (
            sh - (grp * ROW_ALIGN).astype(F32) == jgrp).astype(BF16)
        last_group.append(grp)
    rows = jnp.dot(p_ref[...], xb, preferred_element_type=F32)
    tail = jnp.dot(pc_ref[...], xb, preferred_element_type=F32)
    for ex in range(N_EXPERTS):
        lo = ex * SLAB_ROWS
        old = carry_ref[ex]
        st_ref[slot, lo:lo + ROW_ALIGN, :] = (rows[lo:lo + ROW_ALIGN] + old).astype(BF16)
        st_ref[slot, lo + ROW_ALIGN:lo + SLAB_ROWS, :] = rows[lo + ROW_ALIGN:lo + SLAB_ROWS].astype(BF16)
        carry_ref[ex] = tail[ex * ROW_ALIGN:(ex + 1) * ROW_ALIGN] + jnp.where(last_group[ex] == 0, old, 0.0)

    @pl.when(i >= 1)
    def _():
        for ex in range(N_EXPERTS):
            slab(i - 1, 1 - slot, ex, lambda cp: cp.wait())

    for ex in range(N_EXPERTS):
        slab(i, slot, ex, lambda cp: cp.start())

    extra = [_overflow_chunks(_lead(off_ref, i, ex)[1], cnt_ref[i * N_EXPERTS + ex]) for ex in range(N_EXPERTS)]

    @pl.when(functools.reduce(lambda a, b: a + b, extra) > 0)
    def _():
        xo = y_ref[...].astype(BF16)
        jchunk = lax.broadcasted_iota(jnp.int32, (CHUNK_ROWS, TM_MOE), 0).astype(F32)
        for ex in range(N_EXPERTS):
            start, lead = _lead(off_ref, i, ex)
            sh = (jnp.concatenate([rank_ref[b, ex:ex + 1, :] for b in range(BLOCKS_PER_TILE)], axis=1)
                  + lead.astype(F32))

            def chunk(c, carry, ex=ex, start=start, sh=sh):
                first = SLAB_ROWS + c * CHUNK_ROWS
                pc = (sh == jchunk + first.astype(F32)).astype(BF16)
                ost_ref[...] = jnp.dot(pc, xo, preferred_element_type=F32).astype(BF16)
                cp = pltpu.make_async_copy(
                    ost_ref,
                    xg_ref.at[pl.ds(pl.multiple_of(ex * cap_rows + start + first, ROW_ALIGN), CHUNK_ROWS)],
                    osem.at[0])
                cp.start()
                cp.wait()
                return carry

            lax.fori_loop(0, extra[ex], chunk, 0)

    @pl.when(i == nt - 1)
    def _():
        for ex in range(N_EXPERTS):
            slab(i, slot, ex, lambda cp: cp.wait())


def _compact(off, cnt, y1, rank_em, cap, cap_rows):
    n = y1.shape[0]
    ntiles = n // TM_MOE
    gs = pltpu.PrefetchScalarGridSpec(
        num_scalar_prefetch=2,
        grid=(ntiles,),
        in_specs=[pl.BlockSpec((TM_MOE, D_MODEL), lambda i, o, c: (i, 0)),
                  pl.BlockSpec((BLOCKS_PER_TILE, N_EXPERTS, LANES), lambda i, o, c: (i, 0, 0))],
        out_specs=pl.BlockSpec(memory_space=pl.ANY),
        scratch_shapes=[pltpu.VMEM((N_EXPERTS * SLAB_ROWS, TM_MOE), BF16),
                        pltpu.VMEM((N_EXPERTS * ROW_ALIGN, TM_MOE), BF16),
                        pltpu.VMEM((2, N_EXPERTS * SLAB_ROWS, D_MODEL), BF16),
                        pltpu.VMEM((CHUNK_ROWS, D_MODEL), BF16),
                        pltpu.VMEM((N_EXPERTS, ROW_ALIGN, D_MODEL), F32),
                        pltpu.VMEM((cap_rows - cap, D_MODEL), BF16),
                        pltpu.SemaphoreType.DMA((2,)),
                        pltpu.SemaphoreType.DMA((1,))])
    return pl.pallas_call(
        functools.partial(_compact_kernel, cap=cap, cap_rows=cap_rows),
        grid_spec=gs,
        out_shape=jax.ShapeDtypeStruct((N_EXPERTS * cap_rows, D_MODEL), BF16),
        compiler_params=_cparams(("arbitrary",)),
        name="compact",
    )(off, cnt, y1, rank_em)


def _ffn_kernel(x_ref, wg_ref, wu_ref, wd_ref, y_ref, wgb_ref, wub_ref, wdb_ref, *, nvalid):
    j = pl.program_id(1)

    def swiglu(wg, wu, wd):
        x = x_ref[...]
        hg = jnp.dot(x, wg, preferred_element_type=F32)
        hu = jnp.dot(x, wu, preferred_element_type=F32)
        h = (hg / (1.0 + jnp.exp(-hg)) * hu).astype(BF16)
        y_ref[...] = jnp.dot(h, wd, preferred_element_type=F32).astype(BF16)

    @pl.when(j == 0)
    def _():
        wg = wg_ref[0].astype(BF16)
        wu = wu_ref[0].astype(BF16)
        wd = wd_ref[0].astype(BF16)
        wgb_ref[...] = wg
        wub_ref[...] = wu
        wdb_ref[...] = wd
        swiglu(wg, wu, wd)

    @pl.when(jnp.logical_and(j > 0, j < nvalid))
    def _():
        swiglu(wgb_ref[...], wub_ref[...], wdb_ref[...])

    @pl.when(j >= nvalid)
    def _():
        y_ref[...] = jnp.zeros_like(y_ref)


def _ffn(xg, wg, wu, wd, cap, cap_rows):
    nvalid = cap // TM_FFN
    nt = cap_rows // TM_FFN
    wmap = lambda ex, j: (ex, 0, 0)
    return pl.pallas_call(
        functools.partial(_ffn_kernel, nvalid=nvalid),
        grid=(N_EXPERTS, nt),
        in_specs=[pl.BlockSpec((TM_FFN, D_MODEL), lambda ex, j: (ex * nt + jnp.minimum(j, nvalid - 1), 0)),
                  pl.BlockSpec((1, D_MODEL, D_MODEL), wmap),
                  pl.BlockSpec((1, D_MODEL, D_MODEL), wmap),
                  pl.BlockSpec((1, D_MODEL, D_MODEL), wmap)],
        out_specs=pl.BlockSpec((TM_FFN, D_MODEL), lambda ex, j: (ex * nt + j, 0)),
        out_shape=jax.ShapeDtypeStruct((N_EXPERTS * cap_rows, D_MODEL), BF16),
        scratch_shapes=[pltpu.VMEM((D_MODEL, D_MODEL), BF16)] * 3,
        compiler_params=_cparams(("arbitrary", "arbitrary")),
        name="ffn",
    )(xg, wg, wu, wd)


def _combine_kernel(off_ref, cnt_ref, y1_ref, rank_ref, gate_ref, yh_ref, lw_ref, lb_ref,
                    o_ref, ybuf, obuf, acc_ref, sem, osem, *, cap_rows):
    i = pl.program_id(0)
    nt = pl.num_programs(0)
    slot = i % 2

    ncol = N_EXPERTS * SLAB_ROWS

    def fetch(tile, slt, fn):
        for ex in range(N_EXPERTS):
            start, lead = _lead(off_ref, tile, ex)
            used = lead + cnt_ref[tile * N_EXPERTS + ex]
            for first, rows in SLAB_PIECES:
                def piece(ex=ex, start=start, first=first, rows=rows):
                    fn(pltpu.make_async_copy(
                        yh_ref.at[pl.ds(pl.multiple_of(ex * cap_rows + start + first, ROW_ALIGN), rows)],
                        ybuf.at[slt, pl.ds(ex * SLAB_ROWS + first, rows)],
                        sem.at[slt]))
                if first == 0:
                    piece()
                else:
                    pl.when(used > first)(piece)

    @pl.when(i == 0)
    def _():
        ybuf[...] = jnp.zeros_like(ybuf)
        fetch(0, 0, lambda cp: cp.start())

    @pl.when(i + 1 < nt)
    def _():
        fetch(i + 1, 1 - slot, lambda cp: cp.start())

    rank = rank_ref[...]
    gate = gate_ref[...]
    er = lax.broadcasted_iota(jnp.int32, (N_EXPERTS, ncol), 0)
    ec = lax.broadcasted_iota(jnp.int32, (N_EXPERTS, ncol), 1)
    expand = jnp.logical_and(ec >= er * SLAB_ROWS, ec < (er + 1) * SLAB_ROWS).astype(BF16)
    rexp = jnp.dot(rank.astype(BF16), expand, preferred_element_type=F32)
    gexp = jnp.dot(gate.astype(BF16), expand, preferred_element_type=F32)
    col = lax.broadcasted_iota(jnp.int32, (1, ncol), 1)
    want = jnp.zeros((1, ncol), F32)
    for ex in range(N_EXPERTS):
        _, lead = _lead(off_ref, i, ex)
        inside = jnp.logical_and(col >= ex * SLAB_ROWS, col < (ex + 1) * SLAB_ROWS)
        want = jnp.where(inside, (col - ex * SLAB_ROWS - lead).astype(F32), want)
    gmat = jnp.where(rexp == want, gexp, 0.0).astype(BF16)

    fetch(i, slot, lambda cp: cp.wait())
    extra = [_overflow_chunks(_lead(off_ref, i, ex)[1], cnt_ref[i * N_EXPERTS + ex]) for ex in range(N_EXPERTS)]
    any_extra = functools.reduce(lambda a, b: a + b, extra)

    @pl.when(any_extra == 0)
    def _():
        half = TM_MOE // 2
        mix = [jnp.dot(gmat[r * half:(r + 1) * half], ybuf[slot], preferred_element_type=F32) for r in range(2)]
        for r in range(2):
            rows = slice(r * half, (r + 1) * half)
            o_ref[rows, :] = _layer_norm(DN_ALPHA * y1_ref[rows, :] + mix[r], lw_ref[...], lb_ref[...])

    @pl.when(any_extra > 0)
    def _():
        acc_ref[...] = jnp.dot(gmat, ybuf[slot], preferred_element_type=F32)
        jc = lax.broadcasted_iota(jnp.int32, (TM_MOE, CHUNK_ROWS), 1).astype(F32)
        for ex in range(N_EXPERTS):
            start, lead = _lead(off_ref, i, ex)

            def chunk(c, carry, ex=ex, start=start, lead=lead):
                first = SLAB_ROWS + c * CHUNK_ROWS
                cp = pltpu.make_async_copy(
                    yh_ref.at[pl.ds(pl.multiple_of(ex * cap_rows + start + first, ROW_ALIGN), CHUNK_ROWS)],
                    obuf, osem.at[0])
                cp.start()
                cp.wait()
                gc = jnp.where(rank[:, ex:ex + 1] + lead.astype(F32) == jc + first.astype(F32),
                               gate[:, ex:ex + 1].astype(BF16).astype(F32), 0.0).astype(BF16)
                acc_ref[...] += jnp.dot(gc, obuf[...], preferred_element_type=F32)
                return carry

            lax.fori_loop(0, extra[ex], chunk, 0)
        o_ref[...] = _layer_norm(DN_ALPHA * y1_ref[...] + acc_ref[...], lw_ref[...], lb_ref[...])


def _combine(off, cnt, y1, rank_tm, gate_tm, yh, lw, lb, cap_rows):
    n = y1.shape[0]
    ntiles = n // TM_MOE
    row = lambda i, o, c: (i, 0)
    const = lambda i, o, c: (0, 0)
    gs = pltpu.PrefetchScalarGridSpec(
        num_scalar_prefetch=2,
        grid=(ntiles,),
        in_specs=[pl.BlockSpec((TM_MOE, D_MODEL), row),
                  pl.BlockSpec((TM_MOE, N_EXPERTS), row),
                  pl.BlockSpec((TM_MOE, N_EXPERTS), row),
                  pl.BlockSpec(memory_space=pl.ANY),
                  pl.BlockSpec((1, D_MODEL), const),
                  pl.BlockSpec((1, D_MODEL), const)],
        out_specs=pl.BlockSpec((TM_MOE, D_MODEL), row),
        scratch_shapes=[pltpu.VMEM((2, N_EXPERTS * SLAB_ROWS, D_MODEL), BF16),
                        pltpu.VMEM((CHUNK_ROWS, D_MODEL), BF16),
                        pltpu.VMEM((TM_MOE, D_MODEL), F32),
                        pltpu.SemaphoreType.DMA((2,)),
                        pltpu.SemaphoreType.DMA((1,))])
    return pl.pallas_call(
        functools.partial(_combine_kernel, cap_rows=cap_rows),
        grid_spec=gs,
        out_shape=jax.ShapeDtypeStruct((n, D_MODEL), F32),
        compiler_params=_cparams(("arbitrary",)),
        name="combine",
    )(off, cnt, y1, rank_tm, gate_tm, yh, lw, lb)


def _rope_tables(seq):
    pos = jnp.arange(seq, dtype=F32)[:, None]
    inv_m = 1.0 / (ROPE_BASE ** (jnp.arange(0, MLA_D_ROPE, 2, dtype=F32) / MLA_D_ROPE))
    am = pos * inv_m[None, :]
    one = jnp.ones((seq, MLA_D_NOPE), F32)
    zero_n = jnp.zeros((seq, MLA_D_NOPE), F32)
    zero_p = jnp.zeros((seq, HEAD_PAD - MLA_D_NOPE - MLA_D_ROPE), F32)
    cm = jnp.concatenate([one, jnp.cos(am), jnp.cos(am), zero_p], axis=1)
    sm = jnp.concatenate([zero_n, jnp.sin(am), jnp.sin(am), zero_p], axis=1)
    inv_r = 1.0 / (ROPE_BASE ** (jnp.arange(0, RET_DK, 2, dtype=F32) / RET_DK))
    ar = pos * inv_r[None, :]
    cosr = jnp.concatenate([jnp.cos(ar), jnp.cos(ar)], axis=1)
    sinr = jnp.concatenate([-jnp.sin(ar), jnp.sin(ar)], axis=1)
    return cm, sm, cosr, sinr


def _prep_weights(w_in, mla_w_uq, mla_w_ukv):
    half = MLA_D_ROPE // 2
    base = 4 * RET_WIDTH + MLA_Q_LORA + MLA_KV_LORA
    w_kr = w_in[:, base:base + MLA_D_ROPE]
    zn = jnp.zeros((D_MODEL, MLA_D_NOPE), F32)
    zp = jnp.zeros((D_MODEL, HEAD_PAD - MLA_D_NOPE - MLA_D_ROPE), F32)
    kr_p = jnp.concatenate([zn, w_kr, zp], axis=1)
    kr_rot = jnp.concatenate([zn, -w_kr[:, half:], w_kr[:, :half], zp], axis=1)
    w_all = jnp.concatenate([w_in[:, :base], kr_p, kr_rot], axis=1).astype(BF16)

    wq = mla_w_uq.reshape(MLA_Q_LORA, MLA_HEADS, MLA_D_NOPE + MLA_D_ROPE)
    nope, ropew = wq[..., :MLA_D_NOPE], wq[..., MLA_D_NOPE:]
    zq = jnp.zeros((MLA_Q_LORA, MLA_HEADS, HEAD_PAD - MLA_D_NOPE - MLA_D_ROPE), F32)
    q_p = jnp.concatenate([nope, ropew, zq], axis=-1).reshape(MLA_Q_LORA, -1)
    q_rot = jnp.concatenate([jnp.zeros_like(nope), -ropew[..., half:], ropew[..., :half], zq],
                            axis=-1).reshape(MLA_Q_LORA, -1)
    wq2 = jnp.concatenate([q_p, q_rot], axis=1).astype(BF16)

    wkv3 = mla_w_ukv.reshape(MLA_KV_LORA, MLA_HEADS, MLA_D_NOPE + MLA_D_V)
    k_p = jnp.concatenate([wkv3[..., :MLA_D_NOPE],
                           jnp.zeros((MLA_KV_LORA, MLA_HEADS, HEAD_PAD - MLA_D_NOPE), F32)],
                          axis=-1).reshape(MLA_KV_LORA, -1)
    v_w = wkv3[..., MLA_D_NOPE:].reshape(MLA_KV_LORA, -1)
    wkv = jnp.concatenate([k_p, v_w], axis=1).astype(BF16)
    return w_all, wq2, wkv


def _mixer_ln(x, wts):
    batch, seq, _ = x.shape
    x2d = x.reshape(batch * seq, D_MODEL)
    cm, sm, cosr, sinr = _rope_tables(seq)
    rq, rk, rv, rg, q, k, v = _inproj(x2d, seq, wts["w_all"], wts["wq2"], wts["wkv"],
                                      wts["qnw"], wts["kvnw"], cm, sm)
    ret = _retention(rq, rk, rv, rg, cosr, sinr, wts["dfw"], wts["dbw"], wts["gnw"], batch, seq)
    att = _attention(q, k, v, batch, seq)
    return _outproj(x2d, ret, att, wts["wo1"], wts["wo2"], wts["onw"], wts["ln1w"], wts["ln1b"],
                    wts["wrt"])


def _moe_ln(y1, logits, wts):
    n = y1.shape[0]
    cap = EC_CAPACITY_FACTOR * n // N_EXPERTS
    assert cap % TM_FFN == 0, (n, cap)
    cap_rows = cap + TM_FFN
    rank_em, gate_em, cnt_rep, off_rep = _route(logits, cap)
    cnt = cnt_rep[:, :, 0].astype(jnp.int32).reshape(-1)
    off = off_rep[:, :, 0].astype(jnp.int32).reshape(-1)
    to_tm = lambda a: jnp.transpose(a, (0, 2, 1)).reshape(n, N_EXPERTS)
    xg = _compact(off, cnt, y1, rank_em, cap, cap_rows)
    yh = _ffn(xg, wts["wg"], wts["wu"], wts["wd"], cap, cap_rows)
    return _combine(off, cnt, y1, to_tm(rank_em), to_tm(gate_em), yh, wts["ln2w"], wts["ln2b"],
                    cap_rows)


def _layer(x, wts):
    batch, seq, _ = x.shape
    y1, logits = _mixer_ln(x, wts)
    return _moe_ln(y1, logits, wts).reshape(batch, seq, D_MODEL)


def kernel(x_prompt, x_sample, w_in, ret_decay_fwd, ret_decay_bwd, ret_gn_w, mla_q_norm_w, mla_w_uq,
           mla_kv_norm_w, mla_w_ukv, mla_out_norm_w, w_out, ln1_w, ln1_b, w_router, w_gate, w_up,
           w_down, ln2_w, ln2_b):
    depth = w_in.shape[0]
    y_prompt, y_sample = x_prompt, x_sample
    for l in range(depth):
        w_all, wq2, wkv = _prep_weights(w_in[l], mla_w_uq[l], mla_w_ukv[l])
        rep = lambda a: jnp.broadcast_to(a.astype(F32)[:, None, None], (RET_HEADS, 1, LANES))
        wts = dict(
            w_all=w_all, wq2=wq2, wkv=wkv,
            qnw=mla_q_norm_w[l][None, :], kvnw=mla_kv_norm_w[l][None, :],
            dfw=rep(ret_decay_fwd[l]), dbw=rep(ret_decay_bwd[l]), gnw=ret_gn_w[l][None, :],
            wo1=w_out[l][:RET_WIDTH].astype(BF16), wo2=w_out[l][RET_WIDTH:].astype(BF16),
            onw=mla_out_norm_w[l][None, :], ln1w=ln1_w[l][None, :], ln1b=ln1_b[l][None, :],
            wrt=w_router[l].T,
            wg=w_gate[l], wu=w_up[l], wd=w_down[l],
            ln2w=ln2_w[l][None, :], ln2b=ln2_b[l][None, :])
        y_prompt = _layer(y_prompt, wts)
        y_sample = _layer(y_sample, wts)
    return (y_prompt, y_sample)
```

```python
import functools
import math

import numpy as np
import jax
import jax.numpy as jnp
from jax import lax
from jax.experimental import pallas as pl
from jax.experimental.pallas import tpu as pltpu

F32 = jnp.float32
BF16 = jnp.bfloat16

D_MODEL = 1024
RET_WIDTH = 512
RET_HEADS = 4
RET_DK = 128
RET_CHUNK = 128
MLA_HEADS = 8
MLA_D_NOPE = 64
MLA_D_ROPE = 32
MLA_D_V = 64
MLA_Q_LORA = 256
MLA_KV_LORA = 128
N_EXPERTS = 16
EC_CAPACITY_FACTOR = 2
ROPE_BASE = 10000.0
LN_EPS = 1e-5
RMS_EPS = 1e-6
DN_ALPHA = 2.0 ** 0.25

LANES = 128
HEAD_PAD = 128
PROJ_COLS = 4 * RET_WIDTH + MLA_Q_LORA + MLA_KV_LORA + 2 * LANES
VMEM_LIMIT = 56 * 1024 * 1024

TM_PROJ = 512
RET_STEP_CHUNKS = 16
TQ_ATT = 512
VT_CHUNKS = 2
VT_ROWS = 80
TK_ATT = VT_CHUNKS * TM_PROJ
KSUB_ATT = 256
TM_MOE = 256
ROW_ALIGN = 16
CHUNK_ROWS = 64
SLAB_ROWS = ROW_ALIGN + CHUNK_ROWS
SLAB_PIECES = ((0, 48), (48, 32))
THRESHOLD_STEPS = 64
NOT_ROUTED = -512.0
TM_FFN = 512
BLOCKS_PER_TILE = TM_MOE // LANES

_QK_SCALE = (MLA_D_NOPE + MLA_D_ROPE) ** -0.5 * math.log2(math.e)


def _cparams(sem):
    return pltpu.CompilerParams(dimension_semantics=sem, vmem_limit_bytes=VMEM_LIMIT)


def _inproj_kernel(x_ref, w_ref, wq_ref, wkv_ref, qnw_ref, kvnw_ref, cm_ref, sm_ref,
                   rq_ref, rk_ref, rv_ref, rg_ref, qt_ref, k_ref, vt_ref):
    xb = x_ref[...].astype(BF16)
    proj = jnp.dot(xb, w_ref[...], preferred_element_type=F32)
    rq_ref[...] = proj[:, 0:512]
    rk_ref[...] = proj[:, 512:1024]
    rv_ref[...] = proj[:, 1024:1536]
    rg_ref[...] = proj[:, 1536:2048]
    cm = cm_ref[...]
    sm = sm_ref[...]

    cq = proj[:, 2048:2304]
    cqn = cq * lax.rsqrt(jnp.mean(cq * cq, axis=-1, keepdims=True) + RMS_EPS) * qnw_ref[...]
    qq = jnp.dot(cqn.astype(BF16), wq_ref[...], preferred_element_type=F32)

    ckv = proj[:, 2304:2432]
    ckvn = ckv * lax.rsqrt(jnp.mean(ckv * ckv, axis=-1, keepdims=True) + RMS_EPS) * kvnw_ref[...]
    kv = jnp.dot(ckvn.astype(BF16), wkv_ref[...], preferred_element_type=F32)

    krope = proj[:, 2432:2560] * cm + proj[:, 2560:2688] * sm
    for h in range(MLA_HEADS):
        lo = h * HEAD_PAD
        qh = qq[:, lo:lo + HEAD_PAD] * cm + qq[:, 1024 + lo:1024 + lo + HEAD_PAD] * sm
        qt_ref[0, 0, lo:lo + HEAD_PAD, :] = (qh * _QK_SCALE).T.astype(BF16)
        k_ref[:, lo:lo + HEAD_PAD] = (kv[:, lo:lo + HEAD_PAD] + krope).astype(BF16)
    vt = kv[:, 1024:1536].T.astype(BF16)
    pad_rows = VT_ROWS - MLA_D_V
    ones_row = (lax.broadcasted_iota(jnp.int32, (pad_rows, vt.shape[1]), 0) == 0).astype(BF16)
    for h in range(MLA_HEADS):
        vt_ref[0, 0, h * VT_ROWS:h * VT_ROWS + MLA_D_V, :] = vt[h * MLA_D_V:(h + 1) * MLA_D_V, :]
        vt_ref[0, 0, h * VT_ROWS + MLA_D_V:(h + 1) * VT_ROWS, :] = ones_row


def _inproj(x2d, seq, w_all, wq2, wkv, qnw, kvnw, cm, sm):
    t = x2d.shape[0]
    tm = TM_PROJ
    nseq = seq // tm
    row = lambda i: (i, 0)
    const = lambda i: (0, 0)
    tab = lambda i: (i % nseq, 0)
    outs = [jax.ShapeDtypeStruct((t, RET_WIDTH), F32)] * 4 + [
        jax.ShapeDtypeStruct((t // seq, nseq, MLA_HEADS * HEAD_PAD, tm), BF16),
        jax.ShapeDtypeStruct((t, MLA_HEADS * HEAD_PAD), BF16),
        jax.ShapeDtypeStruct((t // seq, nseq, MLA_HEADS * VT_ROWS, tm), BF16)]
    return pl.pallas_call(
        _inproj_kernel,
        grid=(t // tm,),
        in_specs=[pl.BlockSpec((tm, D_MODEL), row),
                  pl.BlockSpec((D_MODEL, PROJ_COLS), const),
                  pl.BlockSpec((MLA_Q_LORA, 2 * MLA_HEADS * HEAD_PAD), const),
                  pl.BlockSpec((MLA_KV_LORA, MLA_HEADS * (HEAD_PAD + MLA_D_V)), const),
                  pl.BlockSpec((1, MLA_Q_LORA), const),
                  pl.BlockSpec((1, MLA_KV_LORA), const),
                  pl.BlockSpec((tm, LANES), tab),
                  pl.BlockSpec((tm, LANES), tab)],
        out_specs=[pl.BlockSpec((tm, RET_WIDTH), row)] * 4 + [
            pl.BlockSpec((1, 1, MLA_HEADS * HEAD_PAD, tm), lambda i: (i // nseq, i % nseq, 0, 0)),
            pl.BlockSpec((tm, MLA_HEADS * HEAD_PAD), row),
            pl.BlockSpec((1, 1, MLA_HEADS * VT_ROWS, tm), lambda i: (i // nseq, i % nseq, 0, 0))],
        out_shape=outs,
        compiler_params=_cparams(("parallel",)),
        name="inproj",
    )(x2d, w_all, wq2, wkv, qnw, kvnw, cm, sm)


def _log_sigmoid(x):
    return jnp.minimum(x, 0.0) - jnp.log(1.0 + jnp.exp(-jnp.abs(x)))


def _ret_kernel(q_ref, k_ref, v_ref, g_ref, cos_ref, sin_ref, df_ref, db_ref, gnw_ref,
                o_ref, sf_ref, sb_ref, sbst_ref, kvf_ref, kb_ref, vb_ref, *, nsteps):
    phase = pl.program_id(2)
    step = pl.program_id(3)
    c = RET_CHUNK
    lgf = _log_sigmoid(df_ref[0])
    lgb = _log_sigmoid(db_ref[0])
    rpos = lax.broadcasted_iota(jnp.int32, (c, c), 0).astype(F32)
    cpos = lax.broadcasted_iota(jnp.int32, (c, c), 1).astype(F32)
    tn = (((0,), (0,)), ((), ()))

    def rope(x, cs, sn):
        return x * cs + pltpu.roll(x, RET_DK // 2, axis=1) * sn

    @pl.when(jnp.logical_and(phase == 0, step == 0))
    def _():
        sb_ref[...] = jnp.zeros_like(sb_ref)

    @pl.when(jnp.logical_and(phase == 1, step == 0))
    def _():
        sf_ref[...] = jnp.zeros_like(sf_ref)

    @pl.when(phase == 0)
    def _():
        kwb = jnp.exp(lgb * rpos)
        kwf = jnp.exp(lgf * (float(c - 1) - rpos))
        cdb = jnp.exp(lgb * float(c))
        kvstep = nsteps - 1 - step
        kvs = {}
        for g in range(RET_STEP_CHUNKS):
            rows = slice(g * c, (g + 1) * c)
            idx = kvstep * RET_STEP_CHUNKS + g
            kr = rope(k_ref[rows, :], cos_ref[rows, :], sin_ref[rows, :]) * (RET_DK ** -0.5)
            vb = v_ref[rows, :].astype(BF16)
            kb_ref[idx] = kr.astype(BF16)
            vb_ref[idx] = vb
            kw = jnp.concatenate([(kr * kwb).astype(BF16), (kr * kwf).astype(BF16)], axis=1)
            kvs[g] = lax.dot_general(kw, vb, tn, preferred_element_type=F32)
        sb = sb_ref[...]
        for g in range(RET_STEP_CHUNKS - 1, -1, -1):
            idx = kvstep * RET_STEP_CHUNKS + g
            sbst_ref[idx] = sb.astype(BF16)
            kvf_ref[idx] = kvs[g][c:2 * c]
            sb = cdb * sb + kvs[g][0:c]
        sb_ref[...] = sb

    @pl.when(phase == 1)
    def _():
        diff = rpos - cpos
        dmat = jnp.where(diff >= 0, jnp.exp(lgf * jnp.maximum(diff, 0.0)),
                         jnp.exp(lgb * jnp.maximum(-diff, 0.0)))
        qwf = jnp.exp(lgf * (rpos + 1.0))
        qwb = jnp.exp(lgb * (float(c) - rpos))
        cdf = jnp.exp(lgf * float(c))
        gnw = gnw_ref[...]
        qrs, scs, outs = {}, {}, {}
        sfs = {0: sf_ref[...]}

        def scores(g):
            rows = slice(g * c, (g + 1) * c)
            idx = step * RET_STEP_CHUNKS + g
            qrs[g] = rope(q_ref[rows, :], cos_ref[rows, :], sin_ref[rows, :])
            scs[g] = lax.dot_general(qrs[g].astype(BF16), kb_ref[idx], (((1,), (1,)), ((), ())),
                                     preferred_element_type=F32)
            sfs[g + 1] = cdf * sfs[g] + kvf_ref[idx]

        def mix(g):
            idx = step * RET_STEP_CHUNKS + g
            qr = qrs.pop(g)
            lhs = jnp.concatenate([(scs.pop(g) * dmat).astype(BF16), (qr * qwf).astype(BF16),
                                   (qr * qwb).astype(BF16)], axis=1)
            rhs = jnp.concatenate([vb_ref[idx], sfs[g].astype(BF16), sbst_ref[idx]], axis=0)
            outs[g] = jnp.dot(lhs, rhs, preferred_element_type=F32)

        def finish(g):
            rows = slice(g * c, (g + 1) * c)
            o = outs.pop(g)
            mu = jnp.mean(o, axis=-1, keepdims=True)
            oc = o - mu
            var = jnp.mean(oc * oc, axis=-1, keepdims=True)
            on = oc * lax.rsqrt(var + LN_EPS) * gnw
            gt = g_ref[rows, :]
            o_ref[rows, :] = (gt / (1.0 + jnp.exp(-gt)) * on).astype(o_ref.dtype)

        for s in range(RET_STEP_CHUNKS + 4):
            if s - 4 >= 0:
                finish(s - 4)
            if 0 <= s - 2 < RET_STEP_CHUNKS:
                mix(s - 2)
            if s < RET_STEP_CHUNKS:
                scores(s)
        sf_ref[...] = sfs[RET_STEP_CHUNKS]


def _retention(rq, rk, rv, rg, cosr, sinr, dfw, dbw, gnw, batch, seq):
    t = rq.shape[0]
    ts = RET_STEP_CHUNKS * RET_CHUNK
    nsteps = seq // ts
    nchunks = seq // RET_CHUNK
    kvi = lambda p, i: jnp.where(p == 0, nsteps - 1 - i, 0)
    qi = lambda p, i: jnp.where(p == 0, 0, i)
    kv_map = lambda b, h, p, i: (b * nsteps + kvi(p, i), h)
    q_map = lambda b, h, p, i: (b * nsteps + qi(p, i), h)
    tab_map = lambda b, h, p, i: (jnp.where(p == 0, nsteps - 1 - i, i), 0)
    head3 = lambda b, h, p, i: (h, 0, 0)
    return pl.pallas_call(
        functools.partial(_ret_kernel, nsteps=nsteps),
        grid=(batch, RET_HEADS, 2, nsteps),
        in_specs=[pl.BlockSpec((ts, RET_DK), q_map),
                  pl.BlockSpec((ts, RET_DK), kv_map),
                  pl.BlockSpec((ts, RET_DK), kv_map),
                  pl.BlockSpec((ts, RET_DK), q_map),
                  pl.BlockSpec((ts, RET_DK), tab_map),
                  pl.BlockSpec((ts, RET_DK), tab_map),
                  pl.BlockSpec((1, 1, LANES), head3),
                  pl.BlockSpec((1, 1, LANES), head3),
                  pl.BlockSpec((1, RET_DK), lambda b, h, p, i: (0, h))],
        out_specs=pl.BlockSpec((ts, RET_DK), q_map),
        out_shape=jax.ShapeDtypeStruct((t, RET_WIDTH), BF16),
        scratch_shapes=[pltpu.VMEM((RET_DK, RET_DK), F32),
                        pltpu.VMEM((RET_DK, RET_DK), F32),
                        pltpu.VMEM((nchunks, RET_DK, RET_DK), BF16),
                        pltpu.VMEM((nchunks, RET_DK, RET_DK), F32),
                        pltpu.VMEM((nchunks, RET_CHUNK, RET_DK), BF16),
                        pltpu.VMEM((nchunks, RET_CHUNK, RET_DK), BF16)],
        compiler_params=_cparams(("parallel", "parallel", "arbitrary", "arbitrary")),
        name="retention",
    )(rq, rk, rv, rg, cosr, sinr, dfw, dbw, gnw)


def _attn_kernel(qt_ref, k_ref, vt_ref, o_ref, ste_ref, sto_ref, acc_ref, *, seq):
    nkv = seq // TK_ATT
    npair = (seq // TQ_ATT) * nkv
    nt = (((1,), (1,)), ((), ()))
    nsub = TK_ATT // KSUB_ATT
    sub_per_chunk = TM_PROJ // KSUB_ATT
    neg_inf = jnp.full((1, TQ_ATT), -jnp.inf, F32)

    def score_steps(t, st_ref, tile_max):
        t = jnp.minimum(t, npair - 1)
        qi = t // nkv
        ks = pl.multiple_of((t % nkv) * TK_ATT, TK_ATT)

        def step(h, s):
            def run():
                hs = slice(h * HEAD_PAD, (h + 1) * HEAD_PAD)
                sub = slice(s * KSUB_ATT, (s + 1) * KSUB_ATT)
                st_ref[h, sub, :] = jnp.dot(k_ref[pl.ds(ks + s * KSUB_ATT, KSUB_ATT), hs], qt_ref[0, qi, hs, :],
                                            preferred_element_type=F32)
                top = jnp.max(st_ref[h, sub, :], axis=0, keepdims=True)
                tile_max[h] = top if s == 0 else jnp.maximum(tile_max[h], top)
            return run
        return [step(h, s) for h in range(2) for s in range(nsub)]

    def softmax_steps(t, st_ref, run_max, tile_max, out):
        kj = t % nkv
        steps = []
        for h in range(2):
            state = {}

            def start(h=h, state=state):
                m = jnp.where(kj == 0, neg_inf, run_max[h])
                state["mn"] = jnp.maximum(m, tile_max[h])
                state["alpha"] = jnp.exp2(m - state["mn"])

            def part(s, h=h, state=state):
                def run():
                    x = st_ref[h, s * KSUB_ATT:(s + 1) * KSUB_ATT, :] - state["mn"]
                    pb = jnp.exp2(x.astype(BF16))
                    c, r = divmod(s, sub_per_chunk)
                    pv = jnp.dot(vt_ref[0, kj * VT_CHUNKS + c, h * VT_ROWS:(h + 1) * VT_ROWS,
                                        r * KSUB_ATT:(r + 1) * KSUB_ATT],
                                 pb, preferred_element_type=F32)
                    state["pv"] = pv if s == 0 else state["pv"] + pv
                    if s == nsub - 1:
                        old = jnp.where(kj == 0, 0.0, acc_ref[h])
                        acc_ref[h] = state["alpha"] * old + state["pv"]
                        out[h] = state["mn"]
                return run
            steps += [start] + [part(s) for s in range(nsub)]
        return steps

    def interleave(mxu_steps, vpu_steps):
        n = max(len(mxu_steps), len(vpu_steps))
        for i in range(n):
            if i < len(mxu_steps):
                mxu_steps[i]()
            if i < len(vpu_steps):
                vpu_steps[i]()

    def body(jj, carry):
        run_max, max_even = carry
        t0 = 2 * jj
        mid, out, max_odd, max_next = {}, {}, {}, {}
        interleave(score_steps(t0 + 1, sto_ref, max_odd), softmax_steps(t0, ste_ref, run_max, max_even, mid))
        interleave(score_steps(t0 + 2, ste_ref, max_next),
                   softmax_steps(t0 + 1, sto_ref, (mid[0], mid[1]), max_odd, out))

        @pl.when((t0 + 1) % nkv == nkv - 1)
        def _():
            qi = t0 // nkv
            for h in range(2):
                acc = acc_ref[h]
                o_ref[0, qi, h * MLA_D_V:(h + 1) * MLA_D_V, :] = acc[0:MLA_D_V] / acc[MLA_D_V:MLA_D_V + 1]
        return (out[0], out[1]), (max_next[0], max_next[1])

    acc_ref[...] = jnp.zeros_like(acc_ref)
    max_first = {}
    for run in score_steps(0, ste_ref, max_first):
        run()
    lax.fori_loop(0, npair // 2, body, ((neg_inf, neg_inf), (max_first[0], max_first[1])))


def _attention(qt, k, vt, batch, seq):
    nq = seq // TQ_ATT
    nvt = seq // TM_PROJ
    assert (seq // TK_ATT) % 2 == 0, seq
    return pl.pallas_call(
        functools.partial(_attn_kernel, seq=seq),
        grid=(batch, MLA_HEADS // 2),
        in_specs=[pl.BlockSpec((1, nq, 2 * HEAD_PAD, TQ_ATT), lambda b, hp: (b, 0, hp, 0)),
                  pl.BlockSpec((seq, 2 * HEAD_PAD), lambda b, hp: (b, hp)),
                  pl.BlockSpec((1, nvt, 2 * VT_ROWS, TM_PROJ), lambda b, hp: (b, 0, hp, 0))],
        out_specs=pl.BlockSpec((1, nq, 2 * MLA_D_V, TQ_ATT), lambda b, hp: (b, 0, hp, 0)),
        out_shape=jax.ShapeDtypeStruct((batch, nq, MLA_HEADS * MLA_D_V, TQ_ATT), F32),
        scratch_shapes=[pltpu.VMEM((2, TK_ATT, TQ_ATT), F32),
                        pltpu.VMEM((2, TK_ATT, TQ_ATT), F32),
                        pltpu.VMEM((2, VT_ROWS, TQ_ATT), F32)],
        compiler_params=_cparams(("parallel", "parallel")),
        name="attention",
    )(qt, k, vt)


def _layer_norm(z, w, b):
    mu = jnp.mean(z, axis=-1, keepdims=True)
    zc = z - mu
    var = jnp.mean(zc * zc, axis=-1, keepdims=True)
    return zc * lax.rsqrt(var + LN_EPS) * w + b


def _split_bf16(a):
    hi = a.astype(BF16)
    lo = (a - hi.astype(F32)).astype(BF16)
    return hi, lo


def _outproj_kernel(x_ref, ret_ref, att_ref, wo1_ref, wo2_ref, onw_ref, lw_ref, lb_ref, wr_ref,
                    y_ref, lg_ref):
    wh, wl = _split_bf16(wr_ref[...])
    nt = (((1,), (1,)), ((), ()))
    nsub = x_ref.shape[0] // LANES
    pre = {}

    def project(r):
        rows = slice(r * LANES, (r + 1) * LANES)
        att = att_ref[0, 0, :, rows].T
        an = att * lax.rsqrt(jnp.mean(att * att, axis=-1, keepdims=True) + RMS_EPS) * onw_ref[...]
        mix = jnp.dot(ret_ref[rows, :], wo1_ref[...], preferred_element_type=F32)
        mix += jnp.dot(an.astype(BF16), wo2_ref[...], preferred_element_type=F32)
        pre[r] = DN_ALPHA * x_ref[rows, :] + mix

    def normalize(r):
        rows = slice(r * LANES, (r + 1) * LANES)
        y = _layer_norm(pre.pop(r), lw_ref[...], lb_ref[...])
        y_ref[rows, :] = y
        yh, yl = _split_bf16(y)
        lt = lax.dot_general(wh, yh, nt, preferred_element_type=F32)
        lt += lax.dot_general(wh, yl, nt, preferred_element_type=F32)
        lt += lax.dot_general(wl, yh, nt, preferred_element_type=F32)
        lg_ref[r] = lt

    for r in range(nsub + 1):
        if r < nsub:
            project(r)
        if r >= 1:
            normalize(r - 1)


def _outproj(x2d, ret, att_t, wo1, wo2, onw, lw, lb, wrt):
    t = x2d.shape[0]
    tm = TM_PROJ
    nseq = att_t.shape[1]
    assert att_t.shape[3] == tm
    row = lambda i: (i, 0)
    const = lambda i: (0, 0)
    return pl.pallas_call(
        _outproj_kernel,
        grid=(t // tm,),
        in_specs=[pl.BlockSpec((tm, D_MODEL), row),
                  pl.BlockSpec((tm, RET_WIDTH), row),
                  pl.BlockSpec((1, 1, RET_WIDTH, tm), lambda i: (i // nseq, i % nseq, 0, 0)),
                  pl.BlockSpec((RET_WIDTH, D_MODEL), const),
                  pl.BlockSpec((RET_WIDTH, D_MODEL), const),
                  pl.BlockSpec((1, RET_WIDTH), const),
                  pl.BlockSpec((1, D_MODEL), const),
                  pl.BlockSpec((1, D_MODEL), const),
                  pl.BlockSpec((N_EXPERTS, D_MODEL), const)],
        out_specs=[pl.BlockSpec((tm, D_MODEL), row),
                   pl.BlockSpec((tm // LANES, N_EXPERTS, LANES), lambda i: (i, 0, 0))],
        out_shape=[jax.ShapeDtypeStruct((t, D_MODEL), F32),
                   jax.ShapeDtypeStruct((t // LANES, N_EXPERTS, LANES), F32)],
        compiler_params=_cparams(("parallel",)),
        name="outproj",
    )(x2d, ret, att_t, wo1, wo2, onw, lw, lb, wrt)


def _route_kernel(lg_ref, rank_ref, gate_ref, cnt_ref, off_ref, blk_ref, boff_ref, *, cap, nb):
    e = N_EXPERTS
    ntiles = nb // BLOCKS_PER_TILE
    lg = lg_ref[...]
    ex = jnp.exp(lg - jnp.max(lg, axis=1, keepdims=True))
    aff = ex / jnp.sum(ex, axis=1, keepdims=True)
    gate_ref[...] = aff

    def count(mask):
        part = jnp.sum(mask.astype(F32), axis=0)
        return jnp.broadcast_to(jnp.sum(part, axis=1, keepdims=True), (e, LANES))

    def bisect(_, lohi):
        lo, hi = lohi
        mid = 0.5 * (lo + hi)
        ok = count(aff >= mid[None]) >= float(cap)
        return jnp.where(ok, mid, lo), jnp.where(ok, hi, mid)

    lo0 = jnp.zeros((e, LANES), F32)
    hi0 = jnp.full((e, LANES), 2.0, F32)
    lo, hi = lax.fori_loop(0, THRESHOLD_STEPS, bisect, (lo0, hi0))
    gt = aff >= hi[None]
    eq = jnp.logical_and(aff >= lo[None], jnp.logical_not(gt))
    need = float(cap) - count(gt)

    ii = lax.broadcasted_iota(jnp.int32, (LANES, LANES), 0)
    jj = lax.broadcasted_iota(jnp.int32, (LANES, LANES), 1)
    upper = (ii <= jj).astype(BF16)
    ones = jnp.ones((LANES, LANES), BF16)

    def block_scan(mask):
        m2 = mask.astype(BF16).reshape(nb * e, LANES)
        incl = jnp.dot(m2, upper, preferred_element_type=F32).reshape(nb, e, LANES)
        tot = jnp.dot(m2, ones, preferred_element_type=F32).reshape(nb, e, LANES)
        return incl, tot

    def leading_excl_scan(src_ref, dst_ref, n):
        def step(j, carry):
            dst_ref[j] = carry
            return carry + src_ref[j]
        return lax.fori_loop(0, n, step, jnp.zeros((e, LANES), F32))

    eqf = eq.astype(F32)
    incl, tot = block_scan(eq)
    blk_ref[...] = tot
    leading_excl_scan(blk_ref, boff_ref, nb)
    eq_before = incl - eqf + boff_ref[...]
    sel = jnp.logical_or(gt, jnp.logical_and(eq, eq_before < need[None]))

    self32 = sel.astype(F32)
    incl, tot = block_scan(sel)
    local = (incl - self32).reshape(ntiles, BLOCKS_PER_TILE, e, LANES)
    tot4 = tot.reshape(ntiles, BLOCKS_PER_TILE, e, LANES)
    ranks = [local[:, 0]]
    run = tot4[:, 0]
    for b in range(1, BLOCKS_PER_TILE):
        ranks.append(local[:, b] + run)
        run = run + tot4[:, b]
    rank = jnp.stack(ranks, axis=1).reshape(nb, e, LANES)
    rank_ref[...] = jnp.where(sel, rank, NOT_ROUTED)
    cnt_ref[...] = run
    blk_ref[0:ntiles] = run
    leading_excl_scan(blk_ref, boff_ref, ntiles)
    off_ref[...] = boff_ref[0:ntiles]


def _route(logits, cap):
    nb = logits.shape[0]
    ntiles = nb // BLOCKS_PER_TILE
    big = jax.ShapeDtypeStruct((nb, N_EXPERTS, LANES), F32)
    small = jax.ShapeDtypeStruct((ntiles, N_EXPERTS, LANES), F32)
    return pl.pallas_call(
        functools.partial(_route_kernel, cap=cap, nb=nb),
        out_shape=[big, big, small, small],
        scratch_shapes=[pltpu.VMEM((nb, N_EXPERTS, LANES), F32),
                        pltpu.VMEM((nb, N_EXPERTS, LANES), F32)],
        compiler_params=pltpu.CompilerParams(vmem_limit_bytes=VMEM_LIMIT),
        name="route",
    )(logits)


def _lead(off_ref, tile, ex):
    pos = off_ref[tile * N_EXPERTS + ex]
    lead = lax.rem(pos, ROW_ALIGN)
    return pos - lead, lead


def _overflow_chunks(lead, cnt):
    return lax.div(jnp.maximum(lead + cnt - SLAB_ROWS, 0) + (CHUNK_ROWS - 1), CHUNK_ROWS)


def _compact_kernel(off_ref, cnt_ref, y_ref, rank_ref, xg_ref, p_ref, pc_ref, st_ref, ost_ref,
                    carry_ref, zero_ref, sem, osem, *, cap, cap_rows):
    i = pl.program_id(0)
    nt = pl.num_programs(0)
    slot = i % 2

    def slab(tile, slt, ex, fn):
        start, lead = _lead(off_ref, tile, ex)
        used = lead + cnt_ref[tile * N_EXPERTS + ex]
        for first, rows in SLAB_PIECES:
            def piece(first=first, rows=rows):
                fn(pltpu.make_async_copy(
                    st_ref.at[slt, pl.ds(ex * SLAB_ROWS + first, rows)],
                    xg_ref.at[pl.ds(pl.multiple_of(ex * cap_rows + start + first, ROW_ALIGN), rows)],
                    sem.at[slt]))
            if first == 0:
                piece()
            else:
                pl.when(used > first)(piece)

    @pl.when(i == 0)
    def _():
        carry_ref[...] = jnp.zeros_like(carry_ref)
        zero_ref[...] = jnp.zeros_like(zero_ref)
        tails = [pltpu.make_async_copy(zero_ref, xg_ref.at[pl.ds(ex * cap_rows + cap, cap_rows - cap)],
                                       osem.at[0]) for ex in range(N_EXPERTS)]
        for cp in tails:
            cp.start()
        for cp in tails:
            cp.wait()

    xb = y_ref[...].astype(BF16)
    rank = jnp.concatenate([rank_ref[b] for b in range(BLOCKS_PER_TILE)], axis=1)
    jslab = lax.broadcasted_iota(jnp.int32, (SLAB_ROWS, TM_MOE), 0).astype(F32)
    jgrp = lax.broadcasted_iota(jnp.int32, (ROW_ALIGN, TM_MOE), 0).astype(F32)
    last_group = []
    for ex in range(N_EXPERTS):
        _, lead = _lead(off_ref, i, ex)
        sh = rank[ex:ex + 1, :] + lead.astype(F32)
        grp = lax.div(lead + cnt_ref[i * N_EXPERTS + ex], ROW_ALIGN)
        p_ref[ex * SLAB_ROWS:(ex + 1) * SLAB_ROWS, :] = (sh == jslab).astype(BF16)
        pc_ref[ex * ROW_ALIGN:(ex + 1) * ROW_ALIGN, :] = (
            sh - (grp * ROW_ALIGN).astype(F32) == jgrp).astype(BF16)
        last_group.append(grp)
    rows = jnp.dot(p_ref[...], xb, preferred_element_type=F32)
    tail = jnp.dot(pc_ref[...], xb, preferred_element_type=F32)
    for ex in range(N_EXPERTS):
        lo = ex * SLAB_ROWS
        old = carry_ref[ex]
        st_ref[slot, lo:lo + ROW_ALIGN, :] = (rows[lo:lo + ROW_ALIGN] + old).astype(BF16)
        st_ref[slot, lo + ROW_ALIGN:lo + SLAB_ROWS, :] = rows[lo + ROW_ALIGN:lo + SLAB_ROWS].astype(BF16)
        carry_ref[ex] = tail[ex * ROW_ALIGN:(ex + 1) * ROW_ALIGN] + jnp.where(last_group[ex] == 0, old, 0.0)

    @pl.when(i >= 1)
    def _():
        for ex in range(N_EXPERTS):
            slab(i - 1, 1 - slot, ex, lambda cp: cp.wait())

    for ex in range(N_EXPERTS):
        slab(i, slot, ex, lambda cp: cp.start())

    extra = [_overflow_chunks(_lead(off_ref, i, ex)[1], cnt_ref[i * N_EXPERTS + ex]) for ex in range(N_EXPERTS)]

    @pl.when(functools.reduce(lambda a, b: a + b, extra) > 0)
    def _():
        xo = y_ref[...].astype(BF16)
        jchunk = lax.broadcasted_iota(jnp.int32, (CHUNK_ROWS, TM_MOE), 0).astype(F32)
        for ex in range(N_EXPERTS):
            start, lead = _lead(off_ref, i, ex)
            sh = (jnp.concatenate([rank_ref[b, ex:ex + 1, :] for b in range(BLOCKS_PER_TILE)], axis=1)
                  + lead.astype(F32))

            def chunk(c, carry, ex=ex, start=start, sh=sh):
                first = SLAB_ROWS + c * CHUNK_ROWS
                pc = (sh == jchunk + first.astype(F32)).astype(BF16)
                ost_ref[...] = jnp.dot(pc, xo, preferred_element_type=F32).astype(BF16)
                cp = pltpu.make_async_copy(
                    ost_ref,
                    xg_ref.at[pl.ds(pl.multiple_of(ex * cap_rows + start + first, ROW_ALIGN), CHUNK_ROWS)],
                    osem.at[0])
                cp.start()
                cp.wait()
                return carry

            lax.fori_loop(0, extra[ex], chunk, 0)

    @pl.when(i == nt - 1)
    def _():
        for ex in range(N_EXPERTS):
            slab(i, slot, ex, lambda cp: cp.wait())


def _compact(off, cnt, y1, rank_em, cap, cap_rows):
    n = y1.shape[0]
    ntiles = n // TM_MOE
    gs = pltpu.PrefetchScalarGridSpec(
        num_scalar_prefetch=2,
        grid=(ntiles,),
        in_specs=[pl.BlockSpec((TM_MOE, D_MODEL), lambda i, o, c: (i, 0)),
                  pl.BlockSpec((BLOCKS_PER_TILE, N_EXPERTS, LANES), lambda i, o, c: (i, 0, 0))],
        out_specs=pl.BlockSpec(memory_space=pl.ANY),
        scratch_shapes=[pltpu.VMEM((N_EXPERTS * SLAB_ROWS, TM_MOE), BF16),
                        pltpu.VMEM((N_EXPERTS * ROW_ALIGN, TM_MOE), BF16),
                        pltpu.VMEM((2, N_EXPERTS * SLAB_ROWS, D_MODEL), BF16),
                        pltpu.VMEM((CHUNK_ROWS, D_MODEL), BF16),
                        pltpu.VMEM((N_EXPERTS, ROW_ALIGN, D_MODEL), F32),
                        pltpu.VMEM((cap_rows - cap, D_MODEL), BF16),
                        pltpu.SemaphoreType.DMA((2,)),
                        pltpu.SemaphoreType.DMA((1,))])
    return pl.pallas_call(
        functools.partial(_compact_kernel, cap=cap, cap_rows=cap_rows),
        grid_spec=gs,
        out_shape=jax.ShapeDtypeStruct((N_EXPERTS * cap_rows, D_MODEL), BF16),
        compiler_params=_cparams(("arbitrary",)),
        name="compact",
    )(off, cnt, y1, rank_em)


def _ffn_kernel(x_ref, wg_hbm, wu_hbm, wd_hbm, y_ref, wf_ref, wgb_ref, wub_ref, wdb_ref, sem, *, nvalid):
    ex = pl.program_id(0)
    j = pl.program_id(1)
    slot = ex % 2

    def weights(e, slt, fn):
        for k, src in enumerate((wg_hbm, wu_hbm, wd_hbm)):
            fn(pltpu.make_async_copy(src.at[e], wf_ref.at[slt, k], sem.at[slt]))

    def swiglu(wg, wu, wd):
        x = x_ref[...]
        hg = jnp.dot(x, wg, preferred_element_type=F32)
        hu = jnp.dot(x, wu, preferred_element_type=F32)
        h = (hg / (1.0 + jnp.exp(-hg)) * hu).astype(BF16)
        y_ref[...] = jnp.dot(h, wd, preferred_element_type=F32).astype(BF16)

    @pl.when(j == 0)
    def _():
        @pl.when(ex == 0)
        def _():
            weights(0, 0, lambda cp: cp.start())

        @pl.when(ex + 1 < N_EXPERTS)
        def _():
            weights(ex + 1, 1 - slot, lambda cp: cp.start())

        weights(ex, slot, lambda cp: cp.wait())
        wg = wf_ref[slot, 0].astype(BF16)
        wu = wf_ref[slot, 1].astype(BF16)
        wd = wf_ref[slot, 2].astype(BF16)
        wgb_ref[...] = wg
        wub_ref[...] = wu
        wdb_ref[...] = wd
        swiglu(wg, wu, wd)

    @pl.when(jnp.logical_and(j > 0, j < nvalid))
    def _():
        swiglu(wgb_ref[...], wub_ref[...], wdb_ref[...])

    @pl.when(j >= nvalid)
    def _():
        y_ref[...] = jnp.zeros_like(y_ref)


def _ffn(xg, wg, wu, wd, cap, cap_rows):
    nvalid = cap // TM_FFN
    nt = cap_rows // TM_FFN
    return pl.pallas_call(
        functools.partial(_ffn_kernel, nvalid=nvalid),
        grid=(N_EXPERTS, nt),
        in_specs=[pl.BlockSpec((TM_FFN, D_MODEL), lambda ex, j: (ex * nt + jnp.minimum(j, nvalid - 1), 0)),
                  pl.BlockSpec(memory_space=pl.ANY),
                  pl.BlockSpec(memory_space=pl.ANY),
                  pl.BlockSpec(memory_space=pl.ANY)],
        out_specs=pl.BlockSpec((TM_FFN, D_MODEL), lambda ex, j: (ex * nt + j, 0)),
        out_shape=jax.ShapeDtypeStruct((N_EXPERTS * cap_rows, D_MODEL), BF16),
        scratch_shapes=[pltpu.VMEM((2, 3, D_MODEL, D_MODEL), F32)] + [pltpu.VMEM((D_MODEL, D_MODEL), BF16)] * 3
        + [pltpu.SemaphoreType.DMA((2,))],
        compiler_params=_cparams(("arbitrary", "arbitrary")),
        name="ffn",
    )(xg, wg, wu, wd)


def _combine_kernel(off_ref, cnt_ref, y1_ref, rank_ref, gate_ref, yh_ref, lw_ref, lb_ref,
                    o_ref, ybuf, obuf, acc_ref, sem, osem, *, cap_rows):
    i = pl.program_id(0)
    nt = pl.num_programs(0)
    slot = i % 2

    ncol = N_EXPERTS * SLAB_ROWS

    def fetch(tile, slt, fn):
        for ex in range(N_EXPERTS):
            start, lead = _lead(off_ref, tile, ex)
            used = lead + cnt_ref[tile * N_EXPERTS + ex]
            for first, rows in SLAB_PIECES:
                def piece(ex=ex, start=start, first=first, rows=rows):
                    fn(pltpu.make_async_copy(
                        yh_ref.at[pl.ds(pl.multiple_of(ex * cap_rows + start + first, ROW_ALIGN), rows)],
                        ybuf.at[slt, pl.ds(ex * SLAB_ROWS + first, rows)],
                        sem.at[slt]))
                if first == 0:
                    piece()
                else:
                    pl.when(used > first)(piece)

    @pl.when(i == 0)
    def _():
        ybuf[...] = jnp.zeros_like(ybuf)
        fetch(0, 0, lambda cp: cp.start())

    @pl.when(i + 1 < nt)
    def _():
        fetch(i + 1, 1 - slot, lambda cp: cp.start())

    rank = rank_ref[...]
    gate = gate_ref[...]
    er = lax.broadcasted_iota(jnp.int32, (N_EXPERTS, ncol), 0)
    ec = lax.broadcasted_iota(jnp.int32, (N_EXPERTS, ncol), 1)
    expand = jnp.logical_and(ec >= er * SLAB_ROWS, ec < (er + 1) * SLAB_ROWS).astype(BF16)
    rexp = jnp.dot(rank.astype(BF16), expand, preferred_element_type=F32)
    gexp = jnp.dot(gate.astype(BF16), expand, preferred_element_type=F32)
    col = lax.broadcasted_iota(jnp.int32, (1, ncol), 1)
    want = jnp.zeros((1, ncol), F32)
    for ex in range(N_EXPERTS):
        _, lead = _lead(off_ref, i, ex)
        inside = jnp.logical_and(col >= ex * SLAB_ROWS, col < (ex + 1) * SLAB_ROWS)
        want = jnp.where(inside, (col - ex * SLAB_ROWS - lead).astype(F32), want)
    gmat = jnp.where(rexp == want, gexp, 0.0).astype(BF16)

    fetch(i, slot, lambda cp: cp.wait())
    extra = [_overflow_chunks(_lead(off_ref, i, ex)[1], cnt_ref[i * N_EXPERTS + ex]) for ex in range(N_EXPERTS)]
    any_extra = functools.reduce(lambda a, b: a + b, extra)

    @pl.when(any_extra == 0)
    def _():
        half = TM_MOE // 2
        mix = [jnp.dot(gmat[r * half:(r + 1) * half], ybuf[slot], preferred_element_type=F32) for r in range(2)]
        for r in range(2):
            rows = slice(r * half, (r + 1) * half)
            o_ref[rows, :] = _layer_norm(DN_ALPHA * y1_ref[rows, :] + mix[r], lw_ref[...], lb_ref[...])

    @pl.when(any_extra > 0)
    def _():
        acc_ref[...] = jnp.dot(gmat, ybuf[slot], preferred_element_type=F32)
        jc = lax.broadcasted_iota(jnp.int32, (TM_MOE, CHUNK_ROWS), 1).astype(F32)
        for ex in range(N_EXPERTS):
            start, lead = _lead(off_ref, i, ex)

            def chunk(c, carry, ex=ex, start=start, lead=lead):
                first = SLAB_ROWS + c * CHUNK_ROWS
                cp = pltpu.make_async_copy(
                    yh_ref.at[pl.ds(pl.multiple_of(ex * cap_rows + start + first, ROW_ALIGN), CHUNK_ROWS)],
                    obuf, osem.at[0])
                cp.start()
                cp.wait()
                gc = jnp.where(rank[:, ex:ex + 1] + lead.astype(F32) == jc + first.astype(F32),
                               gate[:, ex:ex + 1].astype(BF16).astype(F32), 0.0).astype(BF16)
                acc_ref[...] += jnp.dot(gc, obuf[...], preferred_element_type=F32)
                return carry

            lax.fori_loop(0, extra[ex], chunk, 0)
        o_ref[...] = _layer_norm(DN_ALPHA * y1_ref[...] + acc_ref[...], lw_ref[...], lb_ref[...])


def _combine(off, cnt, y1, rank_tm, gate_tm, yh, lw, lb, cap_rows):
    n = y1.shape[0]
    ntiles = n // TM_MOE
    row = lambda i, o, c: (i, 0)
    const = lambda i, o, c: (0, 0)
    gs = pltpu.PrefetchScalarGridSpec(
        num_scalar_prefetch=2,
        grid=(ntiles,),
        in_specs=[pl.BlockSpec((TM_MOE, D_MODEL), row),
                  pl.BlockSpec((TM_MOE, N_EXPERTS), row),
                  pl.BlockSpec((TM_MOE, N_EXPERTS), row),
                  pl.BlockSpec(memory_space=pl.ANY),
                  pl.BlockSpec((1, D_MODEL), const),
                  pl.BlockSpec((1, D_MODEL), const)],
        out_specs=pl.BlockSpec((TM_MOE, D_MODEL), row),
        scratch_shapes=[pltpu.VMEM((2, N_EXPERTS * SLAB_ROWS, D_MODEL), BF16),
                        pltpu.VMEM((CHUNK_ROWS, D_MODEL), BF16),
                        pltpu.VMEM((TM_MOE, D_MODEL), F32),
                        pltpu.SemaphoreType.DMA((2,)),
                        pltpu.SemaphoreType.DMA((1,))])
    return pl.pallas_call(
        functools.partial(_combine_kernel, cap_rows=cap_rows),
        grid_spec=gs,
        out_shape=jax.ShapeDtypeStruct((n, D_MODEL), F32),
        compiler_params=_cparams(("arbitrary",)),
        name="combine",
    )(off, cnt, y1, rank_tm, gate_tm, yh, lw, lb)


def _rope_tables(seq):
    pos = jnp.arange(seq, dtype=F32)[:, None]
    inv_m = 1.0 / (ROPE_BASE ** (jnp.arange(0, MLA_D_ROPE, 2, dtype=F32) / MLA_D_ROPE))
    am = pos * inv_m[None, :]
    one = jnp.ones((seq, MLA_D_NOPE), F32)
    zero_n = jnp.zeros((seq, MLA_D_NOPE), F32)
    zero_p = jnp.zeros((seq, HEAD_PAD - MLA_D_NOPE - MLA_D_ROPE), F32)
    cm = jnp.concatenate([one, jnp.cos(am), jnp.cos(am), zero_p], axis=1)
    sm = jnp.concatenate([zero_n, jnp.sin(am), jnp.sin(am), zero_p], axis=1)
    inv_r = 1.0 / (ROPE_BASE ** (jnp.arange(0, RET_DK, 2, dtype=F32) / RET_DK))
    ar = pos * inv_r[None, :]
    cosr = jnp.concatenate([jnp.cos(ar), jnp.cos(ar)], axis=1)
    sinr = jnp.concatenate([-jnp.sin(ar), jnp.sin(ar)], axis=1)
    return cm, sm, cosr, sinr


def _prep_weights(w_in, mla_w_uq, mla_w_ukv):
    half = MLA_D_ROPE // 2
    base = 4 * RET_WIDTH + MLA_Q_LORA + MLA_KV_LORA
    w_kr = w_in[:, base:base + MLA_D_ROPE]
    zn = jnp.zeros((D_MODEL, MLA_D_NOPE), F32)
    zp = jnp.zeros((D_MODEL, HEAD_PAD - MLA_D_NOPE - MLA_D_ROPE), F32)
    kr_p = jnp.concatenate([zn, w_kr, zp], axis=1)
    kr_rot = jnp.concatenate([zn, -w_kr[:, half:], w_kr[:, :half], zp], axis=1)
    w_all = jnp.concatenate([w_in[:, :base], kr_p, kr_rot], axis=1).astype(BF16)

    wq = mla_w_uq.reshape(MLA_Q_LORA, MLA_HEADS, MLA_D_NOPE + MLA_D_ROPE)
    nope, ropew = wq[..., :MLA_D_NOPE], wq[..., MLA_D_NOPE:]
    zq = jnp.zeros((MLA_Q_LORA, MLA_HEADS, HEAD_PAD - MLA_D_NOPE - MLA_D_ROPE), F32)
    q_p = jnp.concatenate([nope, ropew, zq], axis=-1).reshape(MLA_Q_LORA, -1)
    q_rot = jnp.concatenate([jnp.zeros_like(nope), -ropew[..., half:], ropew[..., :half], zq],
                            axis=-1).reshape(MLA_Q_LORA, -1)
    wq2 = jnp.concatenate([q_p, q_rot], axis=1).astype(BF16)

    wkv3 = mla_w_ukv.reshape(MLA_KV_LORA, MLA_HEADS, MLA_D_NOPE + MLA_D_V)
    k_p = jnp.concatenate([wkv3[..., :MLA_D_NOPE],
                           jnp.zeros((MLA_KV_LORA, MLA_HEADS, HEAD_PAD - MLA_D_NOPE), F32)],
                          axis=-1).reshape(MLA_KV_LORA, -1)
    v_w = wkv3[..., MLA_D_NOPE:].reshape(MLA_KV_LORA, -1)
    wkv = jnp.concatenate([k_p, v_w], axis=1).astype(BF16)
    return w_all, wq2, wkv


def _mixer_ln(x, wts):
    batch, seq, _ = x.shape
    x2d = x.reshape(batch * seq, D_MODEL)
    cm, sm, cosr, sinr = _rope_tables(seq)
    rq, rk, rv, rg, q, k, v = _inproj(x2d, seq, wts["w_all"], wts["wq2"], wts["wkv"],
                                      wts["qnw"], wts["kvnw"], cm, sm)
    ret = _retention(rq, rk, rv, rg, cosr, sinr, wts["dfw"], wts["dbw"], wts["gnw"], batch, seq)
    att = _attention(q, k, v, batch, seq)
    return _outproj(x2d, ret, att, wts["wo1"], wts["wo2"], wts["onw"], wts["ln1w"], wts["ln1b"],
                    wts["wrt"])


def _moe_ln(y1, logits, wts):
    n = y1.shape[0]
    cap = EC_CAPACITY_FACTOR * n // N_EXPERTS
    assert cap % TM_FFN == 0, (n, cap)
    cap_rows = cap + TM_FFN
    rank_em, gate_em, cnt_rep, off_rep = _route(logits, cap)
    cnt = cnt_rep[:, :, 0].astype(jnp.int32).reshape(-1)
    off = off_rep[:, :, 0].astype(jnp.int32).reshape(-1)
    to_tm = lambda a: jnp.transpose(a, (0, 2, 1)).reshape(n, N_EXPERTS)
    xg = _compact(off, cnt, y1, rank_em, cap, cap_rows)
    yh = _ffn(xg, wts["wg"], wts["wu"], wts["wd"], cap, cap_rows)
    return _combine(off, cnt, y1, to_tm(rank_em), to_tm(gate_em), yh, wts["ln2w"], wts["ln2b"],
                    cap_rows)


def _layer(x, wts):
    batch, seq, _ = x.shape
    y1, logits = _mixer_ln(x, wts)
    return _moe_ln(y1, logits, wts).reshape(batch, seq, D_MODEL)


def kernel(x_prompt, x_sample, w_in, ret_decay_fwd, ret_decay_bwd, ret_gn_w, mla_q_norm_w, mla_w_uq,
           mla_kv_norm_w, mla_w_ukv, mla_out_norm_w, w_out, ln1_w, ln1_b, w_router, w_gate, w_up,
           w_down, ln2_w, ln2_b):
    depth = w_in.shape[0]
    y_prompt, y_sample = x_prompt, x_sample
    for l in range(depth):
        w_all, wq2, wkv = _prep_weights(w_in[l], mla_w_uq[l], mla_w_ukv[l])
        rep = lambda a: jnp.broadcast_to(a.astype(F32)[:, None, None], (RET_HEADS, 1, LANES))
        wts = dict(
            w_all=w_all, wq2=wq2, wkv=wkv,
            qnw=mla_q_norm_w[l][None, :], kvnw=mla_kv_norm_w[l][None, :],
            dfw=rep(ret_decay_fwd[l]), dbw=rep(ret_decay_bwd[l]), gnw=ret_gn_w[l][None, :],
            wo1=w_out[l][:RET_WIDTH].astype(BF16), wo2=w_out[l][RET_WIDTH:].astype(BF16),
            onw=mla_out_norm_w[l][None, :], ln1w=ln1_w[l][None, :], ln1b=ln1_b[l][None, :],
            wrt=w_router[l].T,
            wg=w_gate[l], wu=w_up[l], wd=w_down[l],
            ln2w=ln2_w[l][None, :], ln2b=ln2_b[l][None, :])
        y_prompt = _layer(y_prompt, wts)
        y_sample = _layer(y_sample, wts)
    return (y_prompt, y_sample)
```

```python
import functools
import math

import numpy as np
import jax
import jax.numpy as jnp
from jax import lax
from jax.experimental import pallas as pl
from jax.experimental.pallas import tpu as pltpu

F32 = jnp.float32
BF16 = jnp.bfloat16

D_MODEL = 1024
RET_WIDTH = 512
RET_HEADS = 4
RET_DK = 128
RET_CHUNK = 128
MLA_HEADS = 8
MLA_D_NOPE = 64
MLA_D_ROPE = 32
MLA_D_V = 64
MLA_Q_LORA = 256
MLA_KV_LORA = 128
N_EXPERTS = 16
EC_CAPACITY_FACTOR = 2
ROPE_BASE = 10000.0
LN_EPS = 1e-5
RMS_EPS = 1e-6
DN_ALPHA = 2.0 ** 0.25

LANES = 128
HEAD_PAD = 128
PROJ_COLS = 4 * RET_WIDTH + MLA_Q_LORA + MLA_KV_LORA + 2 * LANES
VMEM_LIMIT = 56 * 1024 * 1024

TM_PROJ = 512
RET_STEP_CHUNKS = 16
TQ_ATT = 512
VT_CHUNKS = 2
VT_ROWS = 80
TK_ATT = VT_CHUNKS * TM_PROJ
KSUB_ATT = 256
TM_MOE = 256
ROW_ALIGN = 16
CHUNK_ROWS = 64
SLAB_ROWS = 64
SLAB_PIECES = ((0, 48), (48, 16))
THRESHOLD_STEPS = 64
NOT_ROUTED = -512.0
TM_FFN = 512
BLOCKS_PER_TILE = TM_MOE // LANES

_QK_SCALE = (MLA_D_NOPE + MLA_D_ROPE) ** -0.5 * math.log2(math.e)


def _cparams(sem):
    return pltpu.CompilerParams(dimension_semantics=sem, vmem_limit_bytes=VMEM_LIMIT)


def _inproj_kernel(x_ref, w_ref, wq_ref, wkv_ref, qnw_ref, kvnw_ref, cm_ref, sm_ref,
                   rq_ref, rk_ref, rv_ref, rg_ref, qt_ref, k_ref, vt_ref):
    xb = x_ref[...].astype(BF16)
    proj = jnp.dot(xb, w_ref[...], preferred_element_type=F32)
    rq_ref[...] = proj[:, 0:512]
    rk_ref[...] = proj[:, 512:1024]
    rv_ref[...] = proj[:, 1024:1536]
    rg_ref[...] = proj[:, 1536:2048]
    cm = cm_ref[...]
    sm = sm_ref[...]

    cq = proj[:, 2048:2304]
    cqn = cq * lax.rsqrt(jnp.mean(cq * cq, axis=-1, keepdims=True) + RMS_EPS) * qnw_ref[...]
    qq = jnp.dot(cqn.astype(BF16), wq_ref[...], preferred_element_type=F32)

    ckv = proj[:, 2304:2432]
    ckvn = ckv * lax.rsqrt(jnp.mean(ckv * ckv, axis=-1, keepdims=True) + RMS_EPS) * kvnw_ref[...]
    kv = jnp.dot(ckvn.astype(BF16), wkv_ref[...], preferred_element_type=F32)

    krope = proj[:, 2432:2560] * cm + proj[:, 2560:2688] * sm
    for h in range(MLA_HEADS):
        lo = h * HEAD_PAD
        qh = qq[:, lo:lo + HEAD_PAD] * cm + qq[:, 1024 + lo:1024 + lo + HEAD_PAD] * sm
        qt_ref[0, 0, lo:lo + HEAD_PAD, :] = (qh * _QK_SCALE).T.astype(BF16)
        k_ref[:, lo:lo + HEAD_PAD] = (kv[:, lo:lo + HEAD_PAD] + krope).astype(BF16)
    vt = kv[:, 1024:1536].T.astype(BF16)
    pad_rows = VT_ROWS - MLA_D_V
    ones_row = (lax.broadcasted_iota(jnp.int32, (pad_rows, vt.shape[1]), 0) == 0).astype(BF16)
    for h in range(MLA_HEADS):
        vt_ref[0, 0, h * VT_ROWS:h * VT_ROWS + MLA_D_V, :] = vt[h * MLA_D_V:(h + 1) * MLA_D_V, :]
        vt_ref[0, 0, h * VT_ROWS + MLA_D_V:(h + 1) * VT_ROWS, :] = ones_row


def _inproj(x2d, seq, w_all, wq2, wkv, qnw, kvnw, cm, sm):
    t = x2d.shape[0]
    tm = TM_PROJ
    nseq = seq // tm
    row = lambda i: (i, 0)
    const = lambda i: (0, 0)
    tab = lambda i: (i % nseq, 0)
    outs = [jax.ShapeDtypeStruct((t, RET_WIDTH), F32)] * 4 + [
        jax.ShapeDtypeStruct((t // seq, nseq, MLA_HEADS * HEAD_PAD, tm), BF16),
        jax.ShapeDtypeStruct((t, MLA_HEADS * HEAD_PAD), BF16),
        jax.ShapeDtypeStruct((t // seq, nseq, MLA_HEADS * VT_ROWS, tm), BF16)]
    return pl.pallas_call(
        _inproj_kernel,
        grid=(t // tm,),
        in_specs=[pl.BlockSpec((tm, D_MODEL), row),
                  pl.BlockSpec((D_MODEL, PROJ_COLS), const),
                  pl.BlockSpec((MLA_Q_LORA, 2 * MLA_HEADS * HEAD_PAD), const),
                  pl.BlockSpec((MLA_KV_LORA, MLA_HEADS * (HEAD_PAD + MLA_D_V)), const),
                  pl.BlockSpec((1, MLA_Q_LORA), const),
                  pl.BlockSpec((1, MLA_KV_LORA), const),
                  pl.BlockSpec((tm, LANES), tab),
                  pl.BlockSpec((tm, LANES), tab)],
        out_specs=[pl.BlockSpec((tm, RET_WIDTH), row)] * 4 + [
            pl.BlockSpec((1, 1, MLA_HEADS * HEAD_PAD, tm), lambda i: (i // nseq, i % nseq, 0, 0)),
            pl.BlockSpec((tm, MLA_HEADS * HEAD_PAD), row),
            pl.BlockSpec((1, 1, MLA_HEADS * VT_ROWS, tm), lambda i: (i // nseq, i % nseq, 0, 0))],
        out_shape=outs,
        compiler_params=_cparams(("parallel",)),
        name="inproj",
    )(x2d, w_all, wq2, wkv, qnw, kvnw, cm, sm)


def _log_sigmoid(x):
    return jnp.minimum(x, 0.0) - jnp.log(1.0 + jnp.exp(-jnp.abs(x)))


def _ret_kernel(q_ref, k_ref, v_ref, g_ref, cos_ref, sin_ref, df_ref, db_ref, gnw_ref,
                o_ref, sf_ref, sb_ref, sbst_ref, kvf_ref, kb_ref, vb_ref, *, nsteps):
    phase = pl.program_id(2)
    step = pl.program_id(3)
    c = RET_CHUNK
    lgf = _log_sigmoid(df_ref[0])
    lgb = _log_sigmoid(db_ref[0])
    rpos = lax.broadcasted_iota(jnp.int32, (c, c), 0).astype(F32)
    cpos = lax.broadcasted_iota(jnp.int32, (c, c), 1).astype(F32)
    tn = (((0,), (0,)), ((), ()))

    def rope(x, cs, sn):
        return x * cs + pltpu.roll(x, RET_DK // 2, axis=1) * sn

    @pl.when(jnp.logical_and(phase == 0, step == 0))
    def _():
        sb_ref[...] = jnp.zeros_like(sb_ref)

    @pl.when(jnp.logical_and(phase == 1, step == 0))
    def _():
        sf_ref[...] = jnp.zeros_like(sf_ref)

    @pl.when(phase == 0)
    def _():
        kwb = jnp.exp(lgb * rpos)
        kwf = jnp.exp(lgf * (float(c - 1) - rpos))
        cdb = jnp.exp(lgb * float(c))
        kvstep = nsteps - 1 - step
        kvs = {}
        for g in range(RET_STEP_CHUNKS):
            rows = slice(g * c, (g + 1) * c)
            idx = kvstep * RET_STEP_CHUNKS + g
            kr = rope(k_ref[rows, :], cos_ref[rows, :], sin_ref[rows, :]) * (RET_DK ** -0.5)
            vb = v_ref[rows, :].astype(BF16)
            kb_ref[idx] = kr.astype(BF16)
            vb_ref[idx] = vb
            kw = jnp.concatenate([(kr * kwb).astype(BF16), (kr * kwf).astype(BF16)], axis=1)
            kvs[g] = lax.dot_general(kw, vb, tn, preferred_element_type=F32)
        sb = sb_ref[...]
        for g in range(RET_STEP_CHUNKS - 1, -1, -1):
            idx = kvstep * RET_STEP_CHUNKS + g
            sbst_ref[idx] = sb.astype(BF16)
            kvf_ref[idx] = kvs[g][c:2 * c]
            sb = cdb * sb + kvs[g][0:c]
        sb_ref[...] = sb

    @pl.when(phase == 1)
    def _():
        diff = rpos - cpos
        dmat = jnp.where(diff >= 0, jnp.exp(lgf * jnp.maximum(diff, 0.0)),
                         jnp.exp(lgb * jnp.maximum(-diff, 0.0)))
        qwf = jnp.exp(lgf * (rpos + 1.0))
        qwb = jnp.exp(lgb * (float(c) - rpos))
        cdf = jnp.exp(lgf * float(c))
        gnw = gnw_ref[...]
        qrs, scs, outs = {}, {}, {}
        sfs = {0: sf_ref[...]}

        def scores(g):
            rows = slice(g * c, (g + 1) * c)
            idx = step * RET_STEP_CHUNKS + g
            qrs[g] = rope(q_ref[rows, :], cos_ref[rows, :], sin_ref[rows, :])
            scs[g] = lax.dot_general(qrs[g].astype(BF16), kb_ref[idx], (((1,), (1,)), ((), ())),
                                     preferred_element_type=F32)
            sfs[g + 1] = cdf * sfs[g] + kvf_ref[idx]

        def mix(g):
            idx = step * RET_STEP_CHUNKS + g
            qr = qrs.pop(g)
            lhs = jnp.concatenate([(scs.pop(g) * dmat).astype(BF16), (qr * qwf).astype(BF16),
                                   (qr * qwb).astype(BF16)], axis=1)
            rhs = jnp.concatenate([vb_ref[idx], sfs[g].astype(BF16), sbst_ref[idx]], axis=0)
            outs[g] = jnp.dot(lhs, rhs, preferred_element_type=F32)

        def finish(g):
            rows = slice(g * c, (g + 1) * c)
            o = outs.pop(g)
            mu = jnp.mean(o, axis=-1, keepdims=True)
            oc = o - mu
            var = jnp.mean(oc * oc, axis=-1, keepdims=True)
            on = oc * lax.rsqrt(var + LN_EPS) * gnw
            gt = g_ref[rows, :]
            o_ref[rows, :] = (gt / (1.0 + jnp.exp(-gt)) * on).astype(o_ref.dtype)

        for s in range(RET_STEP_CHUNKS + 4):
            if s - 4 >= 0:
                finish(s - 4)
            if 0 <= s - 2 < RET_STEP_CHUNKS:
                mix(s - 2)
            if s < RET_STEP_CHUNKS:
                scores(s)
        sf_ref[...] = sfs[RET_STEP_CHUNKS]


def _retention(rq, rk, rv, rg, cosr, sinr, dfw, dbw, gnw, batch, seq):
    t = rq.shape[0]
    ts = RET_STEP_CHUNKS * RET_CHUNK
    nsteps = seq // ts
    nchunks = seq // RET_CHUNK
    kvi = lambda p, i: jnp.where(p == 0, nsteps - 1 - i, 0)
    qi = lambda p, i: jnp.where(p == 0, 0, i)
    kv_map = lambda b, h, p, i: (b * nsteps + kvi(p, i), h)
    q_map = lambda b, h, p, i: (b * nsteps + qi(p, i), h)
    tab_map = lambda b, h, p, i: (jnp.where(p == 0, nsteps - 1 - i, i), 0)
    head3 = lambda b, h, p, i: (h, 0, 0)
    return pl.pallas_call(
        functools.partial(_ret_kernel, nsteps=nsteps),
        grid=(batch, RET_HEADS, 2, nsteps),
        in_specs=[pl.BlockSpec((ts, RET_DK), q_map),
                  pl.BlockSpec((ts, RET_DK), kv_map),
                  pl.BlockSpec((ts, RET_DK), kv_map),
                  pl.BlockSpec((ts, RET_DK), q_map),
                  pl.BlockSpec((ts, RET_DK), tab_map),
                  pl.BlockSpec((ts, RET_DK), tab_map),
                  pl.BlockSpec((1, 1, LANES), head3),
                  pl.BlockSpec((1, 1, LANES), head3),
                  pl.BlockSpec((1, RET_DK), lambda b, h, p, i: (0, h))],
        out_specs=pl.BlockSpec((ts, RET_DK), q_map),
        out_shape=jax.ShapeDtypeStruct((t, RET_WIDTH), BF16),
        scratch_shapes=[pltpu.VMEM((RET_DK, RET_DK), F32),
                        pltpu.VMEM((RET_DK, RET_DK), F32),
                        pltpu.VMEM((nchunks, RET_DK, RET_DK), BF16),
                        pltpu.VMEM((nchunks, RET_DK, RET_DK), F32),
                        pltpu.VMEM((nchunks, RET_CHUNK, RET_DK), BF16),
                        pltpu.VMEM((nchunks, RET_CHUNK, RET_DK), BF16)],
        compiler_params=_cparams(("parallel", "parallel", "arbitrary", "arbitrary")),
        name="retention",
    )(rq, rk, rv, rg, cosr, sinr, dfw, dbw, gnw)


def _attn_kernel(qt_ref, k_ref, vt_ref, o_ref, ste_ref, sto_ref, acc_ref, *, seq):
    nkv = seq // TK_ATT
    npair = (seq // TQ_ATT) * nkv
    nt = (((1,), (1,)), ((), ()))
    nsub = TK_ATT // KSUB_ATT
    sub_per_chunk = TM_PROJ // KSUB_ATT
    neg_inf = jnp.full((1, TQ_ATT), -jnp.inf, F32)

    def score_steps(t, st_ref, tile_max):
        t = jnp.minimum(t, npair - 1)
        qi = t // nkv
        ks = pl.multiple_of((t % nkv) * TK_ATT, TK_ATT)

        def step(h, s):
            def run():
                hs = slice(h * HEAD_PAD, (h + 1) * HEAD_PAD)
                sub = slice(s * KSUB_ATT, (s + 1) * KSUB_ATT)
                st_ref[h, sub, :] = jnp.dot(k_ref[pl.ds(ks + s * KSUB_ATT, KSUB_ATT), hs], qt_ref[0, qi, hs, :],
                                            preferred_element_type=F32)
                top = jnp.max(st_ref[h, sub, :], axis=0, keepdims=True)
                tile_max[h] = top if s == 0 else jnp.maximum(tile_max[h], top)
            return run
        return [step(h, s) for h in range(2) for s in range(nsub)]

    def softmax_steps(t, st_ref, run_max, tile_max, out):
        kj = t % nkv
        steps = []
        for h in range(2):
            state = {}

            def start(h=h, state=state):
                m = jnp.where(kj == 0, neg_inf, run_max[h])
                state["mn"] = jnp.maximum(m, tile_max[h])
                state["alpha"] = jnp.exp2(m - state["mn"])

            def part(s, h=h, state=state):
                def run():
                    x = st_ref[h, s * KSUB_ATT:(s + 1) * KSUB_ATT, :] - state["mn"]
                    pb = jnp.exp2(x.astype(BF16))
                    c, r = divmod(s, sub_per_chunk)
                    pv = jnp.dot(vt_ref[0, kj * VT_CHUNKS + c, h * VT_ROWS:(h + 1) * VT_ROWS,
                                        r * KSUB_ATT:(r + 1) * KSUB_ATT],
                                 pb, preferred_element_type=F32)
                    state["pv"] = pv if s == 0 else state["pv"] + pv
                    if s == nsub - 1:
                        old = jnp.where(kj == 0, 0.0, acc_ref[h])
                        acc_ref[h] = state["alpha"] * old + state["pv"]
                        out[h] = state["mn"]
                return run
            steps += [start] + [part(s) for s in range(nsub)]
        return steps

    def interleave(mxu_steps, vpu_steps):
        n = max(len(mxu_steps), len(vpu_steps))
        for i in range(n):
            if i < len(mxu_steps):
                mxu_steps[i]()
            if i < len(vpu_steps):
                vpu_steps[i]()

    def body(jj, carry):
        run_max, max_even = carry
        t0 = 2 * jj
        mid, out, max_odd, max_next = {}, {}, {}, {}
        interleave(score_steps(t0 + 1, sto_ref, max_odd), softmax_steps(t0, ste_ref, run_max, max_even, mid))
        interleave(score_steps(t0 + 2, ste_ref, max_next),
                   softmax_steps(t0 + 1, sto_ref, (mid[0], mid[1]), max_odd, out))

        @pl.when((t0 + 1) % nkv == nkv - 1)
        def _():
            qi = t0 // nkv
            for h in range(2):
                acc = acc_ref[h]
                o_ref[0, qi, h * MLA_D_V:(h + 1) * MLA_D_V, :] = acc[0:MLA_D_V] / acc[MLA_D_V:MLA_D_V + 1]
        return (out[0], out[1]), (max_next[0], max_next[1])

    acc_ref[...] = jnp.zeros_like(acc_ref)
    max_first = {}
    for run in score_steps(0, ste_ref, max_first):
        run()
    lax.fori_loop(0, npair // 2, body, ((neg_inf, neg_inf), (max_first[0], max_first[1])))


def _attention(qt, k, vt, batch, seq):
    nq = seq // TQ_ATT
    nvt = seq // TM_PROJ
    assert (seq // TK_ATT) % 2 == 0, seq
    return pl.pallas_call(
        functools.partial(_attn_kernel, seq=seq),
        grid=(batch, MLA_HEADS // 2),
        in_specs=[pl.BlockSpec((1, nq, 2 * HEAD_PAD, TQ_ATT), lambda b, hp: (b, 0, hp, 0)),
                  pl.BlockSpec((seq, 2 * HEAD_PAD), lambda b, hp: (b, hp)),
                  pl.BlockSpec((1, nvt, 2 * VT_ROWS, TM_PROJ), lambda b, hp: (b, 0, hp, 0))],
        out_specs=pl.BlockSpec((1, nq, 2 * MLA_D_V, TQ_ATT), lambda b, hp: (b, 0, hp, 0)),
        out_shape=jax.ShapeDtypeStruct((batch, nq, MLA_HEADS * MLA_D_V, TQ_ATT), F32),
        scratch_shapes=[pltpu.VMEM((2, TK_ATT, TQ_ATT), F32),
                        pltpu.VMEM((2, TK_ATT, TQ_ATT), F32),
                        pltpu.VMEM((2, VT_ROWS, TQ_ATT), F32)],
        compiler_params=_cparams(("parallel", "parallel")),
        name="attention",
    )(qt, k, vt)


def _layer_norm(z, w, b):
    mu = jnp.mean(z, axis=-1, keepdims=True)
    zc = z - mu
    var = jnp.mean(zc * zc, axis=-1, keepdims=True)
    return zc * lax.rsqrt(var + LN_EPS) * w + b


def _split_bf16(a):
    hi = a.astype(BF16)
    lo = (a - hi.astype(F32)).astype(BF16)
    return hi, lo


def _outproj_kernel(x_ref, ret_ref, att_ref, wo1_ref, wo2_ref, onw_ref, lw_ref, lb_ref, wr_ref,
                    y_ref, lg_ref):
    wh, wl = _split_bf16(wr_ref[...])
    nt = (((1,), (1,)), ((), ()))
    nsub = x_ref.shape[0] // LANES
    pre = {}

    def project(r):
        rows = slice(r * LANES, (r + 1) * LANES)
        att = att_ref[0, 0, :, rows].T
        an = att * lax.rsqrt(jnp.mean(att * att, axis=-1, keepdims=True) + RMS_EPS) * onw_ref[...]
        mix = jnp.dot(ret_ref[rows, :], wo1_ref[...], preferred_element_type=F32)
        mix += jnp.dot(an.astype(BF16), wo2_ref[...], preferred_element_type=F32)
        pre[r] = DN_ALPHA * x_ref[rows, :] + mix

    def normalize(r):
        rows = slice(r * LANES, (r + 1) * LANES)
        y = _layer_norm(pre.pop(r), lw_ref[...], lb_ref[...])
        y_ref[rows, :] = y
        yh, yl = _split_bf16(y)
        lt = lax.dot_general(wh, yh, nt, preferred_element_type=F32)
        lt += lax.dot_general(wh, yl, nt, preferred_element_type=F32)
        lt += lax.dot_general(wl, yh, nt, preferred_element_type=F32)
        lg_ref[r] = lt

    for r in range(nsub + 1):
        if r < nsub:
            project(r)
        if r >= 1:
            normalize(r - 1)


def _outproj(x2d, ret, att_t, wo1, wo2, onw, lw, lb, wrt):
    t = x2d.shape[0]
    tm = TM_PROJ
    nseq = att_t.shape[1]
    assert att_t.shape[3] == tm
    row = lambda i: (i, 0)
    const = lambda i: (0, 0)
    return pl.pallas_call(
        _outproj_kernel,
        grid=(t // tm,),
        in_specs=[pl.BlockSpec((tm, D_MODEL), row),
                  pl.BlockSpec((tm, RET_WIDTH), row),
                  pl.BlockSpec((1, 1, RET_WIDTH, tm), lambda i: (i // nseq, i % nseq, 0, 0)),
                  pl.BlockSpec((RET_WIDTH, D_MODEL), const),
                  pl.BlockSpec((RET_WIDTH, D_MODEL), const),
                  pl.BlockSpec((1, RET_WIDTH), const),
                  pl.BlockSpec((1, D_MODEL), const),
                  pl.BlockSpec((1, D_MODEL), const),
                  pl.BlockSpec((N_EXPERTS, D_MODEL), const)],
        out_specs=[pl.BlockSpec((tm, D_MODEL), row),
                   pl.BlockSpec((tm // LANES, N_EXPERTS, LANES), lambda i: (i, 0, 0))],
        out_shape=[jax.ShapeDtypeStruct((t, D_MODEL), F32),
                   jax.ShapeDtypeStruct((t // LANES, N_EXPERTS, LANES), F32)],
        compiler_params=_cparams(("parallel",)),
        name="outproj",
    )(x2d, ret, att_t, wo1, wo2, onw, lw, lb, wrt)


def _route_kernel(lg_ref, rank_ref, gate_ref, cnt_ref, off_ref, blk_ref, boff_ref, *, cap, nb):
    e = N_EXPERTS
    ntiles = nb // BLOCKS_PER_TILE
    lg = lg_ref[...]
    ex = jnp.exp(lg - jnp.max(lg, axis=1, keepdims=True))
    aff = ex / jnp.sum(ex, axis=1, keepdims=True)
    gate_ref[...] = aff

    def count(mask):
        part = jnp.sum(mask.astype(F32), axis=0)
        return jnp.broadcast_to(jnp.sum(part, axis=1, keepdims=True), (e, LANES))

    def bisect(_, lohi):
        lo, hi = lohi
        mid = 0.5 * (lo + hi)
        ok = count(aff >= mid[None]) >= float(cap)
        return jnp.where(ok, mid, lo), jnp.where(ok, hi, mid)

    lo0 = jnp.zeros((e, LANES), F32)
    hi0 = jnp.full((e, LANES), 2.0, F32)
    lo, hi = lax.fori_loop(0, THRESHOLD_STEPS, bisect, (lo0, hi0))
    gt = aff >= hi[None]
    eq = jnp.logical_and(aff >= lo[None], jnp.logical_not(gt))
    need = float(cap) - count(gt)

    ii = lax.broadcasted_iota(jnp.int32, (LANES, LANES), 0)
    jj = lax.broadcasted_iota(jnp.int32, (LANES, LANES), 1)
    upper = (ii <= jj).astype(BF16)
    ones = jnp.ones((LANES, LANES), BF16)

    def block_scan(mask):
        m2 = mask.astype(BF16).reshape(nb * e, LANES)
        incl = jnp.dot(m2, upper, preferred_element_type=F32).reshape(nb, e, LANES)
        tot = jnp.dot(m2, ones, preferred_element_type=F32).reshape(nb, e, LANES)
        return incl, tot

    def leading_excl_scan(src_ref, dst_ref, n):
        def step(j, carry):
            dst_ref[j] = carry
            return carry + src_ref[j]
        return lax.fori_loop(0, n, step, jnp.zeros((e, LANES), F32))

    eqf = eq.astype(F32)
    incl, tot = block_scan(eq)
    blk_ref[...] = tot
    leading_excl_scan(blk_ref, boff_ref, nb)
    eq_before = incl - eqf + boff_ref[...]
    sel = jnp.logical_or(gt, jnp.logical_and(eq, eq_before < need[None]))

    self32 = sel.astype(F32)
    incl, tot = block_scan(sel)
    local = (incl - self32).reshape(ntiles, BLOCKS_PER_TILE, e, LANES)
    tot4 = tot.reshape(ntiles, BLOCKS_PER_TILE, e, LANES)
    ranks = [local[:, 0]]
    run = tot4[:, 0]
    for b in range(1, BLOCKS_PER_TILE):
        ranks.append(local[:, b] + run)
        run = run + tot4[:, b]
    rank = jnp.stack(ranks, axis=1).reshape(nb, e, LANES)
    rank_ref[...] = jnp.where(sel, rank, NOT_ROUTED)
    cnt_ref[...] = run
    blk_ref[0:ntiles] = run
    leading_excl_scan(blk_ref, boff_ref, ntiles)
    off_ref[...] = boff_ref[0:ntiles]


def _route(logits, cap):
    nb = logits.shape[0]
    ntiles = nb // BLOCKS_PER_TILE
    big = jax.ShapeDtypeStruct((nb, N_EXPERTS, LANES), F32)
    small = jax.ShapeDtypeStruct((ntiles, N_EXPERTS, LANES), F32)
    return pl.pallas_call(
        functools.partial(_route_kernel, cap=cap, nb=nb),
        out_shape=[big, big, small, small],
        scratch_shapes=[pltpu.VMEM((nb, N_EXPERTS, LANES), F32),
                        pltpu.VMEM((nb, N_EXPERTS, LANES), F32)],
        compiler_params=pltpu.CompilerParams(vmem_limit_bytes=VMEM_LIMIT),
        name="route",
    )(logits)


def _lead(off_ref, tile, ex):
    pos = off_ref[tile * N_EXPERTS + ex]
    lead = lax.rem(pos, ROW_ALIGN)
    return pos - lead, lead


def _overflow_chunks(lead, cnt):
    return lax.div(jnp.maximum(lead + cnt - SLAB_ROWS, 0) + (CHUNK_ROWS - 1), CHUNK_ROWS)


def _compact_kernel(off_ref, cnt_ref, y_ref, rank_ref, xg_ref, p_ref, pc_ref, st_ref, ost_ref,
                    carry_ref, zero_ref, sem, osem, *, cap, cap_rows):
    i = pl.program_id(0)
    nt = pl.num_programs(0)
    slot = i % 2

    def slab(tile, slt, ex, fn):
        start, lead = _lead(off_ref, tile, ex)
        used = lead + cnt_ref[tile * N_EXPERTS + ex]
        for first, rows in SLAB_PIECES:
            def piece(first=first, rows=rows):
                fn(pltpu.make_async_copy(
                    st_ref.at[slt, pl.ds(ex * SLAB_ROWS + first, rows)],
                    xg_ref.at[pl.ds(pl.multiple_of(ex * cap_rows + start + first, ROW_ALIGN), rows)],
                    sem.at[slt]))
            if first == 0:
                piece()
            else:
                pl.when(used > first)(piece)

    @pl.when(i == 0)
    def _():
        carry_ref[...] = jnp.zeros_like(carry_ref)
        zero_ref[...] = jnp.zeros_like(zero_ref)
        tails = [pltpu.make_async_copy(zero_ref, xg_ref.at[pl.ds(ex * cap_rows + cap, cap_rows - cap)],
                                       osem.at[0]) for ex in range(N_EXPERTS)]
        for cp in tails:
            cp.start()
        for cp in tails:
            cp.wait()

    xb = y_ref[...].astype(BF16)
    rank = jnp.concatenate([rank_ref[b] for b in range(BLOCKS_PER_TILE)], axis=1)
    jslab = lax.broadcasted_iota(jnp.int32, (SLAB_ROWS, TM_MOE), 0).astype(F32)
    jgrp = lax.broadcasted_iota(jnp.int32, (ROW_ALIGN, TM_MOE), 0).astype(F32)
    last_group = []
    for ex in range(N_EXPERTS):
        _, lead = _lead(off_ref, i, ex)
        sh = rank[ex:ex + 1, :] + lead.astype(F32)
        grp = lax.div(lead + cnt_ref[i * N_EXPERTS + ex], ROW_ALIGN)
        p_ref[ex * SLAB_ROWS:(ex + 1) * SLAB_ROWS, :] = (sh == jslab).astype(BF16)
        pc_ref[ex * ROW_ALIGN:(ex + 1) * ROW_ALIGN, :] = (
            sh - (grp * ROW_ALIGN).astype(F32) == jgrp).astype(BF16)
        last_group.append(grp)
    rows = jnp.dot(p_ref[...], xb, preferred_element_type=F32)
    tail = jnp.dot(pc_ref[...], xb, preferred_element_type=F32)
    for ex in range(N_EXPERTS):
        lo = ex * SLAB_ROWS
        old = carry_ref[ex]
        st_ref[slot, lo:lo + ROW_ALIGN, :] = (rows[lo:lo + ROW_ALIGN] + old).astype(BF16)
        st_ref[slot, lo + ROW_ALIGN:lo + SLAB_ROWS, :] = rows[lo + ROW_ALIGN:lo + SLAB_ROWS].astype(BF16)
        carry_ref[ex] = tail[ex * ROW_ALIGN:(ex + 1) * ROW_ALIGN] + jnp.where(last_group[ex] == 0, old, 0.0)

    @pl.when(i >= 1)
    def _():
        for ex in range(N_EXPERTS):
            slab(i - 1, 1 - slot, ex, lambda cp: cp.wait())

    for ex in range(N_EXPERTS):
        slab(i, slot, ex, lambda cp: cp.start())

    extra = [_overflow_chunks(_lead(off_ref, i, ex)[1], cnt_ref[i * N_EXPERTS + ex]) for ex in range(N_EXPERTS)]

    @pl.when(functools.reduce(lambda a, b: a + b, extra) > 0)
    def _():
        xo = y_ref[...].astype(BF16)
        jchunk = lax.broadcasted_iota(jnp.int32, (CHUNK_ROWS, TM_MOE), 0).astype(F32)
        for ex in range(N_EXPERTS):
            start, lead = _lead(off_ref, i, ex)
            sh = (jnp.concatenate([rank_ref[b, ex:ex + 1, :] for b in range(BLOCKS_PER_TILE)], axis=1)
                  + lead.astype(F32))

            def chunk(c, carry, ex=ex, start=start, sh=sh):
                first = SLAB_ROWS + c * CHUNK_ROWS
                pc = (sh == jchunk + first.astype(F32)).astype(BF16)
                ost_ref[...] = jnp.dot(pc, xo, preferred_element_type=F32).astype(BF16)
                cp = pltpu.make_async_copy(
                    ost_ref,
                    xg_ref.at[pl.ds(pl.multiple_of(ex * cap_rows + start + first, ROW_ALIGN), CHUNK_ROWS)],
                    osem.at[0])
                cp.start()
                cp.wait()
                return carry

            lax.fori_loop(0, extra[ex], chunk, 0)

    @pl.when(i == nt - 1)
    def _():
        for ex in range(N_EXPERTS):
            slab(i, slot, ex, lambda cp: cp.wait())


def _compact(off, cnt, y1, rank_em, cap, cap_rows):
    n = y1.shape[0]
    ntiles = n // TM_MOE
    gs = pltpu.PrefetchScalarGridSpec(
        num_scalar_prefetch=2,
        grid=(ntiles,),
        in_specs=[pl.BlockSpec((TM_MOE, D_MODEL), lambda i, o, c: (i, 0)),
                  pl.BlockSpec((BLOCKS_PER_TILE, N_EXPERTS, LANES), lambda i, o, c: (i, 0, 0))],
        out_specs=pl.BlockSpec(memory_space=pl.ANY),
        scratch_shapes=[pltpu.VMEM((N_EXPERTS * SLAB_ROWS, TM_MOE), BF16),
                        pltpu.VMEM((N_EXPERTS * ROW_ALIGN, TM_MOE), BF16),
                        pltpu.VMEM((2, N_EXPERTS * SLAB_ROWS, D_MODEL), BF16),
                        pltpu.VMEM((CHUNK_ROWS, D_MODEL), BF16),
                        pltpu.VMEM((N_EXPERTS, ROW_ALIGN, D_MODEL), F32),
                        pltpu.VMEM((cap_rows - cap, D_MODEL), BF16),
                        pltpu.SemaphoreType.DMA((2,)),
                        pltpu.SemaphoreType.DMA((1,))])
    return pl.pallas_call(
        functools.partial(_compact_kernel, cap=cap, cap_rows=cap_rows),
        grid_spec=gs,
        out_shape=jax.ShapeDtypeStruct((N_EXPERTS * cap_rows, D_MODEL), BF16),
        compiler_params=_cparams(("arbitrary",)),
        name="compact",
    )(off, cnt, y1, rank_em)


def _ffn_kernel(x_ref, wg_hbm, wu_hbm, wd_hbm, y_ref, wf_ref, wgb_ref, wub_ref, wdb_ref, sem, *, nvalid):
    ex = pl.program_id(0)
    j = pl.program_id(1)
    slot = ex % 2

    def weights(e, slt, fn):
        for k, src in enumerate((wg_hbm, wu_hbm, wd_hbm)):
            fn(pltpu.make_async_copy(src.at[e], wf_ref.at[slt, k], sem.at[slt]))

    def swiglu(wg, wu, wd):
        x = x_ref[...]
        hg = jnp.dot(x, wg, preferred_element_type=F32)
        hu = jnp.dot(x, wu, preferred_element_type=F32)
        h = (hg / (1.0 + jnp.exp(-hg)) * hu).astype(BF16)
        y_ref[...] = jnp.dot(h, wd, preferred_element_type=F32).astype(BF16)

    @pl.when(j == 0)
    def _():
        @pl.when(ex == 0)
        def _():
            weights(0, 0, lambda cp: cp.start())

        @pl.when(ex + 1 < N_EXPERTS)
        def _():
            weights(ex + 1, 1 - slot, lambda cp: cp.start())

        weights(ex, slot, lambda cp: cp.wait())
        wg = wf_ref[slot, 0].astype(BF16)
        wu = wf_ref[slot, 1].astype(BF16)
        wd = wf_ref[slot, 2].astype(BF16)
        wgb_ref[...] = wg
        wub_ref[...] = wu
        wdb_ref[...] = wd
        swiglu(wg, wu, wd)

    @pl.when(jnp.logical_and(j > 0, j < nvalid))
    def _():
        swiglu(wgb_ref[...], wub_ref[...], wdb_ref[...])

    @pl.when(j >= nvalid)
    def _():
        y_ref[...] = jnp.zeros_like(y_ref)


def _ffn(xg, wg, wu, wd, cap, cap_rows):
    nvalid = cap // TM_FFN
    nt = cap_rows // TM_FFN
    return pl.pallas_call(
        functools.partial(_ffn_kernel, nvalid=nvalid),
        grid=(N_EXPERTS, nt),
        in_specs=[pl.BlockSpec((TM_FFN, D_MODEL), lambda ex, j: (ex * nt + jnp.minimum(j, nvalid - 1), 0)),
                  pl.BlockSpec(memory_space=pl.ANY),
                  pl.BlockSpec(memory_space=pl.ANY),
                  pl.BlockSpec(memory_space=pl.ANY)],
        out_specs=pl.BlockSpec((TM_FFN, D_MODEL), lambda ex, j: (ex * nt + j, 0)),
        out_shape=jax.ShapeDtypeStruct((N_EXPERTS * cap_rows, D_MODEL), BF16),
        scratch_shapes=[pltpu.VMEM((2, 3, D_MODEL, D_MODEL), F32)] + [pltpu.VMEM((D_MODEL, D_MODEL), BF16)] * 3
        + [pltpu.SemaphoreType.DMA((2,))],
        compiler_params=_cparams(("arbitrary", "arbitrary")),
        name="ffn",
    )(xg, wg, wu, wd)


def _combine_kernel(off_ref, cnt_ref, y1_ref, rank_ref, gate_ref, yh_ref, lw_ref, lb_ref,
                    o_ref, ybuf, obuf, acc_ref, sem, osem, *, cap_rows):
    i = pl.program_id(0)
    nt = pl.num_programs(0)
    slot = i % 2

    ncol = N_EXPERTS * SLAB_ROWS

    def fetch(tile, slt, fn):
        for ex in range(N_EXPERTS):
            start, lead = _lead(off_ref, tile, ex)
            used = lead + cnt_ref[tile * N_EXPERTS + ex]
            for first, rows in SLAB_PIECES:
                def piece(ex=ex, start=start, first=first, rows=rows):
                    fn(pltpu.make_async_copy(
                        yh_ref.at[pl.ds(pl.multiple_of(ex * cap_rows + start + first, ROW_ALIGN), rows)],
                        ybuf.at[slt, pl.ds(ex * SLAB_ROWS + first, rows)],
                        sem.at[slt]))
                if first == 0:
                    piece()
                else:
                    pl.when(used > first)(piece)

    @pl.when(i == 0)
    def _():
        ybuf[...] = jnp.zeros_like(ybuf)
        fetch(0, 0, lambda cp: cp.start())

    @pl.when(i + 1 < nt)
    def _():
        fetch(i + 1, 1 - slot, lambda cp: cp.start())

    rank = rank_ref[...]
    gate = gate_ref[...]
    er = lax.broadcasted_iota(jnp.int32, (N_EXPERTS, ncol), 0)
    ec = lax.broadcasted_iota(jnp.int32, (N_EXPERTS, ncol), 1)
    expand = jnp.logical_and(ec >= er * SLAB_ROWS, ec < (er + 1) * SLAB_ROWS).astype(BF16)
    rexp = jnp.dot(rank.astype(BF16), expand, preferred_element_type=F32)
    gexp = jnp.dot(gate.astype(BF16), expand, preferred_element_type=F32)
    col = lax.broadcasted_iota(jnp.int32, (1, ncol), 1)
    want = jnp.zeros((1, ncol), F32)
    for ex in range(N_EXPERTS):
        _, lead = _lead(off_ref, i, ex)
        inside = jnp.logical_and(col >= ex * SLAB_ROWS, col < (ex + 1) * SLAB_ROWS)
        want = jnp.where(inside, (col - ex * SLAB_ROWS - lead).astype(F32), want)
    gmat = jnp.where(rexp == want, gexp, 0.0).astype(BF16)

    fetch(i, slot, lambda cp: cp.wait())
    extra = [_overflow_chunks(_lead(off_ref, i, ex)[1], cnt_ref[i * N_EXPERTS + ex]) for ex in range(N_EXPERTS)]
    any_extra = functools.reduce(lambda a, b: a + b, extra)

    @pl.when(any_extra == 0)
    def _():
        half = TM_MOE // 2
        mix = [jnp.dot(gmat[r * half:(r + 1) * half], ybuf[slot], preferred_element_type=F32) for r in range(2)]
        for r in range(2):
            rows = slice(r * half, (r + 1) * half)
            o_ref[rows, :] = _layer_norm(DN_ALPHA * y1_ref[rows, :] + mix[r], lw_ref[...], lb_ref[...])

    @pl.when(any_extra > 0)
    def _():
        acc_ref[...] = jnp.dot(gmat, ybuf[slot], preferred_element_type=F32)
        jc = lax.broadcasted_iota(jnp.int32, (TM_MOE, CHUNK_ROWS), 1).astype(F32)
        for ex in range(N_EXPERTS):
            start, lead = _lead(off_ref, i, ex)

            def chunk(c, carry, ex=ex, start=start, lead=lead):
                first = SLAB_ROWS + c * CHUNK_ROWS
                cp = pltpu.make_async_copy(
                    yh_ref.at[pl.ds(pl.multiple_of(ex * cap_rows + start + first, ROW_ALIGN), CHUNK_ROWS)],
                    obuf, osem.at[0])
                cp.start()
                cp.wait()
                gc = jnp.where(rank[:, ex:ex + 1] + lead.astype(F32) == jc + first.astype(F32),
                               gate[:, ex:ex + 1].astype(BF16).astype(F32), 0.0).astype(BF16)
                acc_ref[...] += jnp.dot(gc, obuf[...], preferred_element_type=F32)
                return carry

            lax.fori_loop(0, extra[ex], chunk, 0)
        o_ref[...] = _layer_norm(DN_ALPHA * y1_ref[...] + acc_ref[...], lw_ref[...], lb_ref[...])


def _combine(off, cnt, y1, rank_tm, gate_tm, yh, lw, lb, cap_rows):
    n = y1.shape[0]
    ntiles = n // TM_MOE
    row = lambda i, o, c: (i, 0)
    const = lambda i, o, c: (0, 0)
    gs = pltpu.PrefetchScalarGridSpec(
        num_scalar_prefetch=2,
        grid=(ntiles,),
        in_specs=[pl.BlockSpec((TM_MOE, D_MODEL), row),
                  pl.BlockSpec((TM_MOE, N_EXPERTS), row),
                  pl.BlockSpec((TM_MOE, N_EXPERTS), row),
                  pl.BlockSpec(memory_space=pl.ANY),
                  pl.BlockSpec((1, D_MODEL), const),
                  pl.BlockSpec((1, D_MODEL), const)],
        out_specs=pl.BlockSpec((TM_MOE, D_MODEL), row),
        scratch_shapes=[pltpu.VMEM((2, N_EXPERTS * SLAB_ROWS, D_MODEL), BF16),
                        pltpu.VMEM((CHUNK_ROWS, D_MODEL), BF16),
                        pltpu.VMEM((TM_MOE, D_MODEL), F32),
                        pltpu.SemaphoreType.DMA((2,)),
                        pltpu.SemaphoreType.DMA((1,))])
    return pl.pallas_call(
        functools.partial(_combine_kernel, cap_rows=cap_rows),
        grid_spec=gs,
        out_shape=jax.ShapeDtypeStruct((n, D_MODEL), F32),
        compiler_params=_cparams(("arbitrary",)),
        name="combine",
    )(off, cnt, y1, rank_tm, gate_tm, yh, lw, lb)


def _rope_tables(seq):
    pos = jnp.arange(seq, dtype=F32)[:, None]
    inv_m = 1.0 / (ROPE_BASE ** (jnp.arange(0, MLA_D_ROPE, 2, dtype=F32) / MLA_D_ROPE))
    am = pos * inv_m[None, :]
    one = jnp.ones((seq, MLA_D_NOPE), F32)
    zero_n = jnp.zeros((seq, MLA_D_NOPE), F32)
    zero_p = jnp.zeros((seq, HEAD_PAD - MLA_D_NOPE - MLA_D_ROPE), F32)
    cm = jnp.concatenate([one, jnp.cos(am), jnp.cos(am), zero_p], axis=1)
    sm = jnp.concatenate([zero_n, jnp.sin(am), jnp.sin(am), zero_p], axis=1)
    inv_r = 1.0 / (ROPE_BASE ** (jnp.arange(0, RET_DK, 2, dtype=F32) / RET_DK))
    ar = pos * inv_r[None, :]
    cosr = jnp.concatenate([jnp.cos(ar), jnp.cos(ar)], axis=1)
    sinr = jnp.concatenate([-jnp.sin(ar), jnp.sin(ar)], axis=1)
    return cm, sm, cosr, sinr


def _prep_weights(w_in, mla_w_uq, mla_w_ukv):
    half = MLA_D_ROPE // 2
    base = 4 * RET_WIDTH + MLA_Q_LORA + MLA_KV_LORA
    w_kr = w_in[:, base:base + MLA_D_ROPE]
    zn = jnp.zeros((D_MODEL, MLA_D_NOPE), F32)
    zp = jnp.zeros((D_MODEL, HEAD_PAD - MLA_D_NOPE - MLA_D_ROPE), F32)
    kr_p = jnp.concatenate([zn, w_kr, zp], axis=1)
    kr_rot = jnp.concatenate([zn, -w_kr[:, half:], w_kr[:, :half], zp], axis=1)
    w_all = jnp.concatenate([w_in[:, :base], kr_p, kr_rot], axis=1).astype(BF16)

    wq = mla_w_uq.reshape(MLA_Q_LORA, MLA_HEADS, MLA_D_NOPE + MLA_D_ROPE)
    nope, ropew = wq[..., :MLA_D_NOPE], wq[..., MLA_D_NOPE:]
    zq = jnp.zeros((MLA_Q_LORA, MLA_HEADS, HEAD_PAD - MLA_D_NOPE - MLA_D_ROPE), F32)
    q_p = jnp.concatenate([nope, ropew, zq], axis=-1).reshape(MLA_Q_LORA, -1)
    q_rot = jnp.concatenate([jnp.zeros_like(nope), -ropew[..., half:], ropew[..., :half], zq],
                            axis=-1).reshape(MLA_Q_LORA, -1)
    wq2 = jnp.concatenate([q_p, q_rot], axis=1).astype(BF16)

    wkv3 = mla_w_ukv.reshape(MLA_KV_LORA, MLA_HEADS, MLA_D_NOPE + MLA_D_V)
    k_p = jnp.concatenate([wkv3[..., :MLA_D_NOPE],
                           jnp.zeros((MLA_KV_LORA, MLA_HEADS, HEAD_PAD - MLA_D_NOPE), F32)],
                          axis=-1).reshape(MLA_KV_LORA, -1)
    v_w = wkv3[..., MLA_D_NOPE:].reshape(MLA_KV_LORA, -1)
    wkv = jnp.concatenate([k_p, v_w], axis=1).astype(BF16)
    return w_all, wq2, wkv


def _mixer_ln(x, wts):
    batch, seq, _ = x.shape
    x2d = x.reshape(batch * seq, D_MODEL)
    cm, sm, cosr, sinr = _rope_tables(seq)
    rq, rk, rv, rg, q, k, v = _inproj(x2d, seq, wts["w_all"], wts["wq2"], wts["wkv"],
                                      wts["qnw"], wts["kvnw"], cm, sm)
    ret = _retention(rq, rk, rv, rg, cosr, sinr, wts["dfw"], wts["dbw"], wts["gnw"], batch, seq)
    att = _attention(q, k, v, batch, seq)
    return _outproj(x2d, ret, att, wts["wo1"], wts["wo2"], wts["onw"], wts["ln1w"], wts["ln1b"],
                    wts["wrt"])


def _moe_ln(y1, logits, wts):
    n = y1.shape[0]
    cap = EC_CAPACITY_FACTOR * n // N_EXPERTS
    assert cap % TM_FFN == 0, (n, cap)
    cap_rows = cap + TM_FFN
    rank_em, gate_em, cnt_rep, off_rep = _route(logits, cap)
    cnt = cnt_rep[:, :, 0].astype(jnp.int32).reshape(-1)
    off = off_rep[:, :, 0].astype(jnp.int32).reshape(-1)
    to_tm = lambda a: jnp.transpose(a, (0, 2, 1)).reshape(n, N_EXPERTS)
    xg = _compact(off, cnt, y1, rank_em, cap, cap_rows)
    yh = _ffn(xg, wts["wg"], wts["wu"], wts["wd"], cap, cap_rows)
    return _combine(off, cnt, y1, to_tm(rank_em), to_tm(gate_em), yh, wts["ln2w"], wts["ln2b"],
                    cap_rows)


def _layer(x, wts):
    batch, seq, _ = x.shape
    y1, logits = _mixer_ln(x, wts)
    return _moe_ln(y1, logits, wts).reshape(batch, seq, D_MODEL)


def kernel(x_prompt, x_sample, w_in, ret_decay_fwd, ret_decay_bwd, ret_gn_w, mla_q_norm_w, mla_w_uq,
           mla_kv_norm_w, mla_w_ukv, mla_out_norm_w, w_out, ln1_w, ln1_b, w_router, w_gate, w_up,
           w_down, ln2_w, ln2_b):
    depth = w_in.shape[0]
    y_prompt, y_sample = x_prompt, x_sample
    for l in range(depth):
        w_all, wq2, wkv = _prep_weights(w_in[l], mla_w_uq[l], mla_w_ukv[l])
        rep = lambda a: jnp.broadcast_to(a.astype(F32)[:, None, None], (RET_HEADS, 1, LANES))
        wts = dict(
            w_all=w_all, wq2=wq2, wkv=wkv,
            qnw=mla_q_norm_w[l][None, :], kvnw=mla_kv_norm_w[l][None, :],
            dfw=rep(ret_decay_fwd[l]), dbw=rep(ret_decay_bwd[l]), gnw=ret_gn_w[l][None, :],
            wo1=w_out[l][:RET_WIDTH].astype(BF16), wo2=w_out[l][RET_WIDTH:].astype(BF16),
            onw=mla_out_norm_w[l][None, :], ln1w=ln1_w[l][None, :], ln1b=ln1_b[l][None, :],
            wrt=w_router[l].T,
            wg=w_gate[l], wu=w_up[l], wd=w_down[l],
            ln2w=ln2_w[l][None, :], ln2b=ln2_b[l][None, :])
        y_prompt = _layer(y_prompt, wts)
        y_sample = _layer(y_sample, wts)
    return (y_prompt, y_sample)
```

```python
import functools
import math

import numpy as np
import jax
import jax.numpy as jnp
from jax import lax
from jax.experimental import pallas as pl
from jax.experimental.pallas import tpu as pltpu

F32 = jnp.float32
BF16 = jnp.bfloat16

D_MODEL = 1024
RET_WIDTH = 512
RET_HEADS = 4
RET_DK = 128
RET_CHUNK = 128
MLA_HEADS = 8
MLA_D_NOPE = 64
MLA_D_ROPE = 32
MLA_D_V = 64
MLA_Q_LORA = 256
MLA_KV_LORA = 128
N_EXPERTS = 16
EC_CAPACITY_FACTOR = 2
ROPE_BASE = 10000.0
LN_EPS = 1e-5
RMS_EPS = 1e-6
DN_ALPHA = 2.0 ** 0.25

LANES = 128
HEAD_PAD = 128
PROJ_COLS = 4 * RET_WIDTH + MLA_Q_LORA + MLA_KV_LORA + 2 * LANES
VMEM_LIMIT = 56 * 1024 * 1024

TM_PROJ = 512
RET_STEP_CHUNKS = 16
TQ_ATT = 512
VT_CHUNKS = 2
VT_ROWS = 80
TK_ATT = VT_CHUNKS * TM_PROJ
KSUB_ATT = 256
TM_MOE = 256
ROW_ALIGN = 16
CHUNK_ROWS = 64
SLAB_ROWS = 64
SLAB_PIECES = ((0, 48), (48, 16))
THRESHOLD_STEPS = 64
NOT_ROUTED = -512.0
TM_FFN = 512
BLOCKS_PER_TILE = TM_MOE // LANES

_QK_SCALE = (MLA_D_NOPE + MLA_D_ROPE) ** -0.5 * math.log2(math.e)


def _cparams(sem):
    return pltpu.CompilerParams(dimension_semantics=sem, vmem_limit_bytes=VMEM_LIMIT)


def _inproj_kernel(x_ref, w_ref, wq_ref, wkv_ref, qnw_ref, kvnw_ref, cm_ref, sm_ref,
                   rq_ref, rk_ref, rv_ref, rg_ref, qt_ref, k_ref, vt_ref):
    xb = x_ref[...].astype(BF16)
    proj = jnp.dot(xb, w_ref[...], preferred_element_type=F32)
    rq_ref[...] = proj[:, 0:512]
    rk_ref[...] = proj[:, 512:1024]
    rv_ref[...] = proj[:, 1024:1536]
    rg_ref[...] = proj[:, 1536:2048]
    cm = cm_ref[...]
    sm = sm_ref[...]

    cq = proj[:, 2048:2304]
    cqn = cq * lax.rsqrt(jnp.mean(cq * cq, axis=-1, keepdims=True) + RMS_EPS) * qnw_ref[...]
    qq = jnp.dot(cqn.astype(BF16), wq_ref[...], preferred_element_type=F32)

    ckv = proj[:, 2304:2432]
    ckvn = ckv * lax.rsqrt(jnp.mean(ckv * ckv, axis=-1, keepdims=True) + RMS_EPS) * kvnw_ref[...]
    kv = jnp.dot(ckvn.astype(BF16), wkv_ref[...], preferred_element_type=F32)

    krope = proj[:, 2432:2560] * cm + proj[:, 2560:2688] * sm
    for h in range(MLA_HEADS):
        lo = h * HEAD_PAD
        qh = qq[:, lo:lo + HEAD_PAD] * cm + qq[:, 1024 + lo:1024 + lo + HEAD_PAD] * sm
        qt_ref[0, 0, lo:lo + HEAD_PAD, :] = (qh * _QK_SCALE).T.astype(BF16)
        k_ref[:, lo:lo + HEAD_PAD] = (kv[:, lo:lo + HEAD_PAD] + krope).astype(BF16)
    vt = kv[:, 1024:1536].T.astype(BF16)
    pad_rows = VT_ROWS - MLA_D_V
    ones_row = (lax.broadcasted_iota(jnp.int32, (pad_rows, vt.shape[1]), 0) == 0).astype(BF16)
    for h in range(MLA_HEADS):
        vt_ref[0, 0, h * VT_ROWS:h * VT_ROWS + MLA_D_V, :] = vt[h * MLA_D_V:(h + 1) * MLA_D_V, :]
        vt_ref[0, 0, h * VT_ROWS + MLA_D_V:(h + 1) * VT_ROWS, :] = ones_row


def _inproj(x2d, seq, w_all, wq2, wkv, qnw, kvnw, cm, sm):
    t = x2d.shape[0]
    tm = TM_PROJ
    nseq = seq // tm
    row = lambda i: (i, 0)
    const = lambda i: (0, 0)
    tab = lambda i: (i % nseq, 0)
    outs = [jax.ShapeDtypeStruct((t, RET_WIDTH), F32)] * 4 + [
        jax.ShapeDtypeStruct((t // seq, nseq, MLA_HEADS * HEAD_PAD, tm), BF16),
        jax.ShapeDtypeStruct((t, MLA_HEADS * HEAD_PAD), BF16),
        jax.ShapeDtypeStruct((t // seq, nseq, MLA_HEADS * VT_ROWS, tm), BF16)]
    return pl.pallas_call(
        _inproj_kernel,
        grid=(t // tm,),
        in_specs=[pl.BlockSpec((tm, D_MODEL), row),
                  pl.BlockSpec((D_MODEL, PROJ_COLS), const),
                  pl.BlockSpec((MLA_Q_LORA, 2 * MLA_HEADS * HEAD_PAD), const),
                  pl.BlockSpec((MLA_KV_LORA, MLA_HEADS * (HEAD_PAD + MLA_D_V)), const),
                  pl.BlockSpec((1, MLA_Q_LORA), const),
                  pl.BlockSpec((1, MLA_KV_LORA), const),
                  pl.BlockSpec((tm, LANES), tab),
                  pl.BlockSpec((tm, LANES), tab)],
        out_specs=[pl.BlockSpec((tm, RET_WIDTH), row)] * 4 + [
            pl.BlockSpec((1, 1, MLA_HEADS * HEAD_PAD, tm), lambda i: (i // nseq, i % nseq, 0, 0)),
            pl.BlockSpec((tm, MLA_HEADS * HEAD_PAD), row),
            pl.BlockSpec((1, 1, MLA_HEADS * VT_ROWS, tm), lambda i: (i // nseq, i % nseq, 0, 0))],
        out_shape=outs,
        compiler_params=_cparams(("parallel",)),
        name="inproj",
    )(x2d, w_all, wq2, wkv, qnw, kvnw, cm, sm)


def _log_sigmoid(x):
    return jnp.minimum(x, 0.0) - jnp.log(1.0 + jnp.exp(-jnp.abs(x)))


def _ret_kernel(q_ref, k_ref, v_ref, g_ref, cos_ref, sin_ref, df_ref, db_ref, gnw_ref,
                o_ref, sf_ref, sb_ref, sbst_ref, kvf_ref, kb_ref, vb_ref, *, nsteps):
    phase = pl.program_id(2)
    step = pl.program_id(3)
    c = RET_CHUNK
    lgf = _log_sigmoid(df_ref[0])
    lgb = _log_sigmoid(db_ref[0])
    rpos = lax.broadcasted_iota(jnp.int32, (c, c), 0).astype(F32)
    cpos = lax.broadcasted_iota(jnp.int32, (c, c), 1).astype(F32)
    tn = (((0,), (0,)), ((), ()))

    def rope(x, cs, sn):
        return x * cs + pltpu.roll(x, RET_DK // 2, axis=1) * sn

    @pl.when(jnp.logical_and(phase == 0, step == 0))
    def _():
        sb_ref[...] = jnp.zeros_like(sb_ref)

    @pl.when(jnp.logical_and(phase == 1, step == 0))
    def _():
        sf_ref[...] = jnp.zeros_like(sf_ref)

    @pl.when(phase == 0)
    def _():
        kwb = jnp.exp(lgb * rpos)
        kwf = jnp.exp(lgf * (float(c - 1) - rpos))
        cdb = jnp.exp(lgb * float(c))
        kvstep = nsteps - 1 - step
        kvs = {}
        for g in range(RET_STEP_CHUNKS):
            rows = slice(g * c, (g + 1) * c)
            idx = kvstep * RET_STEP_CHUNKS + g
            kr = rope(k_ref[rows, :], cos_ref[rows, :], sin_ref[rows, :]) * (RET_DK ** -0.5)
            vb = v_ref[rows, :].astype(BF16)
            kb_ref[idx] = kr.astype(BF16)
            vb_ref[idx] = vb
            kw = jnp.concatenate([(kr * kwb).astype(BF16), (kr * kwf).astype(BF16)], axis=1)
            kvs[g] = lax.dot_general(kw, vb, tn, preferred_element_type=F32)
        sb = sb_ref[...]
        for g in range(RET_STEP_CHUNKS - 1, -1, -1):
            idx = kvstep * RET_STEP_CHUNKS + g
            sbst_ref[idx] = sb.astype(BF16)
            kvf_ref[idx] = kvs[g][c:2 * c]
            sb = cdb * sb + kvs[g][0:c]
        sb_ref[...] = sb

    @pl.when(phase == 1)
    def _():
        diff = rpos - cpos
        dmat = jnp.where(diff >= 0, jnp.exp(lgf * jnp.maximum(diff, 0.0)),
                         jnp.exp(lgb * jnp.maximum(-diff, 0.0)))
        qwf = jnp.exp(lgf * (rpos + 1.0))
        qwb = jnp.exp(lgb * (float(c) - rpos))
        cdf = jnp.exp(lgf * float(c))
        gnw = gnw_ref[...]
        qrs, scs, outs = {}, {}, {}
        sfs = {0: sf_ref[...]}

        def scores(g):
            rows = slice(g * c, (g + 1) * c)
            idx = step * RET_STEP_CHUNKS + g
            qrs[g] = rope(q_ref[rows, :], cos_ref[rows, :], sin_ref[rows, :])
            scs[g] = lax.dot_general(qrs[g].astype(BF16), kb_ref[idx], (((1,), (1,)), ((), ())),
                                     preferred_element_type=F32)
            sfs[g + 1] = cdf * sfs[g] + kvf_ref[idx]

        def mix(g):
            idx = step * RET_STEP_CHUNKS + g
            qr = qrs.pop(g)
            lhs = jnp.concatenate([(scs.pop(g) * dmat).astype(BF16), (qr * qwf).astype(BF16),
                                   (qr * qwb).astype(BF16)], axis=1)
            rhs = jnp.concatenate([vb_ref[idx], sfs[g].astype(BF16), sbst_ref[idx]], axis=0)
            outs[g] = jnp.dot(lhs, rhs, preferred_element_type=F32)

        def finish(g):
            rows = slice(g * c, (g + 1) * c)
            o = outs.pop(g)
            mu = jnp.mean(o, axis=-1, keepdims=True)
            oc = o - mu
            var = jnp.mean(oc * oc, axis=-1, keepdims=True)
            on = oc * lax.rsqrt(var + LN_EPS) * gnw
            gt = g_ref[rows, :]
            o_ref[rows, :] = (gt / (1.0 + jnp.exp(-gt)) * on).astype(o_ref.dtype)

        for s in range(RET_STEP_CHUNKS + 4):
            if s - 4 >= 0:
                finish(s - 4)
            if 0 <= s - 2 < RET_STEP_CHUNKS:
                mix(s - 2)
            if s < RET_STEP_CHUNKS:
                scores(s)
        sf_ref[...] = sfs[RET_STEP_CHUNKS]


def _retention(rq, rk, rv, rg, cosr, sinr, dfw, dbw, gnw, batch, seq):
    t = rq.shape[0]
    ts = RET_STEP_CHUNKS * RET_CHUNK
    nsteps = seq // ts
    nchunks = seq // RET_CHUNK
    kvi = lambda p, i: jnp.where(p == 0, nsteps - 1 - i, 0)
    qi = lambda p, i: jnp.where(p == 0, 0, i)
    kv_map = lambda b, h, p, i: (b * nsteps + kvi(p, i), h)
    q_map = lambda b, h, p, i: (b * nsteps + qi(p, i), h)
    tab_map = lambda b, h, p, i: (jnp.where(p == 0, nsteps - 1 - i, i), 0)
    head3 = lambda b, h, p, i: (h, 0, 0)
    return pl.pallas_call(
        functools.partial(_ret_kernel, nsteps=nsteps),
        grid=(batch, RET_HEADS, 2, nsteps),
        in_specs=[pl.BlockSpec((ts, RET_DK), q_map),
                  pl.BlockSpec((ts, RET_DK), kv_map),
                  pl.BlockSpec((ts, RET_DK), kv_map),
                  pl.BlockSpec((ts, RET_DK), q_map),
                  pl.BlockSpec((ts, RET_DK), tab_map),
                  pl.BlockSpec((ts, RET_DK), tab_map),
                  pl.BlockSpec((1, 1, LANES), head3),
                  pl.BlockSpec((1, 1, LANES), head3),
                  pl.BlockSpec((1, RET_DK), lambda b, h, p, i: (0, h))],
        out_specs=pl.BlockSpec((ts, RET_DK), q_map),
        out_shape=jax.ShapeDtypeStruct((t, RET_WIDTH), BF16),
        scratch_shapes=[pltpu.VMEM((RET_DK, RET_DK), F32),
                        pltpu.VMEM((RET_DK, RET_DK), F32),
                        pltpu.VMEM((nchunks, RET_DK, RET_DK), BF16),
                        pltpu.VMEM((nchunks, RET_DK, RET_DK), F32),
                        pltpu.VMEM((nchunks, RET_CHUNK, RET_DK), BF16),
                        pltpu.VMEM((nchunks, RET_CHUNK, RET_DK), BF16)],
        compiler_params=_cparams(("parallel", "parallel", "arbitrary", "arbitrary")),
        name="retention",
    )(rq, rk, rv, rg, cosr, sinr, dfw, dbw, gnw)


def _attn_kernel(qt_ref, k_ref, vt_ref, o_ref, ste_ref, sto_ref, acc_ref, *, seq):
    nkv = seq // TK_ATT
    npair = (seq // TQ_ATT) * nkv
    nt = (((1,), (1,)), ((), ()))
    nsub = TK_ATT // KSUB_ATT
    sub_per_chunk = TM_PROJ // KSUB_ATT
    neg_inf = jnp.full((1, TQ_ATT), -jnp.inf, F32)

    def score_steps(t, st_ref, tile_max):
        t = jnp.minimum(t, npair - 1)
        qi = t // nkv
        ks = pl.multiple_of((t % nkv) * TK_ATT, TK_ATT)

        def step(h, s):
            def run():
                hs = slice(h * HEAD_PAD, (h + 1) * HEAD_PAD)
                sub = slice(s * KSUB_ATT, (s + 1) * KSUB_ATT)
                st_ref[h, sub, :] = jnp.dot(k_ref[pl.ds(ks + s * KSUB_ATT, KSUB_ATT), hs], qt_ref[0, qi, hs, :],
                                            preferred_element_type=F32)
                top = jnp.max(st_ref[h, sub, :], axis=0, keepdims=True)
                tile_max[h] = top if s == 0 else jnp.maximum(tile_max[h], top)
            return run
        return [step(h, s) for h in range(2) for s in range(nsub)]

    def softmax_steps(t, st_ref, run_max, tile_max, out):
        kj = t % nkv
        steps = []
        for h in range(2):
            state = {}

            def start(h=h, state=state):
                m = jnp.where(kj == 0, neg_inf, run_max[h])
                state["mn"] = jnp.maximum(m, tile_max[h])
                state["alpha"] = jnp.exp2(m - state["mn"])

            def part(s, h=h, state=state):
                def run():
                    x = st_ref[h, s * KSUB_ATT:(s + 1) * KSUB_ATT, :] - state["mn"]
                    pb = jnp.exp2(x.astype(BF16))
                    c, r = divmod(s, sub_per_chunk)
                    pv = jnp.dot(vt_ref[0, kj * VT_CHUNKS + c, h * VT_ROWS:(h + 1) * VT_ROWS,
                                        r * KSUB_ATT:(r + 1) * KSUB_ATT],
                                 pb, preferred_element_type=F32)
                    state["pv"] = pv if s == 0 else state["pv"] + pv
                    if s == nsub - 1:
                        old = jnp.where(kj == 0, 0.0, acc_ref[h])
                        acc_ref[h] = state["alpha"] * old + state["pv"]
                        out[h] = state["mn"]
                return run
            steps += [start] + [part(s) for s in range(nsub)]
        return steps

    def interleave(mxu_steps, vpu_steps):
        n = max(len(mxu_steps), len(vpu_steps))
        for i in range(n):
            if i < len(mxu_steps):
                mxu_steps[i]()
            if i < len(vpu_steps):
                vpu_steps[i]()

    def body(jj, carry):
        run_max, max_even = carry
        t0 = 2 * jj
        mid, out, max_odd, max_next = {}, {}, {}, {}
        interleave(score_steps(t0 + 1, sto_ref, max_odd), softmax_steps(t0, ste_ref, run_max, max_even, mid))
        interleave(score_steps(t0 + 2, ste_ref, max_next),
                   softmax_steps(t0 + 1, sto_ref, (mid[0], mid[1]), max_odd, out))

        @pl.when((t0 + 1) % nkv == nkv - 1)
        def _():
            qi = t0 // nkv
            for h in range(2):
                acc = acc_ref[h]
                o_ref[0, qi, h * MLA_D_V:(h + 1) * MLA_D_V, :] = acc[0:MLA_D_V] / acc[MLA_D_V:MLA_D_V + 1]
        return (out[0], out[1]), (max_next[0], max_next[1])

    acc_ref[...] = jnp.zeros_like(acc_ref)
    max_first = {}
    for run in score_steps(0, ste_ref, max_first):
        run()
    lax.fori_loop(0, npair // 2, body, ((neg_inf, neg_inf), (max_first[0], max_first[1])))


def _attention(qt, k, vt, batch, seq):
    nq = seq // TQ_ATT
    nvt = seq // TM_PROJ
    assert (seq // TK_ATT) % 2 == 0, seq
    return pl.pallas_call(
        functools.partial(_attn_kernel, seq=seq),
        grid=(batch, MLA_HEADS // 2),
        in_specs=[pl.BlockSpec((1, nq, 2 * HEAD_PAD, TQ_ATT), lambda b, hp: (b, 0, hp, 0)),
                  pl.BlockSpec((seq, 2 * HEAD_PAD), lambda b, hp: (b, hp)),
                  pl.BlockSpec((1, nvt, 2 * VT_ROWS, TM_PROJ), lambda b, hp: (b, 0, hp, 0))],
        out_specs=pl.BlockSpec((1, nq, 2 * MLA_D_V, TQ_ATT), lambda b, hp: (b, 0, hp, 0)),
        out_shape=jax.ShapeDtypeStruct((batch, nq, MLA_HEADS * MLA_D_V, TQ_ATT), F32),
        scratch_shapes=[pltpu.VMEM((2, TK_ATT, TQ_ATT), F32),
                        pltpu.VMEM((2, TK_ATT, TQ_ATT), F32),
                        pltpu.VMEM((2, VT_ROWS, TQ_ATT), F32)],
        compiler_params=_cparams(("parallel", "parallel")),
        name="attention",
    )(qt, k, vt)


def _layer_norm(z, w, b):
    mu = jnp.mean(z, axis=-1, keepdims=True)
    zc = z - mu
    var = jnp.mean(zc * zc, axis=-1, keepdims=True)
    return zc * lax.rsqrt(var + LN_EPS) * w + b


def _split_bf16(a):
    hi = a.astype(BF16)
    lo = (a - hi.astype(F32)).astype(BF16)
    return hi, lo


def _outproj_kernel(x_ref, ret_ref, att_ref, wo1_ref, wo2_ref, onw_ref, lw_ref, lb_ref, wr_ref,
                    y_ref, lg_ref):
    wh, wl = _split_bf16(wr_ref[...])
    nt = (((1,), (1,)), ((), ()))
    nsub = x_ref.shape[0] // LANES
    pre = {}

    def project(r):
        rows = slice(r * LANES, (r + 1) * LANES)
        att = att_ref[0, 0, :, rows].T
        an = att * lax.rsqrt(jnp.mean(att * att, axis=-1, keepdims=True) + RMS_EPS) * onw_ref[...]
        mix = jnp.dot(ret_ref[rows, :], wo1_ref[...], preferred_element_type=F32)
        mix += jnp.dot(an.astype(BF16), wo2_ref[...], preferred_element_type=F32)
        pre[r] = DN_ALPHA * x_ref[rows, :] + mix

    def normalize(r):
        rows = slice(r * LANES, (r + 1) * LANES)
        y = _layer_norm(pre.pop(r), lw_ref[...], lb_ref[...])
        y_ref[rows, :] = y
        yh, yl = _split_bf16(y)
        lt = lax.dot_general(wh, yh, nt, preferred_element_type=F32)
        lt += lax.dot_general(wh, yl, nt, preferred_element_type=F32)
        lt += lax.dot_general(wl, yh, nt, preferred_element_type=F32)
        lg_ref[r] = lt

    for r in range(nsub + 1):
        if r < nsub:
            project(r)
        if r >= 1:
            normalize(r - 1)


def _outproj(x2d, ret, att_t, wo1, wo2, onw, lw, lb, wrt):
    t = x2d.shape[0]
    tm = TM_PROJ
    nseq = att_t.shape[1]
    assert att_t.shape[3] == tm
    row = lambda i: (i, 0)
    const = lambda i: (0, 0)
    return pl.pallas_call(
        _outproj_kernel,
        grid=(t // tm,),
        in_specs=[pl.BlockSpec((tm, D_MODEL), row),
                  pl.BlockSpec((tm, RET_WIDTH), row),
                  pl.BlockSpec((1, 1, RET_WIDTH, tm), lambda i: (i // nseq, i % nseq, 0, 0)),
                  pl.BlockSpec((RET_WIDTH, D_MODEL), const),
                  pl.BlockSpec((RET_WIDTH, D_MODEL), const),
                  pl.BlockSpec((1, RET_WIDTH), const),
                  pl.BlockSpec((1, D_MODEL), const),
                  pl.BlockSpec((1, D_MODEL), const),
                  pl.BlockSpec((N_EXPERTS, D_MODEL), const)],
        out_specs=[pl.BlockSpec((tm, D_MODEL), row),
                   pl.BlockSpec((tm // LANES, N_EXPERTS, LANES), lambda i: (i, 0, 0))],
        out_shape=[jax.ShapeDtypeStruct((t, D_MODEL), F32),
                   jax.ShapeDtypeStruct((t // LANES, N_EXPERTS, LANES), F32)],
        compiler_params=_cparams(("parallel",)),
        name="outproj",
    )(x2d, ret, att_t, wo1, wo2, onw, lw, lb, wrt)


def _route_kernel(lg_ref, rank_ref, gate_ref, cnt_ref, off_ref, blk_ref, boff_ref, *, cap, nb):
    e = N_EXPERTS
    ntiles = nb // BLOCKS_PER_TILE
    lg = lg_ref[...]
    ex = jnp.exp(lg - jnp.max(lg, axis=1, keepdims=True))
    aff = ex / jnp.sum(ex, axis=1, keepdims=True)
    gate_ref[...] = aff

    def count(mask):
        part = jnp.sum(mask.astype(F32), axis=0)
        return jnp.broadcast_to(jnp.sum(part, axis=1, keepdims=True), (e, LANES))

    def bisect(_, lohi):
        lo, hi = lohi
        mid = 0.5 * (lo + hi)
        ok = count(aff >= mid[None]) >= float(cap)
        return jnp.where(ok, mid, lo), jnp.where(ok, hi, mid)

    lo0 = jnp.zeros((e, LANES), F32)
    hi0 = jnp.full((e, LANES), 2.0, F32)
    lo, hi = lax.fori_loop(0, THRESHOLD_STEPS, bisect, (lo0, hi0))
    gt = aff >= hi[None]
    eq = jnp.logical_and(aff >= lo[None], jnp.logical_not(gt))
    need = float(cap) - count(gt)

    ii = lax.broadcasted_iota(jnp.int32, (LANES, LANES), 0)
    jj = lax.broadcasted_iota(jnp.int32, (LANES, LANES), 1)
    upper = (ii <= jj).astype(BF16)
    ones = jnp.ones((LANES, LANES), BF16)

    def block_scan(mask):
        m2 = mask.astype(BF16).reshape(nb * e, LANES)
        incl = jnp.dot(m2, upper, preferred_element_type=F32).reshape(nb, e, LANES)
        tot = jnp.dot(m2, ones, preferred_element_type=F32).reshape(nb, e, LANES)
        return incl, tot

    def leading_excl_scan(src_ref, dst_ref, n):
        def step(j, carry):
            dst_ref[j] = carry
            return carry + src_ref[j]
        return lax.fori_loop(0, n, step, jnp.zeros((e, LANES), F32))

    eqf = eq.astype(F32)
    incl, tot = block_scan(eq)
    blk_ref[...] = tot
    leading_excl_scan(blk_ref, boff_ref, nb)
    eq_before = incl - eqf + boff_ref[...]
    sel = jnp.logical_or(gt, jnp.logical_and(eq, eq_before < need[None]))

    self32 = sel.astype(F32)
    incl, tot = block_scan(sel)
    local = (incl - self32).reshape(ntiles, BLOCKS_PER_TILE, e, LANES)
    tot4 = tot.reshape(ntiles, BLOCKS_PER_TILE, e, LANES)
    ranks = [local[:, 0]]
    run = tot4[:, 0]
    for b in range(1, BLOCKS_PER_TILE):
        ranks.append(local[:, b] + run)
        run = run + tot4[:, b]
    rank = jnp.stack(ranks, axis=1).reshape(nb, e, LANES)
    rank_ref[...] = jnp.where(sel, rank, NOT_ROUTED)
    cnt_ref[...] = run
    blk_ref[0:ntiles] = run
    leading_excl_scan(blk_ref, boff_ref, ntiles)
    off_ref[...] = boff_ref[0:ntiles]


def _route(logits, cap):
    nb = logits.shape[0]
    ntiles = nb // BLOCKS_PER_TILE
    big = jax.ShapeDtypeStruct((nb, N_EXPERTS, LANES), F32)
    small = jax.ShapeDtypeStruct((ntiles, N_EXPERTS, LANES), F32)
    return pl.pallas_call(
        functools.partial(_route_kernel, cap=cap, nb=nb),
        out_shape=[big, big, small, small],
        scratch_shapes=[pltpu.VMEM((nb, N_EXPERTS, LANES), F32),
                        pltpu.VMEM((nb, N_EXPERTS, LANES), F32)],
        compiler_params=pltpu.CompilerParams(vmem_limit_bytes=VMEM_LIMIT),
        name="route",
    )(logits)


def _lead(off_ref, tile, ex):
    pos = off_ref[tile * N_EXPERTS + ex]
    lead = lax.rem(pos, ROW_ALIGN)
    return pos - lead, lead


def _overflow_chunks(lead, cnt):
    return lax.div(jnp.maximum(lead + cnt - SLAB_ROWS, 0) + (CHUNK_ROWS - 1), CHUNK_ROWS)


def _compact_kernel(off_ref, cnt_ref, y_ref, rank_ref, xg_ref, p_ref, pc_ref, st_ref, ost_ref,
                    carry_ref, zero_ref, sem, osem, *, cap, cap_rows):
    i = pl.program_id(0)
    nt = pl.num_programs(0)
    slot = i % 2

    def slab(tile, slt, ex, fn):
        start, lead = _lead(off_ref, tile, ex)
        used = lead + cnt_ref[tile * N_EXPERTS + ex]
        for first, rows in SLAB_PIECES:
            def piece(first=first, rows=rows):
                fn(pltpu.make_async_copy(
                    st_ref.at[slt, pl.ds(ex * SLAB_ROWS + first, rows)],
                    xg_ref.at[pl.ds(pl.multiple_of(ex * cap_rows + start + first, ROW_ALIGN), rows)],
                    sem.at[slt]))
            if first == 0:
                piece()
            else:
                pl.when(used > first)(piece)

    @pl.when(i == 0)
    def _():
        carry_ref[...] = jnp.zeros_like(carry_ref)
        zero_ref[...] = jnp.zeros_like(zero_ref)
        tails = [pltpu.make_async_copy(zero_ref, xg_ref.at[pl.ds(ex * cap_rows + cap, cap_rows - cap)],
                                       osem.at[0]) for ex in range(N_EXPERTS)]
        for cp in tails:
            cp.start()
        for cp in tails:
            cp.wait()

    xb = y_ref[...].astype(BF16)
    rank = jnp.concatenate([rank_ref[b] for b in range(BLOCKS_PER_TILE)], axis=1)
    jslab = lax.broadcasted_iota(jnp.int32, (SLAB_ROWS, TM_MOE), 0).astype(F32)
    jgrp = lax.broadcasted_iota(jnp.int32, (ROW_ALIGN, TM_MOE), 0).astype(F32)
    last_group = []
    for ex in range(N_EXPERTS):
        _, lead = _lead(off_ref, i, ex)
        sh = rank[ex:ex + 1, :] + lead.astype(F32)
        grp = lax.div(lead + cnt_ref[i * N_EXPERTS + ex], ROW_ALIGN)
        p_ref[ex * SLAB_ROWS:(ex + 1) * SLAB_ROWS, :] = (sh == jslab).astype(BF16)
        pc_ref[ex * ROW_ALIGN:(ex + 1) * ROW_ALIGN, :] = (
            sh - (grp * ROW_ALIGN).astype(F32) == jgrp).astype(BF16)
        last_group.append(grp)
    rows = jnp.dot(p_ref[...], xb, preferred_element_type=F32)
    tail = jnp.dot(pc_ref[...], xb, preferred_element_type=F32)
    for ex in range(N_EXPERTS):
        lo = ex * SLAB_ROWS
        old = carry_ref[ex]
        st_ref[slot, lo:lo + ROW_ALIGN, :] = (rows[lo:lo + ROW_ALIGN] + old).astype(BF16)
        st_ref[slot, lo + ROW_ALIGN:lo + SLAB_ROWS, :] = rows[lo + ROW_ALIGN:lo + SLAB_ROWS].astype(BF16)
        carry_ref[ex] = tail[ex * ROW_ALIGN:(ex + 1) * ROW_ALIGN] + jnp.where(last_group[ex] == 0, old, 0.0)

    @pl.when(i >= 1)
    def _():
        for ex in range(N_EXPERTS):
            slab(i - 1, 1 - slot, ex, lambda cp: cp.wait())

    for ex in range(N_EXPERTS):
        slab(i, slot, ex, lambda cp: cp.start())

    extra = [_overflow_chunks(_lead(off_ref, i, ex)[1], cnt_ref[i * N_EXPERTS + ex]) for ex in range(N_EXPERTS)]

    @pl.when(functools.reduce(lambda a, b: a + b, extra) > 0)
    def _():
        xo = y_ref[...].astype(BF16)
        jchunk = lax.broadcasted_iota(jnp.int32, (CHUNK_ROWS, TM_MOE), 0).astype(F32)
        for ex in range(N_EXPERTS):
            start, lead = _lead(off_ref, i, ex)
            sh = (jnp.concatenate([rank_ref[b, ex:ex + 1, :] for b in range(BLOCKS_PER_TILE)], axis=1)
                  + lead.astype(F32))

            def chunk(c, carry, ex=ex, start=start, sh=sh):
                first = SLAB_ROWS + c * CHUNK_ROWS
                pc = (sh == jchunk + first.astype(F32)).astype(BF16)
                ost_ref[...] = jnp.dot(pc, xo, preferred_element_type=F32).astype(BF16)
                cp = pltpu.make_async_copy(
                    ost_ref,
                    xg_ref.at[pl.ds(pl.multiple_of(ex * cap_rows + start + first, ROW_ALIGN), CHUNK_ROWS)],
                    osem.at[0])
                cp.start()
                cp.wait()
                return carry

            lax.fori_loop(0, extra[ex], chunk, 0)

    @pl.when(i == nt - 1)
    def _():
        for ex in range(N_EXPERTS):
            slab(i, slot, ex, lambda cp: cp.wait())


def _compact(off, cnt, y1, rank_em, cap, cap_rows):
    n = y1.shape[0]
    ntiles = n // TM_MOE
    gs = pltpu.PrefetchScalarGridSpec(
        num_scalar_prefetch=2,
        grid=(ntiles,),
        in_specs=[pl.BlockSpec((TM_MOE, D_MODEL), lambda i, o, c: (i, 0)),
                  pl.BlockSpec((BLOCKS_PER_TILE, N_EXPERTS, LANES), lambda i, o, c: (i, 0, 0))],
        out_specs=pl.BlockSpec(memory_space=pl.ANY),
        scratch_shapes=[pltpu.VMEM((N_EXPERTS * SLAB_ROWS, TM_MOE), BF16),
                        pltpu.VMEM((N_EXPERTS * ROW_ALIGN, TM_MOE), BF16),
                        pltpu.VMEM((2, N_EXPERTS * SLAB_ROWS, D_MODEL), BF16),
                        pltpu.VMEM((CHUNK_ROWS, D_MODEL), BF16),
                        pltpu.VMEM((N_EXPERTS, ROW_ALIGN, D_MODEL), F32),
                        pltpu.VMEM((cap_rows - cap, D_MODEL), BF16),
                        pltpu.SemaphoreType.DMA((2,)),
                        pltpu.SemaphoreType.DMA((1,))])
    return pl.pallas_call(
        functools.partial(_compact_kernel, cap=cap, cap_rows=cap_rows),
        grid_spec=gs,
        out_shape=jax.ShapeDtypeStruct((N_EXPERTS * cap_rows, D_MODEL), BF16),
        compiler_params=_cparams(("arbitrary",)),
        name="compact",
    )(off, cnt, y1, rank_em)


def _ffn_kernel(x_ref, wg_hbm, wu_hbm, wd_hbm, y_ref, wf_ref, wgb_ref, wub_ref, wdb_ref, sem, *, nvalid):
    ex = pl.program_id(0)
    j = pl.program_id(1)
    slot = ex % 2

    def weights(e, slt, fn):
        for k, src in enumerate((wg_hbm, wu_hbm, wd_hbm)):
            fn(pltpu.make_async_copy(src.at[e], wf_ref.at[slt, k], sem.at[slt]))

    def swiglu(wg, wu, wd):
        x = x_ref[...]
        hg = jnp.dot(x, wg, preferred_element_type=F32)
        hu = jnp.dot(x, wu, preferred_element_type=F32)
        h = (hg / (1.0 + jnp.exp(-hg)) * hu).astype(BF16)
        y_ref[...] = jnp.dot(h, wd, preferred_element_type=F32).astype(BF16)

    @pl.when(j == 0)
    def _():
        @pl.when(ex == 0)
        def _():
            weights(0, 0, lambda cp: cp.start())

        @pl.when(ex + 1 < N_EXPERTS)
        def _():
            weights(ex + 1, 1 - slot, lambda cp: cp.start())

        weights(ex, slot, lambda cp: cp.wait())
        wg = wf_ref[slot, 0].astype(BF16)
        wu = wf_ref[slot, 1].astype(BF16)
        wd = wf_ref[slot, 2].astype(BF16)
        wgb_ref[...] = wg
        wub_ref[...] = wu
        wdb_ref[...] = wd
        swiglu(wg, wu, wd)

    @pl.when(jnp.logical_and(j > 0, j < nvalid))
    def _():
        swiglu(wgb_ref[...], wub_ref[...], wdb_ref[...])

    @pl.when(j >= nvalid)
    def _():
        y_ref[...] = jnp.zeros_like(y_ref)


def _ffn(xg, wg, wu, wd, cap, cap_rows):
    nvalid = cap // TM_FFN
    nt = cap_rows // TM_FFN
    return pl.pallas_call(
        functools.partial(_ffn_kernel, nvalid=nvalid),
        grid=(N_EXPERTS, nt),
        in_specs=[pl.BlockSpec((TM_FFN, D_MODEL), lambda ex, j: (ex * nt + jnp.minimum(j, nvalid - 1), 0)),
                  pl.BlockSpec(memory_space=pl.ANY),
                  pl.BlockSpec(memory_space=pl.ANY),
                  pl.BlockSpec(memory_space=pl.ANY)],
        out_specs=pl.BlockSpec((TM_FFN, D_MODEL), lambda ex, j: (ex * nt + j, 0)),
        out_shape=jax.ShapeDtypeStruct((N_EXPERTS * cap_rows, D_MODEL), BF16),
        scratch_shapes=[pltpu.VMEM((2, 3, D_MODEL, D_MODEL), F32)] + [pltpu.VMEM((D_MODEL, D_MODEL), BF16)] * 3
        + [pltpu.SemaphoreType.DMA((2,))],
        compiler_params=_cparams(("arbitrary", "arbitrary")),
        name="ffn",
    )(xg, wg, wu, wd)


def _combine_kernel(off_ref, cnt_ref, y1_ref, rank_ref, gate_ref, yh_ref, lw_ref, lb_ref,
                    o_ref, ybuf, obuf, acc_ref, sem, osem, *, cap_rows):
    i = pl.program_id(0)
    nt = pl.num_programs(0)
    slot = i % 2

    ncol = N_EXPERTS * SLAB_ROWS

    def fetch(tile, slt, fn):
        for ex in range(N_EXPERTS):
            start, lead = _lead(off_ref, tile, ex)
            used = lead + cnt_ref[tile * N_EXPERTS + ex]
            for first, rows in SLAB_PIECES:
                def piece(ex=ex, start=start, first=first, rows=rows):
                    fn(pltpu.make_async_copy(
                        yh_ref.at[pl.ds(pl.multiple_of(ex * cap_rows + start + first, ROW_ALIGN), rows)],
                        ybuf.at[slt, pl.ds(ex * SLAB_ROWS + first, rows)],
                        sem.at[slt]))
                if first == 0:
                    piece()
                else:
                    pl.when(used > first)(piece)

    @pl.when(i == 0)
    def _():
        ybuf[...] = jnp.zeros_like(ybuf)
        fetch(0, 0, lambda cp: cp.start())

    rank = rank_ref[...]
    gate = gate_ref[...]
    er = lax.broadcasted_iota(jnp.int32, (N_EXPERTS, ncol), 0)
    ec = lax.broadcasted_iota(jnp.int32, (N_EXPERTS, ncol), 1)
    expand = jnp.logical_and(ec >= er * SLAB_ROWS, ec < (er + 1) * SLAB_ROWS).astype(BF16)
    rexp = jnp.dot(rank.astype(BF16), expand, preferred_element_type=F32)
    gexp = jnp.dot(gate.astype(BF16), expand, preferred_element_type=F32)
    col = lax.broadcasted_iota(jnp.int32, (1, ncol), 1)
    want = jnp.zeros((1, ncol), F32)
    for ex in range(N_EXPERTS):
        _, lead = _lead(off_ref, i, ex)
        inside = jnp.logical_and(col >= ex * SLAB_ROWS, col < (ex + 1) * SLAB_ROWS)
        want = jnp.where(inside, (col - ex * SLAB_ROWS - lead).astype(F32), want)
    gmat = jnp.where(rexp == want, gexp, 0.0).astype(BF16)

    @pl.when(i + 1 < nt)
    def _():
        fetch(i + 1, 1 - slot, lambda cp: cp.start())

    fetch(i, slot, lambda cp: cp.wait())
    extra = [_overflow_chunks(_lead(off_ref, i, ex)[1], cnt_ref[i * N_EXPERTS + ex]) for ex in range(N_EXPERTS)]
    any_extra = functools.reduce(lambda a, b: a + b, extra)

    @pl.when(any_extra == 0)
    def _():
        half = TM_MOE // 2
        mix = [jnp.dot(gmat[r * half:(r + 1) * half], ybuf[slot], preferred_element_type=F32) for r in range(2)]
        for r in range(2):
            rows = slice(r * half, (r + 1) * half)
            o_ref[rows, :] = _layer_norm(DN_ALPHA * y1_ref[rows, :] + mix[r], lw_ref[...], lb_ref[...])

    @pl.when(any_extra > 0)
    def _():
        acc_ref[...] = jnp.dot(gmat, ybuf[slot], preferred_element_type=F32)
        jc = lax.broadcasted_iota(jnp.int32, (TM_MOE, CHUNK_ROWS), 1).astype(F32)
        for ex in range(N_EXPERTS):
            start, lead = _lead(off_ref, i, ex)

            def chunk(c, carry, ex=ex, start=start, lead=lead):
                first = SLAB_ROWS + c * CHUNK_ROWS
                cp = pltpu.make_async_copy(
                    yh_ref.at[pl.ds(pl.multiple_of(ex * cap_rows + start + first, ROW_ALIGN), CHUNK_ROWS)],
                    obuf, osem.at[0])
                cp.start()
                cp.wait()
                gc = jnp.where(rank[:, ex:ex + 1] + lead.astype(F32) == jc + first.astype(F32),
                               gate[:, ex:ex + 1].astype(BF16).astype(F32), 0.0).astype(BF16)
                acc_ref[...] += jnp.dot(gc, obuf[...], preferred_element_type=F32)
                return carry

            lax.fori_loop(0, extra[ex], chunk, 0)
        o_ref[...] = _layer_norm(DN_ALPHA * y1_ref[...] + acc_ref[...], lw_ref[...], lb_ref[...])


def _combine(off, cnt, y1, rank_tm, gate_tm, yh, lw, lb, cap_rows):
    n = y1.shape[0]
    ntiles = n // TM_MOE
    row = lambda i, o, c: (i, 0)
    const = lambda i, o, c: (0, 0)
    gs = pltpu.PrefetchScalarGridSpec(
        num_scalar_prefetch=2,
        grid=(ntiles,),
        in_specs=[pl.BlockSpec((TM_MOE, D_MODEL), row),
                  pl.BlockSpec((TM_MOE, N_EXPERTS), row),
                  pl.BlockSpec((TM_MOE, N_EXPERTS), row),
                  pl.BlockSpec(memory_space=pl.ANY),
                  pl.BlockSpec((1, D_MODEL), const),
                  pl.BlockSpec((1, D_MODEL), const)],
        out_specs=pl.BlockSpec((TM_MOE, D_MODEL), row),
        scratch_shapes=[pltpu.VMEM((2, N_EXPERTS * SLAB_ROWS, D_MODEL), BF16),
                        pltpu.VMEM((CHUNK_ROWS, D_MODEL), BF16),
                        pltpu.VMEM((TM_MOE, D_MODEL), F32),
                        pltpu.SemaphoreType.DMA((2,)),
                        pltpu.SemaphoreType.DMA((1,))])
    return pl.pallas_call(
        functools.partial(_combine_kernel, cap_rows=cap_rows),
        grid_spec=gs,
        out_shape=jax.ShapeDtypeStruct((n, D_MODEL), F32),
        compiler_params=_cparams(("arbitrary",)),
        name="combine",
    )(off, cnt, y1, rank_tm, gate_tm, yh, lw, lb)


def _rope_tables(seq):
    pos = jnp.arange(seq, dtype=F32)[:, None]
    inv_m = 1.0 / (ROPE_BASE ** (jnp.arange(0, MLA_D_ROPE, 2, dtype=F32) / MLA_D_ROPE))
    am = pos * inv_m[None, :]
    one = jnp.ones((seq, MLA_D_NOPE), F32)
    zero_n = jnp.zeros((seq, MLA_D_NOPE), F32)
    zero_p = jnp.zeros((seq, HEAD_PAD - MLA_D_NOPE - MLA_D_ROPE), F32)
    cm = jnp.concatenate([one, jnp.cos(am), jnp.cos(am), zero_p], axis=1)
    sm = jnp.concatenate([zero_n, jnp.sin(am), jnp.sin(am), zero_p], axis=1)
    inv_r = 1.0 / (ROPE_BASE ** (jnp.arange(0, RET_DK, 2, dtype=F32) / RET_DK))
    ar = pos * inv_r[None, :]
    cosr = jnp.concatenate([jnp.cos(ar), jnp.cos(ar)], axis=1)
    sinr = jnp.concatenate([-jnp.sin(ar), jnp.sin(ar)], axis=1)
    return cm, sm, cosr, sinr


def _prep_weights(w_in, mla_w_uq, mla_w_ukv):
    half = MLA_D_ROPE // 2
    base = 4 * RET_WIDTH + MLA_Q_LORA + MLA_KV_LORA
    w_kr = w_in[:, base:base + MLA_D_ROPE]
    zn = jnp.zeros((D_MODEL, MLA_D_NOPE), F32)
    zp = jnp.zeros((D_MODEL, HEAD_PAD - MLA_D_NOPE - MLA_D_ROPE), F32)
    kr_p = jnp.concatenate([zn, w_kr, zp], axis=1)
    kr_rot = jnp.concatenate([zn, -w_kr[:, half:], w_kr[:, :half], zp], axis=1)
    w_all = jnp.concatenate([w_in[:, :base], kr_p, kr_rot], axis=1).astype(BF16)

    wq = mla_w_uq.reshape(MLA_Q_LORA, MLA_HEADS, MLA_D_NOPE + MLA_D_ROPE)
    nope, ropew = wq[..., :MLA_D_NOPE], wq[..., MLA_D_NOPE:]
    zq = jnp.zeros((MLA_Q_LORA, MLA_HEADS, HEAD_PAD - MLA_D_NOPE - MLA_D_ROPE), F32)
    q_p = jnp.concatenate([nope, ropew, zq], axis=-1).reshape(MLA_Q_LORA, -1)
    q_rot = jnp.concatenate([jnp.zeros_like(nope), -ropew[..., half:], ropew[..., :half], zq],
                            axis=-1).reshape(MLA_Q_LORA, -1)
    wq2 = jnp.concatenate([q_p, q_rot], axis=1).astype(BF16)

    wkv3 = mla_w_ukv.reshape(MLA_KV_LORA, MLA_HEADS, MLA_D_NOPE + MLA_D_V)
    k_p = jnp.concatenate([wkv3[..., :MLA_D_NOPE],
                           jnp.zeros((MLA_KV_LORA, MLA_HEADS, HEAD_PAD - MLA_D_NOPE), F32)],
                          axis=-1).reshape(MLA_KV_LORA, -1)
    v_w = wkv3[..., MLA_D_NOPE:].reshape(MLA_KV_LORA, -1)
    wkv = jnp.concatenate([k_p, v_w], axis=1).astype(BF16)
    return w_all, wq2, wkv


def _mixer_ln(x, wts):
    batch, seq, _ = x.shape
    x2d = x.reshape(batch * seq, D_MODEL)
    cm, sm, cosr, sinr = _rope_tables(seq)
    rq, rk, rv, rg, q, k, v = _inproj(x2d, seq, wts["w_all"], wts["wq2"], wts["wkv"],
                                      wts["qnw"], wts["kvnw"], cm, sm)
    ret = _retention(rq, rk, rv, rg, cosr, sinr, wts["dfw"], wts["dbw"], wts["gnw"], batch, seq)
    att = _attention(q, k, v, batch, seq)
    return _outproj(x2d, ret, att, wts["wo1"], wts["wo2"], wts["onw"], wts["ln1w"], wts["ln1b"],
                    wts["wrt"])


def _moe_ln(y1, logits, wts):
    n = y1.shape[0]
    cap = EC_CAPACITY_FACTOR * n // N_EXPERTS
    assert cap % TM_FFN == 0, (n, cap)
    cap_rows = cap + TM_FFN
    rank_em, gate_em, cnt_rep, off_rep = _route(logits, cap)
    cnt = cnt_rep[:, :, 0].astype(jnp.int32).reshape(-1)
    off = off_rep[:, :, 0].astype(jnp.int32).reshape(-1)
    to_tm = lambda a: jnp.transpose(a, (0, 2, 1)).reshape(n, N_EXPERTS)
    xg = _compact(off, cnt, y1, rank_em, cap, cap_rows)
    yh = _ffn(xg, wts["wg"], wts["wu"], wts["wd"], cap, cap_rows)
    return _combine(off, cnt, y1, to_tm(rank_em), to_tm(gate_em), yh, wts["ln2w"], wts["ln2b"],
                    cap_rows)


def _layer(x, wts):
    batch, seq, _ = x.shape
    y1, logits = _mixer_ln(x, wts)
    return _moe_ln(y1, logits, wts).reshape(batch, seq, D_MODEL)


def kernel(x_prompt, x_sample, w_in, ret_decay_fwd, ret_decay_bwd, ret_gn_w, mla_q_norm_w, mla_w_uq,
           mla_kv_norm_w, mla_w_ukv, mla_out_norm_w, w_out, ln1_w, ln1_b, w_router, w_gate, w_up,
           w_down, ln2_w, ln2_b):
    depth = w_in.shape[0]
    y_prompt, y_sample = x_prompt, x_sample
    for l in range(depth):
        w_all, wq2, wkv = _prep_weights(w_in[l], mla_w_uq[l], mla_w_ukv[l])
        rep = lambda a: jnp.broadcast_to(a.astype(F32)[:, None, None], (RET_HEADS, 1, LANES))
        wts = dict(
            w_all=w_all, wq2=wq2, wkv=wkv,
            qnw=mla_q_norm_w[l][None, :], kvnw=mla_kv_norm_w[l][None, :],
            dfw=rep(ret_decay_fwd[l]), dbw=rep(ret_decay_bwd[l]), gnw=ret_gn_w[l][None, :],
            wo1=w_out[l][:RET_WIDTH].astype(BF16), wo2=w_out[l][RET_WIDTH:].astype(BF16),
            onw=mla_out_norm_w[l][None, :], ln1w=ln1_w[l][None, :], ln1b=ln1_b[l][None, :],
            wrt=w_router[l].T,
            wg=w_gate[l], wu=w_up[l], wd=w_down[l],
            ln2w=ln2_w[l][None, :], ln2b=ln2_b[l][None, :])
        y_prompt = _layer(y_prompt, wts)
        y_sample = _layer(y_sample, wts)
    return (y_prompt, y_sample)
```
